```python
import jax, jax.numpy as jnp
from jax import lax
import numpy as np

D_MODEL = 1024
BATCH = 8
SEQ = 4096
DEPTH = 1

GRID_W = 64
CTX_LEN = 256
D_MIX = D_MODEL
RET_HEADS = 4
RET_DK = 128
RET_DV = 128
RET_W = RET_HEADS * RET_DV
RET_CHUNK = 128
ATT_HEADS = 8
ATT_KV_HEADS = 2
ATT_DH = 64
ATT_W = ATT_HEADS * ATT_DH
ATT_BLOCK = 128
WINDOW = 128
ROPE_BASE = 10000.0
EPS = 1e-6
NEG = -1e30
SPLIT_SIZES = (RET_HEADS * RET_DK, RET_HEADS * RET_DK, RET_W, RET_W,
               ATT_W, ATT_KV_HEADS * ATT_DH, ATT_KV_HEADS * ATT_DH, ATT_W)
IN_COLS = 4 * RET_W + 2 * ATT_W + 2 * ATT_KV_HEADS * ATT_DH

kernel_name = "hymba_retention_swa_sink_prefix_dit"


def rms_norm(x):
    x32 = x.astype(jnp.float32)
    return (x32 * lax.rsqrt(jnp.mean(x32 * x32, axis=-1, keepdims=True) + EPS)).astype(x.dtype)


def split_heads(t, n_heads):
    B, L, _ = t.shape
    return t.reshape(B, L, n_heads, -1).transpose(0, 2, 1, 3)


def merge_heads(t):
    B, H, L, d = t.shape
    return t.transpose(0, 2, 1, 3).reshape(B, L, H * d)


def project(h, w_in_l):
    y = h @ w_in_l
    idx = [int(i) for i in np.cumsum(SPLIT_SIZES)[:-1]]
    rq, rk, rv, rg, aq, ak, av, ag = jnp.split(y, idx, axis=-1)
    rq = split_heads(rq, RET_HEADS)
    rk = split_heads(rk, RET_HEADS) * (RET_DK ** -0.5)
    rv = split_heads(rv, RET_HEADS)
    aq = split_heads(aq, ATT_HEADS)
    ak = split_heads(ak, ATT_KV_HEADS)
    av = split_heads(av, ATT_KV_HEADS)
    return rq, rk, rv, rg, aq, ak, av, ag


def axial_rope(x, rows, cols):
    dh = x.shape[-1]
    half = dh // 2
    nf = half // 2
    inv = ROPE_BASE ** (-jnp.arange(nf, dtype=jnp.float32) / nf)
    ang = jnp.concatenate([rows.astype(jnp.float32)[:, None] * inv,
                           cols.astype(jnp.float32)[:, None] * inv], axis=-1)
    cos, sin = jnp.cos(ang), jnp.sin(ang)
    x1, x2 = x[..., :half], x[..., half:]
    return jnp.concatenate([x1 * cos - x2 * sin, x1 * sin + x2 * cos], axis=-1).astype(x.dtype)


def retention_chunked(q, k, v, log_gamma, state0, strict):
    B, H, L, _ = q.shape
    C = RET_CHUNK
    n = L // C

    def chunks(t):
        return t.astype(jnp.float32).reshape(B, H, n, C, t.shape[-1]).transpose(2, 0, 1, 3, 4)

    pos = jnp.arange(C, dtype=jnp.float32)
    diff = pos[:, None] - pos[None, :]
    mask = (diff > 0) if strict else (diff >= 0)
    decay_in = jnp.where(mask[None], jnp.exp(log_gamma[:, None, None] * jnp.where(mask, diff, 0.0)[None]), 0.0)
    decay_q = jnp.exp(log_gamma[:, None] * (pos + 1.0))[None, :, :, None]
    decay_k = jnp.exp(log_gamma[:, None] * (C - 1.0 - pos))[None, :, :, None]
    decay_c = jnp.exp(log_gamma * C)[None, :, None, None]

    def step(R, inp):
        qc, kc, vc = inp
        s = jnp.einsum('bhid,bhjd->bhij', qc, kc) * decay_in
        inner = jnp.einsum('bhij,bhje->bhie', s, vc)
        cross = jnp.einsum('bhid,bhde->bhie', qc, R) * decay_q
        R = R * decay_c + jnp.einsum('bhjd,bhje->bhde', kc * decay_k, vc)
        return R, inner + cross

    R, out = lax.scan(step, state0, (chunks(q), chunks(k), chunks(v)))
    out = out.transpose(1, 2, 0, 3, 4).reshape(B, H, L, -1)
    return out, R


def retention_final_state(k, v, log_gamma, reverse):
    L = k.shape[2]
    pos = jnp.arange(L, dtype=jnp.float32)
    expo = pos if reverse else (L - 1.0 - pos)
    w = jnp.exp(log_gamma[:, None] * expo)[None, :, :, None]
    return jnp.einsum('bhld,bhle->bhde', k.astype(jnp.float32) * w, v.astype(jnp.float32))


def retention_branch(q, k, v, g, log_gamma, gn_w, s_fwd, s_bwd):
    out_f, fin_f = retention_chunked(q, k, v, log_gamma[0], s_fwd, strict=False)
    out_b, fin_b = retention_chunked(jnp.flip(q, 2), jnp.flip(k, 2), jnp.flip(v, 2),
                                     log_gamma[1], s_bwd, strict=True)
    y = out_f + jnp.flip(out_b, 2)
    mu = jnp.mean(y, axis=-1, keepdims=True)
    var = jnp.mean(jnp.square(y - mu), axis=-1, keepdims=True)
    y = (y - mu) * lax.rsqrt(var + EPS)
    y = merge_heads(y).astype(g.dtype) * gn_w
    return y * jax.nn.silu(g), fin_f, fin_b


def sink_softmax(logits, sink):
    m = jnp.maximum(jnp.max(logits, axis=-1, keepdims=True), sink)
    p = jnp.exp(logits - m)
    return p / (jnp.sum(p, axis=-1, keepdims=True) + jnp.exp(sink - m))


def band_blocks(t):
    B, K, L, d = t.shape
    nb = L // ATT_BLOCK
    tp = jnp.pad(t, ((0, 0), (0, 0), (ATT_BLOCK, ATT_BLOCK), (0, 0))).reshape(B, K, nb + 2, ATT_BLOCK, d)
    return jnp.concatenate([tp[:, :, 0:nb], tp[:, :, 1:nb + 1], tp[:, :, 2:nb + 2]], axis=3)


def windowed_gqa(q, k, v, k_ctx, v_ctx, sink):
    B, Hq, L, dh = q.shape
    Hkv = k.shape[1]
    G = Hq // Hkv
    nb = L // ATT_BLOCK
    Lc = k_ctx.shape[2]
    qb = q.astype(jnp.float32).reshape(B, Hkv, G, nb, ATT_BLOCK, dh) * (dh ** -0.5)
    kb = band_blocks(k.astype(jnp.float32))
    vb = band_blocks(v.astype(jnp.float32))
    qpos = jnp.arange(nb)[:, None, None] * ATT_BLOCK + jnp.arange(ATT_BLOCK)[None, :, None]
    kpos = jnp.arange(nb)[:, None, None] * ATT_BLOCK - ATT_BLOCK + jnp.arange(3 * ATT_BLOCK)[None, None, :]
    valid = (jnp.abs(qpos - kpos) <= WINDOW) & (kpos >= 0) & (kpos < L)
    s_loc = jnp.where(valid, jnp.einsum('bkgnqd,bknjd->bkgnqj', qb, kb), NEG)
    s_ctx = jnp.einsum('bkgnqd,bkcd->bkgnqc', qb, k_ctx.astype(jnp.float32))
    p = sink_softmax(jnp.concatenate([s_ctx, s_loc], axis=-1),
                     sink.astype(jnp.float32).reshape(1, Hkv, G, 1, 1, 1))
    out = (jnp.einsum('bkgnqc,bkcd->bkgnqd', p[..., :Lc], v_ctx.astype(jnp.float32))
           + jnp.einsum('bkgnqj,bknjd->bkgnqd', p[..., Lc:], vb))
    return out.reshape(B, Hq, L, dh).astype(q.dtype)


def context_gqa(q, k, v, sink):
    B, Hq, Lc, dh = q.shape
    Hkv = k.shape[1]
    G = Hq // Hkv
    qg = q.astype(jnp.float32).reshape(B, Hkv, G, Lc, dh) * (dh ** -0.5)
    s = jnp.einsum('bkgqd,bkcd->bkgqc', qg, k.astype(jnp.float32))
    p = sink_softmax(s, sink.astype(jnp.float32).reshape(1, Hkv, G, 1, 1))
    out = jnp.einsum('bkgqc,bkcd->bkgqd', p, v.astype(jnp.float32))
    return out.reshape(B, Hq, Lc, dh).astype(q.dtype)


def setup_inputs(seed: int = 0) -> dict:
    key = jax.random.key(seed)
    ks = jax.random.split(key, 12)
    f32 = jnp.float32
    x = jax.random.normal(ks[0], (BATCH, SEQ, D_MODEL), f32)
    c = jax.random.normal(ks[1], (BATCH, D_MODEL), f32)
    ctx = jax.random.normal(ks[2], (BATCH, CTX_LEN, D_MODEL), f32)
    c_ctx = jax.random.normal(ks[3], (D_MODEL,), f32)
    w_ada = jax.random.normal(ks[4], (DEPTH, D_MODEL, 3 * D_MODEL), f32) * (0.5 * D_MODEL ** -0.5)
    b_ada = jax.random.normal(ks[5], (DEPTH, 3 * D_MODEL), f32) * 0.02
    w_in = jax.random.normal(ks[6], (DEPTH, D_MODEL, IN_COLS), f32) * (D_MODEL ** -0.5)
    base_logit = jnp.log(2.0 ** (5.0 + jnp.arange(RET_HEADS, dtype=f32)) - 1.0)
    ret_decay_logit = base_logit[None, None, :] + 0.1 * jax.random.normal(ks[7], (DEPTH, 2, RET_HEADS), f32)
    ret_gn_w = 1.0 + 0.02 * jax.random.normal(ks[8], (DEPTH, RET_W), f32)
    att_sink = 0.5 * jax.random.normal(ks[9], (DEPTH, ATT_HEADS), f32)
    w_out = jax.random.normal(ks[10], (DEPTH, D_MIX, D_MODEL), f32) * (D_MIX ** -0.5)
    final_norm_w = 1.0 + 0.02 * jax.random.normal(ks[11], (D_MODEL,), f32)
    return {"x": x, "c": c, "ctx": ctx, "c_ctx": c_ctx, "w_ada": w_ada, "b_ada": b_ada,
            "w_in": w_in, "ret_decay_logit": ret_decay_logit, "ret_gn_w": ret_gn_w,
            "att_sink": att_sink, "w_out": w_out, "final_norm_w": final_norm_w}


def reference(x, c, ctx, c_ctx, w_ada, b_ada, w_in, ret_decay_logit, ret_gn_w, att_sink, w_out, final_norm_w):
    B, L, _ = x.shape
    ROWS = L // GRID_W
    rows = jnp.broadcast_to(jnp.arange(ROWS, dtype=jnp.int32)[:, None], (ROWS, GRID_W)).reshape(-1)
    cols = jnp.broadcast_to(jnp.arange(GRID_W, dtype=jnp.int32)[None, :], (ROWS, GRID_W)).reshape(-1)

    for l in range(DEPTH):
        shift, scale, gate = jnp.split(jax.nn.silu(c) @ w_ada[l] + b_ada[l], 3, axis=-1)
        shift_c, scale_c, gate_c = jnp.split(jax.nn.silu(c_ctx) @ w_ada[l] + b_ada[l], 3, axis=-1)
        h = rms_norm(x) * (1.0 + scale[:, None, :]) + shift[:, None, :]
        hc = rms_norm(ctx) * (1.0 + scale_c) + shift_c

        rq, rk, rv, rg, aq, ak, av, ag = project(h, w_in[l])
        crq, crk, crv, crg, caq, cak, cav, cag = project(hc, w_in[l])
        rq, rk = axial_rope(rq, rows, cols), axial_rope(rk, rows, cols)
        aq, ak = axial_rope(aq, rows, cols), axial_rope(ak, rows, cols)
        log_gamma = jax.nn.log_sigmoid(ret_decay_logit[l].astype(jnp.float32))

        if l < DEPTH - 1:
            zero = jnp.zeros((B, RET_HEADS, RET_DK, RET_DV), jnp.float32)
            ret_c, s_fwd, s_bwd = retention_branch(crq, crk, crv, crg, log_gamma, ret_gn_w[l], zero, zero)
            att_c = merge_heads(context_gqa(caq, cak, cav, att_sink[l])) * jax.nn.silu(cag)
            ctx_next = ctx + gate_c * (jnp.concatenate([ret_c, att_c], axis=-1) @ w_out[l])
        else:
            s_fwd = retention_final_state(crk, crv, log_gamma[0], reverse=False)
            s_bwd = retention_final_state(crk, crv, log_gamma[1], reverse=True)
            ctx_next = ctx

        ret_x, _, _ = retention_branch(rq, rk, rv, rg, log_gamma, ret_gn_w[l], s_fwd, s_bwd)
        att_x = merge_heads(windowed_gqa(aq, ak, av, cak, cav, att_sink[l])) * jax.nn.silu(ag)
        mixed = jnp.concatenate([ret_x, att_x], axis=-1) @ w_out[l]
        x = x + gate[:, None, :] * mixed
        ctx = ctx_next

    return rms_norm(x) * final_norm_w
```

```python
import functools

import numpy as np
import jax
import jax.numpy as jnp
from jax import lax
from jax.experimental import pallas as pl
from jax.experimental.pallas import tpu as pltpu

F32 = jnp.float32
BF16 = jnp.bfloat16

D_MODEL = 1024
GRID_W = 64
RET_HEADS = 4
RET_D = 128
RET_W = RET_HEADS * RET_D
ATT_HEADS = 8
ATT_KV = 2
ATT_DH = 64
ATT_W = ATT_HEADS * ATT_DH
ATT_BLOCK = 128
WINDOW = 128
ROPE_BASE = 10000.0
EPS = 1e-6
NEG = -1e30
IN_COLS = 4 * RET_W + 2 * ATT_W + 2 * ATT_KV * ATT_DH
OFF_RQ, OFF_RK, OFF_RV, OFF_RG = 0, RET_W, 2 * RET_W, 3 * RET_W
OFF_AQ = 4 * RET_W
OFF_AK = OFF_AQ + ATT_W
OFF_AV = OFF_AK + ATT_KV * ATT_DH
OFF_AG = OFF_AV + ATT_KV * ATT_DH

LANES = 128
VMEM_LIMIT = 48 * 1024 * 1024

PROJ_ROWS = 512
RET_CHUNK = 256
ATT_ROWS = 512


def _att_pair_perm():
    p = np.arange(ATT_W)
    j, half, d = p // LANES, (p % LANES) // ATT_DH, p % ATT_DH
    return (j + 4 * half) * ATT_DH + d


def _silu(t):
    return t / (1.0 + jnp.exp(-t))


def _log_sigmoid(t):
    return -(jnp.maximum(-t, 0.0) + jnp.log1p(jnp.exp(-jnp.abs(t))))


def _mod_kernel(c_ref, w_ref, b_ref, o_ref):
    s = _silu(c_ref[...])
    o_ref[...] = jnp.dot(s, w_ref[...], preferred_element_type=F32,
                         precision=lax.Precision.HIGHEST) + b_ref[...]


def _modulation(cc, w_ada, b_ada):
    n = w_ada.shape[1]
    bn = 512
    return pl.pallas_call(
        _mod_kernel,
        out_shape=jax.ShapeDtypeStruct((cc.shape[0], n), F32),
        grid=(n // bn,),
        in_specs=[pl.BlockSpec(cc.shape, lambda i: (0, 0)),
                  pl.BlockSpec((D_MODEL, bn), lambda i: (0, i)),
                  pl.BlockSpec((1, bn), lambda i: (0, i))],
        out_specs=pl.BlockSpec((cc.shape[0], bn), lambda i: (0, i)),
        compiler_params=pltpu.CompilerParams(dimension_semantics=("arbitrary",)),
        name="mod",
    )(cc, w_ada, b_ada)


def _norm_mod(x, m_ref):
    shift = m_ref[0, :, 0:D_MODEL]
    scale = m_ref[0, :, D_MODEL:2 * D_MODEL]
    h = x * lax.rsqrt(jnp.mean(x * x, axis=-1, keepdims=True) + EPS)
    return h * (1.0 + scale) + shift


def _rope_ret(t, cos, sin):
    return t * cos + pltpu.roll(t, 64, 1) * sin


def _rope_att(t, cos, sin, low_half):
    partner = jnp.where(low_half, pltpu.roll(t, 96, 1), pltpu.roll(t, 32, 1))
    return t * cos + partner * sin


def _proj_kernel(x_ref, m_ref, w_ref, cr_ref, sr_ref, ca_ref, sa_ref,
                 r_ref, aq_ref, ak_ref, av_ref, ag_ref):
    hb = _norm_mod(x_ref[0], m_ref).astype(BF16)
    cr, sr, ca, sa = cr_ref[...], sr_ref[...], ca_ref[...], sa_ref[...]
    low_half = (lax.broadcasted_iota(jnp.int32, ca.shape, 1) % ATT_DH) < (ATT_DH // 2)

    def mm(c0, c1):
        return jnp.dot(hb, w_ref[:, c0:c1], preferred_element_type=F32)

    y = mm(OFF_RQ, OFF_RK)
    for h in range(RET_HEADS):
        sl = slice(h * LANES, (h + 1) * LANES)
        r_ref[0, :, OFF_RQ + h * LANES:OFF_RQ + (h + 1) * LANES] = _rope_ret(y[:, sl], cr, sr).astype(BF16)
    y = mm(OFF_RK, OFF_RV) * (RET_D ** -0.5)
    for h in range(RET_HEADS):
        sl = slice(h * LANES, (h + 1) * LANES)
        r_ref[0, :, OFF_RK + h * LANES:OFF_RK + (h + 1) * LANES] = _rope_ret(y[:, sl], cr, sr).astype(BF16)
    r_ref[0, :, OFF_RV:OFF_RG + RET_W] = mm(OFF_RV, OFF_AQ).astype(BF16)
    y = mm(OFF_AQ, OFF_AK)
    for j in range(ATT_W // LANES):
        sl = slice(j * LANES, (j + 1) * LANES)
        aq_ref[0, :, sl] = (_rope_att(y[:, sl], ca, sa, low_half) * (ATT_DH ** -0.5)).astype(BF16)
    y = mm(OFF_AK, OFF_AG)
    ak_ref[0] = _rope_att(y[:, 0:LANES], ca, sa, low_half).astype(BF16)
    av_ref[0] = y[:, LANES:2 * LANES].astype(BF16)
    ag_ref[0] = mm(OFF_AG, IN_COLS).astype(BF16)


def _project(x, mod3, w_in_b, tabs):
    B, L, _ = x.shape
    tm = PROJ_ROWS
    tab_spec = pl.BlockSpec((tm, LANES), lambda b, i: (i, 0))
    row = lambda n: pl.BlockSpec((1, tm, n), lambda b, i: (b, i, 0))
    return pl.pallas_call(
        _proj_kernel,
        out_shape=(jax.ShapeDtypeStruct((B, L, 4 * RET_W), BF16),
                   jax.ShapeDtypeStruct((B, L, ATT_W), BF16),
                   jax.ShapeDtypeStruct((B, L, LANES), BF16),
                   jax.ShapeDtypeStruct((B, L, LANES), BF16),
                   jax.ShapeDtypeStruct((B, L, ATT_W), BF16)),
        grid=(B, L // tm),
        in_specs=[row(D_MODEL),
                  pl.BlockSpec((1, 1, 3 * D_MODEL), lambda b, i: (b, 0, 0)),
                  pl.BlockSpec((D_MODEL, IN_COLS), lambda b, i: (0, 0)),
                  tab_spec, tab_spec, tab_spec, tab_spec],
        out_specs=(row(4 * RET_W), row(ATT_W), row(LANES), row(LANES), row(ATT_W)),
        compiler_params=pltpu.CompilerParams(
            dimension_semantics=("arbitrary", "arbitrary"), vmem_limit_bytes=VMEM_LIMIT),
        name="proj",
    )(x, mod3, w_in_b, *tabs)


def _ctx_kernel(x_ref, m_ref, w_ref, dl_ref, sf_ref, sb_ref, ck_ref, cv_ref):
    lc = x_ref.shape[1]
    hb = _norm_mod(x_ref[0], m_ref).astype(BF16)
    y = jnp.dot(hb, w_ref[...], preferred_element_type=F32)
    ck_ref[0] = y[:, 2 * RET_W:2 * RET_W + LANES].astype(BF16)
    cv_ref[0] = y[:, 2 * RET_W + LANES:2 * RET_W + 2 * LANES].astype(BF16)
    lg = _log_sigmoid(dl_ref[...])
    pos = lax.broadcasted_iota(jnp.int32, (lc, 1), 0).astype(F32)
    for h in range(RET_HEADS):
        k = y[:, h * RET_D:(h + 1) * RET_D] * (RET_D ** -0.5)
        v = y[:, RET_W + h * RET_D:RET_W + (h + 1) * RET_D].astype(BF16)
        wf = jnp.exp(lg[0, h, 0:1, :] * (lc - 1.0 - pos))
        wb = jnp.exp(lg[1, h, 0:1, :] * pos)
        dn = (((0,), (0,)), ((), ()))
        sf_ref[0, h] = lax.dot_general((k * wf).astype(BF16), v, dn, preferred_element_type=F32)
        sb_ref[0, h] = lax.dot_general((k * wb).astype(BF16), v, dn, preferred_element_type=F32)


def _context(ctx, mod3, w_ctx_b, dl):
    B, lc, _ = ctx.shape
    ncol = w_ctx_b.shape[1]
    st = jax.ShapeDtypeStruct((B, RET_HEADS, RET_D, RET_D), F32)
    kv = jax.ShapeDtypeStruct((B, lc, LANES), BF16)
    st_spec = pl.BlockSpec((1, RET_HEADS, RET_D, RET_D), lambda b: (b, 0, 0, 0))
    kv_spec = pl.BlockSpec((1, lc, LANES), lambda b: (b, 0, 0))
    ctx_row = mod3.shape[0] // 2
    return pl.pallas_call(
        _ctx_kernel,
        out_shape=(st, st, kv, kv),
        grid=(B,),
        in_specs=[pl.BlockSpec((1, lc, D_MODEL), lambda b: (b, 0, 0)),
                  pl.BlockSpec((1, 1, 3 * D_MODEL), lambda b: (ctx_row, 0, 0)),
                  pl.BlockSpec((D_MODEL, ncol), lambda b: (0, 0)),
                  pl.BlockSpec(dl.shape, lambda b: (0, 0, 0, 0))],
        out_specs=(st_spec, st_spec, kv_spec, kv_spec),
        compiler_params=pltpu.CompilerParams(
            dimension_semantics=("arbitrary",), vmem_limit_bytes=VMEM_LIMIT),
        name="ctx",
    )(ctx, mod3, w_ctx_b, dl)


def _ret_kernel(q_ref, k_ref, v_ref, g_ref, sf_ref, sb_ref, dl_ref, gn_ref, o_ref, rb_scr):
    L = q_ref.shape[1]
    C = RET_CHUNK
    nch = L // C
    lg = _log_sigmoid(dl_ref[...])
    lgf, lgb = lg[0, 0, 0:1, :], lg[1, 0, 0:1, :]
    lgf1, lgb1 = lgf[:, 0:1], lgb[:, 0:1]
    pos = lax.broadcasted_iota(jnp.int32, (C, 1), 0).astype(F32)
    kf_dec = jnp.exp(lgf * (C - 1.0 - pos))
    kb_dec = jnp.exp(lgb * pos)
    qf_dec = jnp.exp(lgf * (pos + 1.0))
    qb_dec = jnp.exp(lgb * (C - pos))
    cf, cb = jnp.exp(lgf * C), jnp.exp(lgb * C)
    diff = (lax.broadcasted_iota(jnp.int32, (C, C), 0)
            - lax.broadcasted_iota(jnp.int32, (C, C), 1)).astype(F32)
    decay = jnp.where(diff >= 0.0, jnp.exp(lgf1 * jnp.maximum(diff, 0.0)),
                      jnp.exp(lgb1 * jnp.maximum(-diff, 0.0)))
    tn = (((0,), (0,)), ((), ()))
    nt = (((1,), (1,)), ((), ()))

    def state_update(state, kc, vc, kdec, cdec):
        kd = (kc.astype(F32) * kdec).astype(BF16)
        return state * cdec + lax.dot_general(kd, vc, tn, preferred_element_type=F32)

    def back(t, rb):
        n = nch - 1 - t
        rows = pl.ds(pl.multiple_of(n * C, C), C)
        rb_scr[n] = rb.astype(BF16)
        return state_update(rb, k_ref[0, rows, :], v_ref[0, rows, :], kb_dec, cb)

    lax.fori_loop(0, nch, back, sb_ref[0, 0])

    gn = gn_ref[0]

    def fwd(n, rf):
        rows = pl.ds(pl.multiple_of(n * C, C), C)
        qc, kc, vc = q_ref[0, rows, :], k_ref[0, rows, :], v_ref[0, rows, :]
        s = lax.dot_general(qc, kc, nt, preferred_element_type=F32)
        inner = jnp.dot((s * decay).astype(BF16), vc, preferred_element_type=F32)
        states = jnp.concatenate([rf.astype(BF16), rb_scr[n]], axis=1)
        cross = jnp.dot(qc, states, preferred_element_type=F32)
        y = inner + cross[:, 0:RET_D] * qf_dec + cross[:, RET_D:2 * RET_D] * qb_dec
        mu = jnp.mean(y, axis=-1, keepdims=True)
        yc = y - mu
        var = jnp.mean(yc * yc, axis=-1, keepdims=True)
        yn = yc * lax.rsqrt(var + EPS) * gn
        o_ref[0, rows, :] = (yn * _silu(g_ref[0, rows, :].astype(F32))).astype(BF16)
        return state_update(rf, kc, vc, kf_dec, cf)

    lax.fori_loop(0, nch, fwd, sf_ref[0, 0])


def _retention(r, sf, sb, dl, gn):
    B, L, _ = r.shape
    col = lambda off: pl.BlockSpec((1, L, RET_D), lambda b, h: (b, 0, off + h))
    st_spec = pl.BlockSpec((1, 1, RET_D, RET_D), lambda b, h: (b, h, 0, 0))
    return pl.pallas_call(
        _ret_kernel,
        out_shape=jax.ShapeDtypeStruct((B, L, RET_W), BF16),
        grid=(B, RET_HEADS),
        in_specs=[col(0), col(RET_HEADS), col(2 * RET_HEADS), col(3 * RET_HEADS),
                  st_spec, st_spec,
                  pl.BlockSpec((2, 1, 8, LANES), lambda b, h: (0, h, 0, 0)),
                  pl.BlockSpec((1, 1, RET_D), lambda b, h: (h, 0, 0))],
        out_specs=pl.BlockSpec((1, L, RET_D), lambda b, h: (b, 0, h)),
        scratch_shapes=[pltpu.VMEM((L // RET_CHUNK, RET_D, RET_D), BF16)],
        compiler_params=pltpu.CompilerParams(
            dimension_semantics=("arbitrary", "arbitrary"), vmem_limit_bytes=VMEM_LIMIT),
        name="ret",
    )(r, r, r, r, sf, sb, dl, gn)


def _att_kernel(q_ref, k_ref, v_ref, ck_ref, cv_ref, g_ref, sink_ref, o_ref):
    L = k_ref.shape[1]
    nb_total = L // ATT_BLOCK
    nb_step = q_ref.shape[1] // ATT_BLOCK
    lc = ck_ref.shape[1]
    i = pl.program_id(1)
    lane = lax.broadcasted_iota(jnp.int32, (1, LANES), 1)
    low = lane < ATT_DH
    r_i = lax.broadcasted_iota(jnp.int32, (3 * ATT_BLOCK, ATT_BLOCK), 0)
    q_i = lax.broadcasted_iota(jnp.int32, (3 * ATT_BLOCK, ATT_BLOCK), 1)
    band = jnp.where((q_i <= r_i) & (q_i >= r_i - 2 * WINDOW), 0.0, NEG).astype(F32)
    band = jnp.concatenate([band] * (ATT_W // LANES), axis=1)
    ck, cv = ck_ref[0], cv_ref[0]
    nt = (((1,), (1,)), ((), ()))
    tn = (((0,), (0,)), ((), ()))

    for jb in range(nb_step):
        n = i * nb_step + jb
        rows = slice(jb * ATT_BLOCK, (jb + 1) * ATT_BLOCK)
        q = q_ref[0, rows, :]
        qall = jnp.concatenate([q[:, j * LANES:(j + 1) * LANES] for j in range(ATT_W // LANES)], axis=0)
        p_start = pl.multiple_of(jnp.maximum(n - 1, 0) * ATT_BLOCK, ATT_BLOCK)
        o_start = pl.multiple_of(n * ATT_BLOCK, ATT_BLOCK)
        n_start = pl.multiple_of(jnp.minimum(n + 1, nb_total - 1) * ATT_BLOCK, ATT_BLOCK)
        kall = jnp.concatenate([ck, k_ref[0, pl.ds(p_start, ATT_BLOCK), :],
                                k_ref[0, pl.ds(o_start, ATT_BLOCK), :],
                                k_ref[0, pl.ds(n_start, ATT_BLOCK), :]], axis=0)
        vall = jnp.concatenate([cv, v_ref[0, pl.ds(p_start, ATT_BLOCK), :],
                                v_ref[0, pl.ds(o_start, ATT_BLOCK), :],
                                v_ref[0, pl.ds(n_start, ATT_BLOCK), :]], axis=0)
        edge = jnp.concatenate([
            jnp.full((ATT_BLOCK, 1), jnp.where(n == 0, NEG, 0.0), F32),
            jnp.zeros((ATT_BLOCK, 1), F32),
            jnp.full((ATT_BLOCK, 1), jnp.where(n == nb_total - 1, NEG, 0.0), F32)], axis=0)
        bias = band + edge

        halves = []
        for kv in range(ATT_KV):
            keep = low if kv == 0 else jnp.logical_not(low)
            kk = jnp.where(keep, kall, jnp.zeros_like(kall))
            s = lax.dot_general(kk, qall, nt, preferred_element_type=F32)
            s = jnp.concatenate([s[0:lc], s[lc:] + bias], axis=0)
            sink = sink_ref[kv:kv + 1, :]
            m = jnp.maximum(jnp.max(s, axis=0, keepdims=True), sink)
            p = jnp.exp(s - m)
            den = jnp.sum(p, axis=0, keepdims=True) + jnp.exp(sink - m)
            o = lax.dot_general(vall, p.astype(BF16), tn, preferred_element_type=F32)
            o = o / den
            halves.append(o[0:ATT_DH] if kv == 0 else o[ATT_DH:2 * ATT_DH])
        comb = jnp.concatenate(halves, axis=0)
        for j in range(ATT_W // LANES):
            sl = slice(j * LANES, (j + 1) * LANES)
            oj = comb[:, sl].T
            o_ref[0, rows, sl] = (oj * _silu(g_ref[0, rows, sl].astype(F32))).astype(BF16)


def _attention(aq, ak, av, ck, cv, ag, sink2):
    B, L, _ = aq.shape
    lc = ck.shape[1]
    tq = ATT_ROWS
    qrow = pl.BlockSpec((1, tq, ATT_W), lambda b, i: (b, i, 0))
    full = lambda n: pl.BlockSpec((1, n, LANES), lambda b, i: (b, 0, 0))
    return pl.pallas_call(
        _att_kernel,
        out_shape=jax.ShapeDtypeStruct((B, L, ATT_W), BF16),
        grid=(B, L // tq),
        in_specs=[qrow, full(L), full(L), full(lc), full(lc), qrow,
                  pl.BlockSpec(sink2.shape, lambda b, i: (0, 0))],
        out_specs=qrow,
        compiler_params=pltpu.CompilerParams(
            dimension_semantics=("arbitrary", "arbitrary"), vmem_limit_bytes=VMEM_LIMIT),
        name="att",
    )(aq, ak, av, ck, cv, ag, sink2)


def _out_kernel(r_ref, a_ref, x_ref, m_ref, wr_ref, wa_ref, fn_ref, o_ref):
    mixed = (jnp.dot(r_ref[0], wr_ref[...], preferred_element_type=F32)
             + jnp.dot(a_ref[0], wa_ref[...], preferred_element_type=F32))
    gate = m_ref[0, :, 2 * D_MODEL:3 * D_MODEL]
    xn = x_ref[0] + gate * mixed
    o_ref[0] = xn * lax.rsqrt(jnp.mean(xn * xn, axis=-1, keepdims=True) + EPS) * fn_ref[...]


def _output(ret, att, x, mod3, w_r, w_a, fnw):
    B, L, _ = x.shape
    tm = PROJ_ROWS
    row = lambda n: pl.BlockSpec((1, tm, n), lambda b, i: (b, i, 0))
    wspec = pl.BlockSpec((RET_W, D_MODEL), lambda b, i: (0, 0))
    return pl.pallas_call(
        _out_kernel,
        out_shape=jax.ShapeDtypeStruct((B, L, D_MODEL), F32),
        grid=(B, L // tm),
        in_specs=[row(RET_W), row(ATT_W), row(D_MODEL),
                  pl.BlockSpec((1, 1, 3 * D_MODEL), lambda b, i: (b, 0, 0)),
                  wspec, wspec, pl.BlockSpec((1, D_MODEL), lambda b, i: (0, 0))],
        out_specs=row(D_MODEL),
        compiler_params=pltpu.CompilerParams(
            dimension_semantics=("arbitrary", "arbitrary"), vmem_limit_bytes=VMEM_LIMIT),
        name="out",
    )(ret, att, x, mod3, w_r, w_a, fnw)


def _rope_tables(L):
    pos = np.arange(L)
    rows, cols = (pos // GRID_W).astype(np.float32), (pos % GRID_W).astype(np.float32)

    def tables(dh):
        nf = dh // 4
        inv = jnp.asarray(ROPE_BASE, F32) ** (-jnp.arange(nf, dtype=F32) / nf)
        ang = jnp.concatenate([jnp.asarray(rows)[:, None] * inv, jnp.asarray(cols)[:, None] * inv], axis=-1)
        cos, sin = jnp.cos(ang), jnp.sin(ang)
        reps = LANES // dh
        return (jnp.tile(jnp.concatenate([cos, cos], axis=-1), (1, reps)),
                jnp.tile(jnp.concatenate([-sin, sin], axis=-1), (1, reps)))

    cr, sr = tables(RET_D)
    ca, sa = tables(ATT_DH)
    return cr, sr, ca, sa


def kernel(x, c, ctx, c_ctx, w_ada, b_ada, w_in, ret_decay_logit, ret_gn_w, att_sink, w_out, final_norm_w):
    B, L, _ = x.shape
    assert w_ada.shape[0] == 1, "single-layer trunk"
    perm = _att_pair_perm()

    cc = jnp.zeros((2 * B, D_MODEL), F32).at[:B].set(c).at[B].set(c_ctx)
    mod3 = _modulation(cc, w_ada[0], b_ada[0][None, :]).reshape(2 * B, 1, 3 * D_MODEL)

    w = w_in[0]
    w_perm = jnp.concatenate([w[:, :OFF_AQ], w[:, OFF_AQ + perm], w[:, OFF_AK:OFF_AG],
                              w[:, OFF_AG + perm]], axis=1).astype(BF16)
    w_ctx = jnp.concatenate([w[:, OFF_RK:OFF_RG], w[:, OFF_AK:OFF_AG]], axis=1).astype(BF16)
    dl = jnp.broadcast_to(ret_decay_logit[0].astype(F32)[:, :, None, None], (2, RET_HEADS, 8, LANES))

    r, aq, ak, av, ag = _project(x, mod3, w_perm, _rope_tables(L))
    sf, sb, ck, cv = _context(ctx, mod3, w_ctx, dl)
    ret = _retention(r, sf, sb, dl, ret_gn_w[0].reshape(RET_HEADS, 1, RET_D))

    sink = att_sink[0].astype(F32)
    sink2 = jnp.zeros((8, ATT_W), F32).at[0].set(jnp.repeat(sink[:4], LANES)).at[1].set(jnp.repeat(sink[4:], LANES))
    att = _attention(aq, ak, av, ck, cv, ag, sink2)

    wo = w_out[0]
    return _output(ret, att, x, mod3, wo[:RET_W].astype(BF16), wo[RET_W + perm].astype(BF16),
                   final_norm_w[None, :])
```

```python
import functools

import numpy as np
import jax
import jax.numpy as jnp
from jax import lax
from jax.experimental import pallas as pl
from jax.experimental.pallas import tpu as pltpu

F32 = jnp.float32
BF16 = jnp.bfloat16

D_MODEL = 1024
GRID_W = 64
RET_HEADS = 4
RET_D = 128
RET_W = RET_HEADS * RET_D
ATT_HEADS = 8
ATT_KV = 2
ATT_DH = 64
ATT_W = ATT_HEADS * ATT_DH
ATT_BLOCK = 128
WINDOW = 128
ROPE_BASE = 10000.0
EPS = 1e-6
NEG = -1e30
LOG2E = 1.4426950408889634
IN_COLS = 4 * RET_W + 2 * ATT_W + 2 * ATT_KV * ATT_DH
OFF_RQ, OFF_RK, OFF_RV, OFF_RG = 0, RET_W, 2 * RET_W, 3 * RET_W
OFF_AQ = 4 * RET_W
OFF_AK = OFF_AQ + ATT_W
OFF_AV = OFF_AK + ATT_KV * ATT_DH
OFF_AG = OFF_AV + ATT_KV * ATT_DH

LANES = 128
VMEM_LIMIT = 48 * 1024 * 1024

PROJ_ROWS = 512
RET_CHUNK = 256
RET_UNROLL = 16
ATT_ROWS = 512


def _att_pair_perm():
    p = np.arange(ATT_W)
    j, half, d = p // LANES, (p % LANES) // ATT_DH, p % ATT_DH
    return (j + 4 * half) * ATT_DH + d


def _silu(t):
    return t / (1.0 + jnp.exp(-t))


def _log_sigmoid(t):
    return -(jnp.maximum(-t, 0.0) + jnp.log1p(jnp.exp(-jnp.abs(t))))


def _mod_kernel(c_ref, w_ref, b_ref, o_ref):
    s = _silu(c_ref[...])
    o_ref[...] = jnp.dot(s, w_ref[...], preferred_element_type=F32,
                         precision=lax.Precision.HIGHEST) + b_ref[...]


def _modulation(cc, w_ada, b_ada):
    n = w_ada.shape[1]
    bn = 512
    return pl.pallas_call(
        _mod_kernel,
        out_shape=jax.ShapeDtypeStruct((cc.shape[0], n), F32),
        grid=(n // bn,),
        in_specs=[pl.BlockSpec(cc.shape, lambda i: (0, 0)),
                  pl.BlockSpec((D_MODEL, bn), lambda i: (0, i)),
                  pl.BlockSpec((1, bn), lambda i: (0, i))],
        out_specs=pl.BlockSpec((cc.shape[0], bn), lambda i: (0, i)),
        compiler_params=pltpu.CompilerParams(dimension_semantics=("arbitrary",)),
        name="mod",
    )(cc, w_ada, b_ada)


def _norm_mod(x, m_ref):
    shift = m_ref[0, :, 0:D_MODEL]
    scale = m_ref[0, :, D_MODEL:2 * D_MODEL]
    h = x * lax.rsqrt(jnp.mean(x * x, axis=-1, keepdims=True) + EPS)
    return h * (1.0 + scale) + shift


def _rope_ret(t, cos, sin):
    return t * cos + pltpu.roll(t, 64, 1) * sin


def _rope_att(t, cos, sin, low_half):
    partner = jnp.where(low_half, pltpu.roll(t, 96, 1), pltpu.roll(t, 32, 1))
    return t * cos + partner * sin


def _store_kv_groups(k_ref, v_ref, k, v):
    low = lax.broadcasted_iota(jnp.int32, k.shape, 1) < ATT_DH
    k_ref[0, 0] = jnp.where(low, k, 0.0).astype(BF16)
    k_ref[0, 1] = jnp.where(low, 0.0, k).astype(BF16)
    v_ref[0, 0] = jnp.where(low, v, 1.0).astype(BF16)
    v_ref[0, 1] = jnp.where(low, 1.0, v).astype(BF16)


def _proj_kernel(x_ref, m_ref, w_ref, cr_ref, sr_ref, ca_ref, sa_ref,
                 r_ref, aq_ref, ak_ref, av_ref, ag_ref):
    hb = _norm_mod(x_ref[0], m_ref).astype(BF16)
    cr, sr, ca, sa = cr_ref[...], sr_ref[...], ca_ref[...], sa_ref[...]
    low_half = (lax.broadcasted_iota(jnp.int32, ca.shape, 1) % ATT_DH) < (ATT_DH // 2)

    def mm(c0, c1):
        return jnp.dot(hb, w_ref[:, c0:c1], preferred_element_type=F32)

    y = mm(OFF_RQ, OFF_RK)
    for h in range(RET_HEADS):
        sl = slice(h * LANES, (h + 1) * LANES)
        r_ref[0, :, OFF_RQ + h * LANES:OFF_RQ + (h + 1) * LANES] = _rope_ret(y[:, sl], cr, sr).astype(BF16)
    y = mm(OFF_RK, OFF_RV) * (RET_D ** -0.5)
    for h in range(RET_HEADS):
        sl = slice(h * LANES, (h + 1) * LANES)
        r_ref[0, :, OFF_RK + h * LANES:OFF_RK + (h + 1) * LANES] = _rope_ret(y[:, sl], cr, sr).astype(BF16)
    r_ref[0, :, OFF_RV:OFF_RG] = mm(OFF_RV, OFF_RG).astype(BF16)
    r_ref[0, :, OFF_RG:OFF_AQ] = _silu(mm(OFF_RG, OFF_AQ)).astype(BF16)
    y = mm(OFF_AQ, OFF_AK)
    for j in range(ATT_W // LANES):
        sl = slice(j * LANES, (j + 1) * LANES)
        aq_ref[0, :, sl] = (_rope_att(y[:, sl], ca, sa, low_half) * (ATT_DH ** -0.5 * LOG2E)).astype(BF16)
    y = mm(OFF_AK, OFF_AG)
    _store_kv_groups(ak_ref, av_ref, _rope_att(y[:, 0:LANES], ca, sa, low_half), y[:, LANES:2 * LANES])
    ag_ref[0] = _silu(mm(OFF_AG, IN_COLS)).astype(BF16)


def _project(x, mod3, w_in_b, tabs):
    B, L, _ = x.shape
    tm = PROJ_ROWS
    tab_spec = pl.BlockSpec((tm, LANES), lambda b, i: (i, 0))
    row = lambda n: pl.BlockSpec((1, tm, n), lambda b, i: (b, i, 0))
    kv_spec = pl.BlockSpec((1, ATT_KV, tm, LANES), lambda b, i: (b, 0, i, 0))
    kv_shape = jax.ShapeDtypeStruct((B, ATT_KV, L, LANES), BF16)
    return pl.pallas_call(
        _proj_kernel,
        out_shape=(jax.ShapeDtypeStruct((B, L, 4 * RET_W), BF16),
                   jax.ShapeDtypeStruct((B, L, ATT_W), BF16),
                   kv_shape, kv_shape,
                   jax.ShapeDtypeStruct((B, L, ATT_W), BF16)),
        grid=(B, L // tm),
        in_specs=[row(D_MODEL),
                  pl.BlockSpec((1, 1, 3 * D_MODEL), lambda b, i: (b, 0, 0)),
                  pl.BlockSpec((D_MODEL, IN_COLS), lambda b, i: (0, 0)),
                  tab_spec, tab_spec, tab_spec, tab_spec],
        out_specs=(row(4 * RET_W), row(ATT_W), kv_spec, kv_spec, row(ATT_W)),
        compiler_params=pltpu.CompilerParams(
            dimension_semantics=("arbitrary", "arbitrary"), vmem_limit_bytes=VMEM_LIMIT),
        name="proj",
    )(x, mod3, w_in_b, *tabs)


def _ctx_kernel(x_ref, m_ref, w_ref, dl_ref, sf_ref, sb_ref, ck_ref, cv_ref):
    lc = x_ref.shape[1]
    hb = _norm_mod(x_ref[0], m_ref).astype(BF16)
    y = jnp.dot(hb, w_ref[...], preferred_element_type=F32)
    _store_kv_groups(ck_ref, cv_ref, y[:, 2 * RET_W:2 * RET_W + LANES],
                     y[:, 2 * RET_W + LANES:2 * RET_W + 2 * LANES])
    lg = _log_sigmoid(dl_ref[...])
    pos = lax.broadcasted_iota(jnp.int32, (lc, 1), 0).astype(F32)
    for h in range(RET_HEADS):
        k = y[:, h * RET_D:(h + 1) * RET_D] * (RET_D ** -0.5)
        v = y[:, RET_W + h * RET_D:RET_W + (h + 1) * RET_D].astype(BF16)
        wf = jnp.exp(lg[0, h, 0:1, :] * (lc - 1.0 - pos))
        wb = jnp.exp(lg[1, h, 0:1, :] * pos)
        dn = (((0,), (0,)), ((), ()))
        sf_ref[0, h] = lax.dot_general((k * wf).astype(BF16), v, dn, preferred_element_type=F32)
        sb_ref[0, h] = lax.dot_general((k * wb).astype(BF16), v, dn, preferred_element_type=F32)


def _context(ctx, mod3, w_ctx_b, dl):
    B, lc, _ = ctx.shape
    ncol = w_ctx_b.shape[1]
    st = jax.ShapeDtypeStruct((B, RET_HEADS, RET_D, RET_D), F32)
    kv = jax.ShapeDtypeStruct((B, ATT_KV, lc, LANES), BF16)
    st_spec = pl.BlockSpec((1, RET_HEADS, RET_D, RET_D), lambda b: (b, 0, 0, 0))
    kv_spec = pl.BlockSpec((1, ATT_KV, lc, LANES), lambda b: (b, 0, 0, 0))
    ctx_row = mod3.shape[0] // 2
    return pl.pallas_call(
        _ctx_kernel,
        out_shape=(st, st, kv, kv),
        grid=(B,),
        in_specs=[pl.BlockSpec((1, lc, D_MODEL), lambda b: (b, 0, 0)),
                  pl.BlockSpec((1, 1, 3 * D_MODEL), lambda b: (ctx_row, 0, 0)),
                  pl.BlockSpec((D_MODEL, ncol), lambda b: (0, 0)),
                  pl.BlockSpec(dl.shape, lambda b: (0, 0, 0, 0))],
        out_specs=(st_spec, st_spec, kv_spec, kv_spec),
        compiler_params=pltpu.CompilerParams(
            dimension_semantics=("arbitrary",), vmem_limit_bytes=VMEM_LIMIT),
        name="ctx",
    )(ctx, mod3, w_ctx_b, dl)


def _ret_kernel(q_ref, k_ref, v_ref, g_ref, sf_ref, sb_ref, dl_ref, gn_ref, o_ref, u_scr, st_scr):
    L = q_ref.shape[1]
    C = RET_CHUNK
    nch = L // C
    lg = _log_sigmoid(dl_ref[...])
    lgf, lgb = lg[0, 0, 0:1, :], lg[1, 0, 0:1, :]
    lgf1, lgb1 = lgf[:, 0:1], lgb[:, 0:1]
    pos = lax.broadcasted_iota(jnp.int32, (C, 1), 0).astype(F32)
    kf_dec = jnp.exp(lgf * (C - 1.0 - pos)).astype(BF16)
    kb_dec = jnp.exp(lgb * pos).astype(BF16)
    qf_dec = jnp.exp(lgf * (pos + 1.0))
    qb_dec = jnp.exp(lgb * (C - pos))
    cf, cb = jnp.exp(lgf * C), jnp.exp(lgb * C)
    diff = (lax.broadcasted_iota(jnp.int32, (C, C), 0)
            - lax.broadcasted_iota(jnp.int32, (C, C), 1)).astype(F32)
    decay = jnp.where(diff >= 0.0, jnp.exp(lgf1 * jnp.maximum(diff, 0.0)),
                      jnp.exp(lgb1 * jnp.maximum(-diff, 0.0)))
    tn = (((0,), (0,)), ((), ()))
    nt = (((1,), (1,)), ((), ()))

    chunk = lambda n: slice(n * C, (n + 1) * C)

    for n in range(nch):
        kc = k_ref[0, chunk(n), :]
        kd = jnp.concatenate([kc * kf_dec, kc * kb_dec], axis=1)
        u_scr[n] = lax.dot_general(kd, v_ref[0, chunk(n), :], tn, preferred_element_type=F32)

    rf, rb = sf_ref[0, 0], sb_ref[0, 0]
    for n in range(nch):
        st_scr[n, :, 0:RET_D] = rf.astype(BF16)
        rf = rf * cf + u_scr[n, 0:RET_D, :]
        m = nch - 1 - n
        st_scr[m, :, RET_D:2 * RET_D] = rb.astype(BF16)
        rb = rb * cb + u_scr[m, RET_D:2 * RET_D, :]

    gn = gn_ref[0]

    def qk(n):
        return lax.dot_general(q_ref[0, chunk(n), :], k_ref[0, chunk(n), :], nt, preferred_element_type=F32)

    s_next = qk(0)
    for n in range(nch):
        s = s_next
        cross = jnp.dot(q_ref[0, chunk(n), :], st_scr[n], preferred_element_type=F32)
        if n + 1 < nch:
            s_next = qk(n + 1)
        inner = jnp.dot((s * decay).astype(BF16), v_ref[0, chunk(n), :], preferred_element_type=F32)
        y = inner + cross[:, 0:RET_D] * qf_dec + cross[:, RET_D:2 * RET_D] * qb_dec
        mu = jnp.mean(y, axis=-1, keepdims=True)
        yc = y - mu
        var = jnp.mean(yc * yc, axis=-1, keepdims=True)
        yn = yc * lax.rsqrt(var + EPS) * gn
        o_ref[0, chunk(n), :] = (yn * g_ref[0, chunk(n), :].astype(F32)).astype(BF16)


def _retention(r, sf, sb, dl, gn):
    B, L, _ = r.shape
    col = lambda off: pl.BlockSpec((1, L, RET_D), lambda b, h: (b, 0, off + h))
    st_spec = pl.BlockSpec((1, 1, RET_D, RET_D), lambda b, h: (b, h, 0, 0))
    return pl.pallas_call(
        _ret_kernel,
        out_shape=jax.ShapeDtypeStruct((B, L, RET_W), BF16),
        grid=(B, RET_HEADS),
        in_specs=[col(0), col(RET_HEADS), col(2 * RET_HEADS), col(3 * RET_HEADS),
                  st_spec, st_spec,
                  pl.BlockSpec((2, 1, 8, LANES), lambda b, h: (0, h, 0, 0)),
                  pl.BlockSpec((1, 1, RET_D), lambda b, h: (h, 0, 0))],
        out_specs=pl.BlockSpec((1, L, RET_D), lambda b, h: (b, 0, h)),
        scratch_shapes=[pltpu.VMEM((L // RET_CHUNK, 2 * RET_D, RET_D), F32),
                        pltpu.VMEM((L // RET_CHUNK, RET_D, 2 * RET_D), BF16)],
        compiler_params=pltpu.CompilerParams(
            dimension_semantics=("arbitrary", "arbitrary"), vmem_limit_bytes=VMEM_LIMIT),
        name="ret",
    )(r, r, r, r, sf, sb, dl, gn)


def _att_kernel(q_ref, k_ref, v_ref, ck_ref, cv_ref, g_ref, sink_ref, o_ref):
    L = k_ref.shape[2]
    nb_total = L // ATT_BLOCK
    nb_step = q_ref.shape[1] // ATT_BLOCK
    nhb = ATT_W // LANES
    i = pl.program_id(1)
    r_i = lax.broadcasted_iota(jnp.int32, (ATT_BLOCK, ATT_BLOCK), 0)
    q_i = lax.broadcasted_iota(jnp.int32, (ATT_BLOCK, ATT_BLOCK), 1)
    band_prev = jnp.where(q_i <= r_i, 0.0, NEG).astype(F32)
    band_next = jnp.where(r_i <= q_i, 0.0, NEG).astype(F32)
    nt = (((1,), (1,)), ((), ()))
    tn = (((0,), (0,)), ((), ()))

    def key_rows(jb):
        n = i * nb_step + jb
        blk = lambda t: pl.ds(pl.multiple_of(t * ATT_BLOCK, ATT_BLOCK), ATT_BLOCK)
        return n, (blk(jnp.maximum(n - 1, 0)), blk(n), blk(jnp.minimum(n + 1, nb_total - 1)))

    def scores(jb, kv):
        n, (p_rows, o_rows, n_rows) = key_rows(jb)
        q = q_ref[0, jb * ATT_BLOCK:(jb + 1) * ATT_BLOCK, :]
        qall = jnp.concatenate([q[:, j * LANES:(j + 1) * LANES] for j in range(nhb)], axis=0)
        kall = jnp.concatenate([ck_ref[0, kv], k_ref[0, kv, p_rows, :], k_ref[0, kv, o_rows, :],
                                k_ref[0, kv, n_rows, :]], axis=0)
        return lax.dot_general(kall, qall, nt, preferred_element_type=F32)

    def attend(jb, kv, s):
        n, (p_rows, o_rows, n_rows) = key_rows(jb)
        lc = ck_ref.shape[2]
        bias_prev = jnp.concatenate([band_prev + jnp.where(n == 0, NEG, 0.0)] * nhb, axis=1)
        bias_next = jnp.concatenate([band_next + jnp.where(n == nb_total - 1, NEG, 0.0)] * nhb, axis=1)
        parts = [s[0:lc], s[lc:lc + ATT_BLOCK] + bias_prev, s[lc + ATT_BLOCK:lc + 2 * ATT_BLOCK],
                 s[lc + 2 * ATT_BLOCK:] + bias_next]
        sink = sink_ref[kv:kv + 1, :] * LOG2E
        m = sink
        for t in parts:
            m = jnp.maximum(m, jnp.max(t, axis=0, keepdims=True))
        p = jnp.concatenate([jnp.exp2(t - m).astype(BF16) for t in parts], axis=0)
        vaug = jnp.concatenate([cv_ref[0, kv], v_ref[0, kv, p_rows, :], v_ref[0, kv, o_rows, :],
                                v_ref[0, kv, n_rows, :]], axis=0)
        o = lax.dot_general(vaug, p, tn, preferred_element_type=F32)
        if kv == 0:
            val, ones = slice(0, ATT_DH), slice(ATT_DH, ATT_DH + 1)
        else:
            val, ones = slice(ATT_DH, 2 * ATT_DH), slice(0, 1)
        den = o[ones] + jnp.exp2(sink - m)
        return o[val] * (1.0 / den)

    chains = [(jb, kv) for jb in range(nb_step) for kv in range(ATT_KV)]
    s_next = scores(*chains[0])
    halves = []
    for c, (jb, kv) in enumerate(chains):
        s = s_next
        if c + 1 < len(chains):
            s_next = scores(*chains[c + 1])
        halves.append(attend(jb, kv, s))
        if kv == ATT_KV - 1:
            rows = slice(jb * ATT_BLOCK, (jb + 1) * ATT_BLOCK)
            comb = jnp.concatenate(halves, axis=0)
            halves = []
            for j in range(nhb):
                sl = slice(j * LANES, (j + 1) * LANES)
                oj = comb[:, sl].T
                o_ref[0, rows, sl] = (oj * g_ref[0, rows, sl].astype(F32)).astype(BF16)


def _attention(aq, ak, av, ck, cv, ag, sink2):
    B, L, _ = aq.shape
    lc = ck.shape[2]
    tq = ATT_ROWS
    qrow = pl.BlockSpec((1, tq, ATT_W), lambda b, i: (b, i, 0))
    full = lambda n: pl.BlockSpec((1, ATT_KV, n, LANES), lambda b, i: (b, 0, 0, 0))
    return pl.pallas_call(
        _att_kernel,
        out_shape=jax.ShapeDtypeStruct((B, L, ATT_W), BF16),
        grid=(B, L // tq),
        in_specs=[qrow, full(L), full(L), full(lc), full(lc), qrow,
                  pl.BlockSpec(sink2.shape, lambda b, i: (0, 0))],
        out_specs=qrow,
        compiler_params=pltpu.CompilerParams(
            dimension_semantics=("arbitrary", "arbitrary"), vmem_limit_bytes=VMEM_LIMIT),
        name="att",
    )(aq, ak, av, ck, cv, ag, sink2)


def _out_kernel(r_ref, a_ref, x_ref, m_ref, wr_ref, wa_ref, fn_ref, o_ref):
    mixed = (jnp.dot(r_ref[0], wr_ref[...], preferred_element_type=F32)
             + jnp.dot(a_ref[0], wa_ref[...], preferred_element_type=F32))
    gate = m_ref[0, :, 2 * D_MODEL:3 * D_MODEL]
    xn = x_ref[0] + gate * mixed
    o_ref[0] = xn * lax.rsqrt(jnp.mean(xn * xn, axis=-1, keepdims=True) + EPS) * fn_ref[...]


def _output(ret, att, x, mod3, w_r, w_a, fnw):
    B, L, _ = x.shape
    tm = PROJ_ROWS
    row = lambda n: pl.BlockSpec((1, tm, n), lambda b, i: (b, i, 0))
    wspec = pl.BlockSpec((RET_W, D_MODEL), lambda b, i: (0, 0))
    return pl.pallas_call(
        _out_kernel,
        out_shape=jax.ShapeDtypeStruct((B, L, D_MODEL), F32),
        grid=(B, L // tm),
        in_specs=[row(RET_W), row(ATT_W), row(D_MODEL),
                  pl.BlockSpec((1, 1, 3 * D_MODEL), lambda b, i: (b, 0, 0)),
                  wspec, wspec, pl.BlockSpec((1, D_MODEL), lambda b, i: (0, 0))],
        out_specs=row(D_MODEL),
        compiler_params=pltpu.CompilerParams(
            dimension_semantics=("arbitrary", "arbitrary"), vmem_limit_bytes=VMEM_LIMIT),
        name="out",
    )(ret, att, x, mod3, w_r, w_a, fnw)


def _rope_tables(L):
    pos = np.arange(L)
    rows, cols = (pos // GRID_W).astype(np.float32), (pos % GRID_W).astype(np.float32)

    def tables(dh):
        nf = dh // 4
        inv = jnp.asarray(ROPE_BASE, F32) ** (-jnp.arange(nf, dtype=F32) / nf)
        ang = jnp.concatenate([jnp.asarray(rows)[:, None] * inv, jnp.asarray(cols)[:, None] * inv], axis=-1)
        cos, sin = jnp.cos(ang), jnp.sin(ang)
        reps = LANES // dh
        return (jnp.tile(jnp.concatenate([cos, cos], axis=-1), (1, reps)),
                jnp.tile(jnp.concatenate([-sin, sin], axis=-1), (1, reps)))

    cr, sr = tables(RET_D)
    ca, sa = tables(ATT_DH)
    return cr, sr, ca, sa


def kernel(x, c, ctx, c_ctx, w_ada, b_ada, w_in, ret_decay_logit, ret_gn_w, att_sink, w_out, final_norm_w):
    B, L, _ = x.shape
    assert w_ada.shape[0] == 1, "single-layer trunk"
    perm = _att_pair_perm()

    cc = jnp.zeros((2 * B, D_MODEL), F32).at[:B].set(c).at[B].set(c_ctx)
    mod3 = _modulation(cc, w_ada[0], b_ada[0][None, :]).reshape(2 * B, 1, 3 * D_MODEL)

    w = w_in[0]
    w_perm = jnp.concatenate([w[:, :OFF_AQ], w[:, OFF_AQ + perm], w[:, OFF_AK:OFF_AG],
                              w[:, OFF_AG + perm]], axis=1).astype(BF16)
    w_ctx = jnp.concatenate([w[:, OFF_RK:OFF_RG], w[:, OFF_AK:OFF_AG]], axis=1).astype(BF16)
    dl = jnp.broadcast_to(ret_decay_logit[0].astype(F32)[:, :, None, None], (2, RET_HEADS, 8, LANES))

    r, aq, ak, av, ag = _project(x, mod3, w_perm, _rope_tables(L))
    sf, sb, ck, cv = _context(ctx, mod3, w_ctx, dl)
    ret = _retention(r, sf, sb, dl, ret_gn_w[0].reshape(RET_HEADS, 1, RET_D))

    sink = att_sink[0].astype(F32)
    sink2 = jnp.zeros((8, ATT_W), F32).at[0].set(jnp.repeat(sink[:4], LANES)).at[1].set(jnp.repeat(sink[4:], LANES))
    att = _attention(aq, ak, av, ck, cv, ag, sink2)

    wo = w_out[0]
    return _output(ret, att, x, mod3, wo[:RET_W].astype(BF16), wo[RET_W + perm].astype(BF16),
                   final_norm_w[None, :])
```

```python
import numpy as np
import jax
import jax.numpy as jnp
from jax import lax
from jax.experimental import pallas as pl
from jax.experimental.pallas import tpu as pltpu

F32 = jnp.float32
BF16 = jnp.bfloat16

D_MODEL = 1024
GRID_W = 64
RET_HEADS = 4
RET_D = 128
RET_W = RET_HEADS * RET_D
ATT_HEADS = 8
ATT_KV = 2
ATT_DH = 64
ATT_W = ATT_HEADS * ATT_DH
ATT_BLOCK = 128
WINDOW = 128
ROPE_BASE = 10000.0
EPS = 1e-6
NEG = -1e30
LOG2E = 1.4426950408889634
IN_COLS = 4 * RET_W + 2 * ATT_W + 2 * ATT_KV * ATT_DH
OFF_RQ, OFF_RK, OFF_RV, OFF_RG = 0, RET_W, 2 * RET_W, 3 * RET_W
OFF_AQ = 4 * RET_W
OFF_AK = OFF_AQ + ATT_W
OFF_AV = OFF_AK + ATT_KV * ATT_DH
OFF_AG = OFF_AV + ATT_KV * ATT_DH

LANES = 128
VMEM_LIMIT = 48 * 1024 * 1024

PROJ_ROWS = 512
RET_CHUNK = 256
RET_UNROLL = 16
ATT_ROWS = 512


def _att_pair_perm():
    p = np.arange(ATT_W)
    j, half, d = p // LANES, (p % LANES) // ATT_DH, p % ATT_DH
    return (j + 4 * half) * ATT_DH + d


def _silu(t):
    return t / (1.0 + jnp.exp(-t))


def _log_sigmoid(t):
    return -(jnp.maximum(-t, 0.0) + jnp.log1p(jnp.exp(-jnp.abs(t))))


def _mod_kernel(c_ref, w_ref, b_ref, o_ref):
    s = _silu(c_ref[...])
    o_ref[...] = jnp.dot(s, w_ref[...], preferred_element_type=F32,
                         precision=lax.Precision.HIGHEST) + b_ref[...]


def _modulation(cc, w_ada, b_ada):
    n = w_ada.shape[1]
    bn = 512
    return pl.pallas_call(
        _mod_kernel,
        out_shape=jax.ShapeDtypeStruct((cc.shape[0], n), F32),
        grid=(n // bn,),
        in_specs=[pl.BlockSpec(cc.shape, lambda i: (0, 0)),
                  pl.BlockSpec((D_MODEL, bn), lambda i: (0, i)),
                  pl.BlockSpec((1, bn), lambda i: (0, i))],
        out_specs=pl.BlockSpec((cc.shape[0], bn), lambda i: (0, i)),
        compiler_params=pltpu.CompilerParams(dimension_semantics=("arbitrary",)),
        name="mod",
    )(cc, w_ada, b_ada)


def _norm_mod(x, m_ref):
    shift = m_ref[0, :, 0:D_MODEL]
    scale = m_ref[0, :, D_MODEL:2 * D_MODEL]
    h = x * lax.rsqrt(jnp.mean(x * x, axis=-1, keepdims=True) + EPS)
    return h * (1.0 + scale) + shift


def _rope_ret(t, cos, sin):
    return t * cos + pltpu.roll(t, 64, 1) * sin


def _rope_att(t, cos, sin, low_half):
    partner = jnp.where(low_half, pltpu.roll(t, 96, 1), pltpu.roll(t, 32, 1))
    return t * cos + partner * sin


def _store_kv_groups(k_ref, v_ref, k, v):
    low = lax.broadcasted_iota(jnp.int32, k.shape, 1) < ATT_DH
    k_ref[0, 0] = jnp.where(low, k, 0.0).astype(BF16)
    k_ref[0, 1] = jnp.where(low, 0.0, k).astype(BF16)
    v_ref[0, 0] = jnp.where(low, v, 1.0).astype(BF16)
    v_ref[0, 1] = jnp.where(low, 1.0, v).astype(BF16)


def _proj_kernel(x_ref, m_ref, w_ref, cr_ref, sr_ref, ca_ref, sa_ref,
                 r_ref, aq_ref, ak_ref, av_ref, ag_ref):
    hb = _norm_mod(x_ref[0], m_ref).astype(BF16)
    cr, sr, ca, sa = cr_ref[...], sr_ref[...], ca_ref[...], sa_ref[...]
    low_half = (lax.broadcasted_iota(jnp.int32, ca.shape, 1) % ATT_DH) < (ATT_DH // 2)

    def mm(c0, c1):
        return jnp.dot(hb, w_ref[:, c0:c1], preferred_element_type=F32)

    y = mm(OFF_RQ, OFF_RK)
    for h in range(RET_HEADS):
        sl = slice(h * LANES, (h + 1) * LANES)
        r_ref[0, :, OFF_RQ + h * LANES:OFF_RQ + (h + 1) * LANES] = _rope_ret(y[:, sl], cr, sr).astype(BF16)
    y = mm(OFF_RK, OFF_RV) * (RET_D ** -0.5)
    for h in range(RET_HEADS):
        sl = slice(h * LANES, (h + 1) * LANES)
        r_ref[0, :, OFF_RK + h * LANES:OFF_RK + (h + 1) * LANES] = _rope_ret(y[:, sl], cr, sr).astype(BF16)
    r_ref[0, :, OFF_RV:OFF_RG] = mm(OFF_RV, OFF_RG).astype(BF16)
    r_ref[0, :, OFF_RG:OFF_AQ] = _silu(mm(OFF_RG, OFF_AQ)).astype(BF16)
    y = mm(OFF_AQ, OFF_AK)
    for j in range(ATT_W // LANES):
        sl = slice(j * LANES, (j + 1) * LANES)
        aq_ref[0, :, sl] = (_rope_att(y[:, sl], ca, sa, low_half) * (ATT_DH ** -0.5 * LOG2E)).astype(BF16)
    y = mm(OFF_AK, OFF_AG)
    _store_kv_groups(ak_ref, av_ref, _rope_att(y[:, 0:LANES], ca, sa, low_half), y[:, LANES:2 * LANES])
    ag_ref[0] = _silu(mm(OFF_AG, IN_COLS)).astype(BF16)


def _project(x, mod3, w_in_b, tabs):
    B, L, _ = x.shape
    tm = PROJ_ROWS
    tab_spec = pl.BlockSpec((tm, LANES), lambda b, i: (i, 0))
    row = lambda n: pl.BlockSpec((1, tm, n), lambda b, i: (b, i, 0))
    kv_spec = pl.BlockSpec((1, ATT_KV, tm, LANES), lambda b, i: (b, 0, i, 0))
    kv_shape = jax.ShapeDtypeStruct((B, ATT_KV, L, LANES), BF16)
    return pl.pallas_call(
        _proj_kernel,
        out_shape=(jax.ShapeDtypeStruct((B, L, 4 * RET_W), BF16),
                   jax.ShapeDtypeStruct((B, L, ATT_W), BF16),
                   kv_shape, kv_shape,
                   jax.ShapeDtypeStruct((B, L, ATT_W), BF16)),
        grid=(B, L // tm),
        in_specs=[row(D_MODEL),
                  pl.BlockSpec((1, 1, 3 * D_MODEL), lambda b, i: (b, 0, 0)),
                  pl.BlockSpec((D_MODEL, IN_COLS), lambda b, i: (0, 0)),
                  tab_spec, tab_spec, tab_spec, tab_spec],
        out_specs=(row(4 * RET_W), row(ATT_W), kv_spec, kv_spec, row(ATT_W)),
        compiler_params=pltpu.CompilerParams(
            dimension_semantics=("arbitrary", "arbitrary"), vmem_limit_bytes=VMEM_LIMIT),
        name="proj",
    )(x, mod3, w_in_b, *tabs)


def _ctx_kernel(x_ref, m_ref, wk_ref, wv_ref, wa_ref, dl_ref, sf_ref, sb_ref, ck_ref, cv_ref):
    lc = x_ref.shape[1]
    hb = _norm_mod(x_ref[0], m_ref).astype(BF16)
    yk = jnp.dot(hb, wk_ref[...], preferred_element_type=F32)
    yv = jnp.dot(hb, wv_ref[...], preferred_element_type=F32)
    ya = jnp.dot(hb, wa_ref[...], preferred_element_type=F32)
    _store_kv_groups(ck_ref, cv_ref, ya[:, 0:LANES], ya[:, LANES:2 * LANES])
    lg = _log_sigmoid(dl_ref[...])
    pos = lax.broadcasted_iota(jnp.int32, (lc, 1), 0).astype(F32)
    for h in range(RET_HEADS):
        k = yk[:, h * RET_D:(h + 1) * RET_D] * (RET_D ** -0.5)
        v = yv[:, h * RET_D:(h + 1) * RET_D].astype(BF16)
        wf = jnp.exp(lg[0, h, 0:1, :] * (lc - 1.0 - pos))
        wb = jnp.exp(lg[1, h, 0:1, :] * pos)
        dn = (((0,), (0,)), ((), ()))
        sf_ref[0, h] = lax.dot_general((k * wf).astype(BF16), v, dn, preferred_element_type=F32)
        sb_ref[0, h] = lax.dot_general((k * wb).astype(BF16), v, dn, preferred_element_type=F32)


def _context(ctx, mod3, w_in_b, dl):
    B, lc, _ = ctx.shape
    wcol = lambda off, n: pl.BlockSpec((D_MODEL, n), lambda b: (0, off // n))
    st =jax.ShapeDtypeStruct((B, RET_HEADS, RET_D, RET_D), F32)
    kv = jax.ShapeDtypeStruct((B, ATT_KV, lc, LANES), BF16)
    st_spec = pl.BlockSpec((1, RET_HEADS, RET_D, RET_D), lambda b: (b, 0, 0, 0))
    kv_spec = pl.BlockSpec((1, ATT_KV, lc, LANES), lambda b: (b, 0, 0, 0))
    ctx_row = mod3.shape[0] // 2
    return pl.pallas_call(
        _ctx_kernel,
        out_shape=(st, st, kv, kv),
        grid=(B,),
        in_specs=[pl.BlockSpec((1, lc, D_MODEL), lambda b: (b, 0, 0)),
                  pl.BlockSpec((1, 1, 3 * D_MODEL), lambda b: (ctx_row, 0, 0)),
                  wcol(OFF_RK, RET_W), wcol(OFF_RV, RET_W), wcol(OFF_AK, 2 * LANES),
                  pl.BlockSpec(dl.shape, lambda b: (0, 0, 0, 0))],
        out_specs=(st_spec, st_spec, kv_spec, kv_spec),
        compiler_params=pltpu.CompilerParams(
            dimension_semantics=("arbitrary",), vmem_limit_bytes=VMEM_LIMIT),
        name="ctx",
    )(ctx, mod3, w_in_b, w_in_b, w_in_b, dl)


def _ret_kernel(q_ref, k_ref, v_ref, g_ref, sf_ref, sb_ref, dl_ref, gn_ref, o_ref, u_scr, st_scr):
    L = q_ref.shape[1]
    C = RET_CHUNK
    nch = L // C
    lg = _log_sigmoid(dl_ref[...])
    lgf, lgb = lg[0, 0, 0:1, :], lg[1, 0, 0:1, :]
    lgf1, lgb1 = lgf[:, 0:1], lgb[:, 0:1]
    pos = lax.broadcasted_iota(jnp.int32, (C, 1), 0).astype(F32)
    kf_dec = jnp.exp(lgf * (C - 1.0 - pos)).astype(BF16)
    kb_dec = jnp.exp(lgb * pos).astype(BF16)
    qf_dec = jnp.exp(lgf * (pos + 1.0))
    qb_dec = jnp.exp(lgb * (C - pos))
    cf, cb = jnp.exp(lgf * C), jnp.exp(lgb * C)
    diff = (lax.broadcasted_iota(jnp.int32, (C, C), 0)
            - lax.broadcasted_iota(jnp.int32, (C, C), 1)).astype(F32)
    decay = jnp.where(diff >= 0.0, jnp.exp(lgf1 * jnp.maximum(diff, 0.0)),
                      jnp.exp(lgb1 * jnp.maximum(-diff, 0.0)))
    tn = (((0,), (0,)), ((), ()))
    nt = (((1,), (1,)), ((), ()))

    chunk = lambda n: slice(n * C, (n + 1) * C)

    for n in range(nch):
        kc = k_ref[0, chunk(n), :]
        kd = jnp.concatenate([kc * kf_dec, kc * kb_dec], axis=1)
        u_scr[n] = lax.dot_general(kd, v_ref[0, chunk(n), :], tn, preferred_element_type=F32)

    rf, rb = sf_ref[0, 0], sb_ref[0, 0]
    for n in range(nch):
        st_scr[n, :, 0:RET_D] = rf.astype(BF16)
        rf = rf * cf + u_scr[n, 0:RET_D, :]
        m = nch - 1 - n
        st_scr[m, :, RET_D:2 * RET_D] = rb.astype(BF16)
        rb = rb * cb + u_scr[m, RET_D:2 * RET_D, :]

    gn = gn_ref[0]

    def qk(n):
        return lax.dot_general(q_ref[0, chunk(n), :], k_ref[0, chunk(n), :], nt, preferred_element_type=F32)

    s_next = qk(0)
    for n in range(nch):
        s = s_next
        cross = jnp.dot(q_ref[0, chunk(n), :], st_scr[n], preferred_element_type=F32)
        if n + 1 < nch:
            s_next = qk(n + 1)
        inner = jnp.dot((s * decay).astype(BF16), v_ref[0, chunk(n), :], preferred_element_type=F32)
        y = inner + cross[:, 0:RET_D] * qf_dec + cross[:, RET_D:2 * RET_D] * qb_dec
        mu = jnp.mean(y, axis=-1, keepdims=True)
        yc = y - mu
        var = jnp.mean(yc * yc, axis=-1, keepdims=True)
        yn = yc * lax.rsqrt(var + EPS) * gn
        o_ref[0, chunk(n), :] = (yn * g_ref[0, chunk(n), :].astype(F32)).astype(BF16)


def _retention(r, sf, sb, dl, gn):
    B, L, _ = r.shape
    col = lambda off: pl.BlockSpec((1, L, RET_D), lambda b, h: (b, 0, off + h))
    st_spec = pl.BlockSpec((1, 1, RET_D, RET_D), lambda b, h: (b, h, 0, 0))
    return pl.pallas_call(
        _ret_kernel,
        out_shape=jax.ShapeDtypeStruct((B, L, RET_W), BF16),
        grid=(B, RET_HEADS),
        in_specs=[col(0), col(RET_HEADS), col(2 * RET_HEADS), col(3 * RET_HEADS),
                  st_spec, st_spec,
                  pl.BlockSpec((2, 1, 8, LANES), lambda b, h: (0, h, 0, 0)),
                  pl.BlockSpec((1, 1, RET_D), lambda b, h: (h, 0, 0))],
        out_specs=pl.BlockSpec((1, L, RET_D), lambda b, h: (b, 0, h)),
        scratch_shapes=[pltpu.VMEM((L // RET_CHUNK, 2 * RET_D, RET_D), F32),
                        pltpu.VMEM((L // RET_CHUNK, RET_D, 2 * RET_D), BF16)],
        compiler_params=pltpu.CompilerParams(
            dimension_semantics=("arbitrary", "arbitrary"), vmem_limit_bytes=VMEM_LIMIT),
        name="ret",
    )(r, r, r, r, sf, sb, dl, gn)


def _att_kernel(q_ref, k_ref, v_ref, ck_ref, cv_ref, g_ref, sink_ref, r_ref, x_ref, m_ref, w_ref, fn_ref,
                o_ref):
    L = k_ref.shape[2]
    nb_total = L // ATT_BLOCK
    nb_step = q_ref.shape[1] // ATT_BLOCK
    nhb = ATT_W // LANES
    i = pl.program_id(1)
    r_i = lax.broadcasted_iota(jnp.int32, (ATT_BLOCK, ATT_BLOCK), 0)
    q_i = lax.broadcasted_iota(jnp.int32, (ATT_BLOCK, ATT_BLOCK), 1)
    band_prev = jnp.where(q_i <= r_i, 0.0, NEG).astype(F32)
    band_next = jnp.where(r_i <= q_i, 0.0, NEG).astype(F32)
    nt = (((1,), (1,)), ((), ()))
    tn = (((0,), (0,)), ((), ()))

    def key_rows(jb):
        n = i * nb_step + jb
        blk = lambda t: pl.ds(pl.multiple_of(t * ATT_BLOCK, ATT_BLOCK), ATT_BLOCK)
        return n, (blk(jnp.maximum(n - 1, 0)), blk(n), blk(jnp.minimum(n + 1, nb_total - 1)))

    def scores(jb, kv):
        n, (p_rows, o_rows, n_rows) = key_rows(jb)
        q = q_ref[0, jb * ATT_BLOCK:(jb + 1) * ATT_BLOCK, :]
        qall = jnp.concatenate([q[:, j * LANES:(j + 1) * LANES] for j in range(nhb)], axis=0)
        kall = jnp.concatenate([ck_ref[0, kv], k_ref[0, kv, p_rows, :], k_ref[0, kv, o_rows, :],
                                k_ref[0, kv, n_rows, :]], axis=0)
        return lax.dot_general(kall, qall, nt, preferred_element_type=F32)

    def attend(jb, kv, s):
        n, (p_rows, o_rows, n_rows) = key_rows(jb)
        lc = ck_ref.shape[2]
        bias_prev = jnp.concatenate([band_prev + jnp.where(n == 0, NEG, 0.0)] * nhb, axis=1)
        bias_next = jnp.concatenate([band_next + jnp.where(n == nb_total - 1, NEG, 0.0)] * nhb, axis=1)
        parts = [s[0:lc], s[lc:lc + ATT_BLOCK] + bias_prev, s[lc + ATT_BLOCK:lc + 2 * ATT_BLOCK],
                 s[lc + 2 * ATT_BLOCK:] + bias_next]
        sink = sink_ref[kv:kv + 1, :] * LOG2E
        m = sink
        for t in parts:
            m = jnp.maximum(m, jnp.max(t, axis=0, keepdims=True))
        p = jnp.concatenate([jnp.exp2(t - m).astype(BF16) for t in parts], axis=0)
        vaug = jnp.concatenate([cv_ref[0, kv], v_ref[0, kv, p_rows, :], v_ref[0, kv, o_rows, :],
                                v_ref[0, kv, n_rows, :]], axis=0)
        o = lax.dot_general(vaug, p, tn, preferred_element_type=F32)
        if kv == 0:
            val, ones = slice(0, ATT_DH), slice(ATT_DH, ATT_DH + 1)
        else:
            val, ones = slice(ATT_DH, 2 * ATT_DH), slice(0, 1)
        den = o[ones] + jnp.exp2(sink - m)
        return o[val] * (1.0 / den)

    def gated_block(jb, halves):
        rows = slice(jb * ATT_BLOCK, (jb + 1) * ATT_BLOCK)
        comb = jnp.concatenate(halves, axis=0)
        return jnp.concatenate(
            [(comb[:, j * LANES:(j + 1) * LANES].T
              * g_ref[0, rows, j * LANES:(j + 1) * LANES].astype(F32)).astype(BF16) for j in range(nhb)], axis=1)

    def finish(jb, att_blk):
        rows = slice(jb * ATT_BLOCK, (jb + 1) * ATT_BLOCK)
        mix_in = jnp.concatenate([r_ref[0, rows, :], att_blk], axis=1)
        mixed = jnp.dot(mix_in, w_ref[...], preferred_element_type=F32)
        xn = x_ref[0, rows, :] + m_ref[0, :, 2 * D_MODEL:3 * D_MODEL] * mixed
        o_ref[0, rows, :] = xn * lax.rsqrt(jnp.mean(xn * xn, axis=-1, keepdims=True) + EPS) * fn_ref[...]

    chains = [(jb, kv) for jb in range(nb_step) for kv in range(ATT_KV)]
    s_next = scores(*chains[0])
    halves, pending = [], []
    for c, (jb, kv) in enumerate(chains):
        s = s_next
        if c + 1 < len(chains):
            s_next = scores(*chains[c + 1])
        halves.append(attend(jb, kv, s))
        if pending:
            finish(*pending.pop())
        if kv == ATT_KV - 1:
            pending.append((jb, gated_block(jb, halves)))
            halves = []
    finish(*pending.pop())


def _attention_output(aq, ak, av, ck, cv, ag, sink2, ret, x, mod3, w_out_b, fnw):
    B, L, _ = aq.shape
    lc = ck.shape[2]
    tq = ATT_ROWS
    row = lambda n: pl.BlockSpec((1, tq, n), lambda b, i: (b, i, 0))
    full = lambda n: pl.BlockSpec((1, ATT_KV, n, LANES), lambda b, i: (b, 0, 0, 0))
    return pl.pallas_call(
        _att_kernel,
        out_shape=jax.ShapeDtypeStruct((B, L, D_MODEL), F32),
        grid=(B, L // tq),
        in_specs=[row(ATT_W), full(L), full(L), full(lc), full(lc), row(ATT_W),
                  pl.BlockSpec(sink2.shape, lambda b, i: (0, 0)),
                  row(RET_W), row(D_MODEL),
                  pl.BlockSpec((1, 1, 3 * D_MODEL), lambda b, i: (b, 0, 0)),
                  pl.BlockSpec((RET_W + ATT_W, D_MODEL), lambda b, i: (0, 0)),
                  pl.BlockSpec((1, D_MODEL), lambda b, i: (0, 0))],
        out_specs=row(D_MODEL),
        compiler_params=pltpu.CompilerParams(
            dimension_semantics=("arbitrary", "arbitrary"), vmem_limit_bytes=VMEM_LIMIT),
        name="att",
    )(aq, ak, av, ck, cv, ag, sink2, ret, x, mod3, w_out_b, fnw)


def _rope_tables(L):
    pos = np.arange(L)
    rows, cols = (pos // GRID_W).astype(np.float64), (pos % GRID_W).astype(np.float64)

    def tables(dh):
        nf = dh // 4
        inv = ROPE_BASE ** (-np.arange(nf, dtype=np.float64) / nf)
        ang = np.concatenate([rows[:, None] * inv, cols[:, None] * inv], axis=-1)
        cos, sin = np.cos(ang), np.sin(ang)
        reps = LANES // dh
        return (jnp.asarray(np.tile(np.concatenate([cos, cos], axis=-1), (1, reps)), F32),
                jnp.asarray(np.tile(np.concatenate([-sin, sin], axis=-1), (1, reps)), F32))

    cr, sr = tables(RET_D)
    ca, sa = tables(ATT_DH)
    return cr, sr, ca, sa


def kernel(x, c, ctx, c_ctx, w_ada, b_ada, w_in, ret_decay_logit, ret_gn_w, att_sink, w_out, final_norm_w):
    B, L, _ = x.shape
    assert w_ada.shape[0] == 1, "single-layer trunk"
    perm = _att_pair_perm()

    cc = jnp.concatenate([c, c_ctx[None, :], jnp.zeros((B - 1, D_MODEL), F32)], axis=0)
    mod3 = _modulation(cc, w_ada[0], b_ada[0][None, :]).reshape(2 * B, 1, 3 * D_MODEL)

    in_cols = np.arange(IN_COLS)
    in_cols[OFF_AQ:OFF_AK] = OFF_AQ + perm
    in_cols[OFF_AG:] = OFF_AG + perm
    w_in_b = w_in[0][:, in_cols].astype(BF16)
    out_rows = np.concatenate([np.arange(RET_W), RET_W + perm])
    w_out_b = w_out[0][out_rows].astype(BF16)
    dl = jnp.broadcast_to(ret_decay_logit[0].astype(F32)[:, :, None, None], (2, RET_HEADS, 8, LANES))

    r, aq, ak, av, ag = _project(x, mod3, w_in_b, _rope_tables(L))
    sf, sb, ck, cv = _context(ctx, mod3, w_in_b, dl)
    ret = _retention(r, sf, sb, dl, ret_gn_w[0].reshape(RET_HEADS, 1, RET_D))

    sink2 = jnp.repeat(att_sink[0].astype(F32).reshape(ATT_KV, ATT_HEADS // ATT_KV), LANES, axis=1)
    return _attention_output(aq, ak, av, ck, cv, ag, sink2, ret, x, mod3, w_out_b, final_norm_w[None, :])
```

```python
import numpy as np
import jax
import jax.numpy as jnp
from jax import lax
from jax.experimental import pallas as pl
from jax.experimental.pallas import tpu as pltpu

F32 = jnp.float32
BF16 = jnp.bfloat16

D_MODEL = 1024
GRID_W = 64
RET_HEADS = 4
RET_D = 128
RET_W = RET_HEADS * RET_D
ATT_HEADS = 8
ATT_KV = 2
ATT_DH = 64
ATT_W = ATT_HEADS * ATT_DH
ATT_BLOCK = 128
WINDOW = 128
ROPE_BASE = 10000.0
EPS = 1e-6
NEG = -1e30
LOG2E = 1.4426950408889634
IN_COLS = 4 * RET_W + 2 * ATT_W + 2 * ATT_KV * ATT_DH
OFF_RQ, OFF_RK, OFF_RV, OFF_RG = 0, RET_W, 2 * RET_W, 3 * RET_W
OFF_AQ = 4 * RET_W
OFF_AK = OFF_AQ + ATT_W
OFF_AV = OFF_AK + ATT_KV * ATT_DH
OFF_AG = OFF_AV + ATT_KV * ATT_DH

LANES = 128
VMEM_LIMIT = 48 * 1024 * 1024

PROJ_ROWS = 1024
PROJ_SUB = 256
RET_CHUNK = 256
RET_UNROLL = 16
ATT_ROWS = 1024


def _att_pair_perm():
    p = np.arange(ATT_W)
    j, half, d = p // LANES, (p % LANES) // ATT_DH, p % ATT_DH
    return (j + 4 * half) * ATT_DH + d


def _silu(t):
    return t / (1.0 + jnp.exp(-t))


def _log_sigmoid(t):
    return -(jnp.maximum(-t, 0.0) + jnp.log1p(jnp.exp(-jnp.abs(t))))


def _mod_kernel(c_ref, w_ref, b_ref, o_ref):
    s = _silu(c_ref[...])
    o_ref[...] = jnp.dot(s, w_ref[...], preferred_element_type=F32,
                         precision=lax.Precision.HIGHEST) + b_ref[...]


def _modulation(cc, w_ada, b_ada):
    n = w_ada.shape[1]
    bn = 512
    return pl.pallas_call(
        _mod_kernel,
        out_shape=jax.ShapeDtypeStruct((cc.shape[0], n), F32),
        grid=(n // bn,),
        in_specs=[pl.BlockSpec(cc.shape, lambda i: (0, 0)),
                  pl.BlockSpec((D_MODEL, bn), lambda i: (0, i)),
                  pl.BlockSpec((1, bn), lambda i: (0, i))],
        out_specs=pl.BlockSpec((cc.shape[0], bn), lambda i: (0, i)),
        compiler_params=pltpu.CompilerParams(dimension_semantics=("arbitrary",)),
        name="mod",
    )(cc, w_ada, b_ada)


def _norm_mod(x, m_ref):
    shift = m_ref[0, :, 0:D_MODEL]
    scale = m_ref[0, :, D_MODEL:2 * D_MODEL]
    h = x * lax.rsqrt(jnp.mean(x * x, axis=-1, keepdims=True) + EPS)
    return h * (1.0 + scale) + shift


def _rope_ret(t, cos, sin):
    return t * cos + pltpu.roll(t, 64, 1) * sin


def _rope_att(t, cos, sin, low_half):
    partner = jnp.where(low_half, pltpu.roll(t, 96, 1), pltpu.roll(t, 32, 1))
    return t * cos + partner * sin


def _store_kv_groups(k_ref, v_ref, rows, k, v):
    low = lax.broadcasted_iota(jnp.int32, k.shape, 1) < ATT_DH
    k_ref[0, 0, rows, :] = jnp.where(low, k, 0.0).astype(BF16)
    k_ref[0, 1, rows, :] = jnp.where(low, 0.0, k).astype(BF16)
    v_ref[0, 0, rows, :] = jnp.where(low, v, 1.0).astype(BF16)
    v_ref[0, 1, rows, :] = jnp.where(low, 1.0, v).astype(BF16)


def _proj_kernel(x_ref, m_ref, w_ref, cr_ref, sr_ref, ca_ref, sa_ref,
                 r_ref, aq_ref, ak_ref, av_ref, ag_ref):
    for t in range(x_ref.shape[1] // PROJ_SUB):
        rows = slice(t * PROJ_SUB, (t + 1) * PROJ_SUB)
        hb = _norm_mod(x_ref[0, rows, :], m_ref).astype(BF16)
        cr, sr, ca, sa = cr_ref[rows, :], sr_ref[rows, :], ca_ref[rows, :], sa_ref[rows, :]
        low_half = (lax.broadcasted_iota(jnp.int32, ca.shape, 1) % ATT_DH) < (ATT_DH // 2)

        def mm(c0, c1):
            return jnp.dot(hb, w_ref[:, c0:c1], preferred_element_type=F32)

        y = mm(OFF_RQ, OFF_RK)
        for h in range(RET_HEADS):
            sl = slice(h * LANES, (h + 1) * LANES)
            r_ref[0, rows, OFF_RQ + h * LANES:OFF_RQ + (h + 1) * LANES] = _rope_ret(y[:, sl], cr, sr).astype(BF16)
        y = mm(OFF_RK, OFF_RV) * (RET_D ** -0.5)
        for h in range(RET_HEADS):
            sl = slice(h * LANES, (h + 1) * LANES)
            r_ref[0, rows, OFF_RK + h * LANES:OFF_RK + (h + 1) * LANES] = _rope_ret(y[:, sl], cr, sr).astype(BF16)
        r_ref[0, rows, OFF_RV:OFF_RG] = mm(OFF_RV, OFF_RG).astype(BF16)
        r_ref[0, rows, OFF_RG:OFF_AQ] = _silu(mm(OFF_RG, OFF_AQ)).astype(BF16)
        y = mm(OFF_AQ, OFF_AK)
        for j in range(ATT_W // LANES):
            sl = slice(j * LANES, (j + 1) * LANES)
            aq_ref[0, rows, sl] = (_rope_att(y[:, sl], ca, sa, low_half) * (ATT_DH ** -0.5 * LOG2E)).astype(BF16)
        y = mm(OFF_AK, OFF_AG)
        _store_kv_groups(ak_ref, av_ref, rows, _rope_att(y[:, 0:LANES], ca, sa, low_half), y[:, LANES:2 * LANES])
        ag_ref[0, rows, :] = _silu(mm(OFF_AG, IN_COLS)).astype(BF16)


def _project(x, mod3, w_in_b, tabs):
    B, L, _ = x.shape
    tm = PROJ_ROWS
    tab_spec = pl.BlockSpec((tm, LANES), lambda b, i: (i, 0))
    row = lambda n: pl.BlockSpec((1, tm, n), lambda b, i: (b, i, 0))
    kv_spec = pl.BlockSpec((1, ATT_KV, tm, LANES), lambda b, i: (b, 0, i, 0))
    kv_shape = jax.ShapeDtypeStruct((B, ATT_KV, L, LANES), BF16)
    return pl.pallas_call(
        _proj_kernel,
        out_shape=(jax.ShapeDtypeStruct((B, L, 4 * RET_W), BF16),
                   jax.ShapeDtypeStruct((B, L, ATT_W), BF16),
                   kv_shape, kv_shape,
                   jax.ShapeDtypeStruct((B, L, ATT_W), BF16)),
        grid=(B, L // tm),
        in_specs=[row(D_MODEL),
                  pl.BlockSpec((1, 1, 3 * D_MODEL), lambda b, i: (b, 0, 0)),
                  pl.BlockSpec((D_MODEL, IN_COLS), lambda b, i: (0, 0), pipeline_mode=pl.Buffered(1)),
                  tab_spec, tab_spec, tab_spec, tab_spec],
        out_specs=(row(4 * RET_W), row(ATT_W), kv_spec, kv_spec, row(ATT_W)),
        compiler_params=pltpu.CompilerParams(
            dimension_semantics=("arbitrary", "arbitrary"), vmem_limit_bytes=VMEM_LIMIT),
        name="proj",
    )(x, mod3, w_in_b, *tabs)


def _ctx_kernel(x_ref, m_ref, wk_ref, wv_ref, wa_ref, dl_ref, sf_ref, sb_ref, ck_ref, cv_ref):
    lc = x_ref.shape[1]
    hb = _norm_mod(x_ref[0], m_ref).astype(BF16)
    yk = jnp.dot(hb, wk_ref[...], preferred_element_type=F32)
    yv = jnp.dot(hb, wv_ref[...], preferred_element_type=F32)
    ya = jnp.dot(hb, wa_ref[...], preferred_element_type=F32)
    _store_kv_groups(ck_ref, cv_ref, slice(0, lc), ya[:, 0:LANES], ya[:, LANES:2 * LANES])
    lg = _log_sigmoid(dl_ref[...])
    pos = lax.broadcasted_iota(jnp.int32, (lc, 1), 0).astype(F32)
    for h in range(RET_HEADS):
        k = yk[:, h * RET_D:(h + 1) * RET_D] * (RET_D ** -0.5)
        v = yv[:, h * RET_D:(h + 1) * RET_D].astype(BF16)
        wf = jnp.exp(lg[0, h, 0:1, :] * (lc - 1.0 - pos))
        wb = jnp.exp(lg[1, h, 0:1, :] * pos)
        dn = (((0,), (0,)), ((), ()))
        sf_ref[0, h] = lax.dot_general((k * wf).astype(BF16), v, dn, preferred_element_type=F32)
        sb_ref[0, h] = lax.dot_general((k * wb).astype(BF16), v, dn, preferred_element_type=F32)


def _context(ctx, mod3, w_in_b, dl):
    B, lc, _ = ctx.shape
    wcol = lambda off, n: pl.BlockSpec((D_MODEL, n), lambda b: (0, off // n))
    st =jax.ShapeDtypeStruct((B, RET_HEADS, RET_D, RET_D), F32)
    kv = jax.ShapeDtypeStruct((B, ATT_KV, lc, LANES), BF16)
    st_spec = pl.BlockSpec((1, RET_HEADS, RET_D, RET_D), lambda b: (b, 0, 0, 0))
    kv_spec = pl.BlockSpec((1, ATT_KV, lc, LANES), lambda b: (b, 0, 0, 0))
    ctx_row = mod3.shape[0] // 2
    return pl.pallas_call(
        _ctx_kernel,
        out_shape=(st, st, kv, kv),
        grid=(B,),
        in_specs=[pl.BlockSpec((1, lc, D_MODEL), lambda b: (b, 0, 0)),
                  pl.BlockSpec((1, 1, 3 * D_MODEL), lambda b: (ctx_row, 0, 0)),
                  wcol(OFF_RK, RET_W), wcol(OFF_RV, RET_W), wcol(OFF_AK, 2 * LANES),
                  pl.BlockSpec(dl.shape, lambda b: (0, 0, 0, 0))],
        out_specs=(st_spec, st_spec, kv_spec, kv_spec),
        compiler_params=pltpu.CompilerParams(
            dimension_semantics=("arbitrary",), vmem_limit_bytes=VMEM_LIMIT),
        name="ctx",
    )(ctx, mod3, w_in_b, w_in_b, w_in_b, dl)


def _ret_kernel(q_ref, k_ref, v_ref, g_ref, sf_ref, sb_ref, dl_ref, gn_ref, o_ref, u_scr, st_scr):
    L = q_ref.shape[1]
    C = RET_CHUNK
    nch = L // C
    lg = _log_sigmoid(dl_ref[...])
    lgf, lgb = lg[0, 0, 0:1, :], lg[1, 0, 0:1, :]
    lgf1, lgb1 = lgf[:, 0:1], lgb[:, 0:1]
    pos = lax.broadcasted_iota(jnp.int32, (C, 1), 0).astype(F32)
    kf_dec = jnp.exp(lgf * (C - 1.0 - pos)).astype(BF16)
    kb_dec = jnp.exp(lgb * pos).astype(BF16)
    qf_dec = jnp.exp(lgf * (pos + 1.0))
    qb_dec = jnp.exp(lgb * (C - pos))
    cf, cb = jnp.exp(lgf * C), jnp.exp(lgb * C)
    diff = (lax.broadcasted_iota(jnp.int32, (C, C), 0)
            - lax.broadcasted_iota(jnp.int32, (C, C), 1)).astype(F32)
    decay = jnp.where(diff >= 0.0, jnp.exp(lgf1 * jnp.maximum(diff, 0.0)),
                      jnp.exp(lgb1 * jnp.maximum(-diff, 0.0)))
    tn = (((0,), (0,)), ((), ()))
    nt = (((1,), (1,)), ((), ()))

    chunk = lambda n: slice(n * C, (n + 1) * C)

    for n in range(nch):
        kc = k_ref[0, chunk(n), :]
        kd = jnp.concatenate([kc * kf_dec, kc * kb_dec], axis=1)
        u_scr[n] = lax.dot_general(kd, v_ref[0, chunk(n), :], tn, preferred_element_type=F32)

    rf, rb = sf_ref[0, 0], sb_ref[0, 0]
    for n in range(nch):
        st_scr[n, :, 0:RET_D] = rf.astype(BF16)
        rf = rf * cf + u_scr[n, 0:RET_D, :]
        m = nch - 1 - n
        st_scr[m, :, RET_D:2 * RET_D] = rb.astype(BF16)
        rb = rb * cb + u_scr[m, RET_D:2 * RET_D, :]

    gn = gn_ref[0]

    def qk(n):
        return lax.dot_general(q_ref[0, chunk(n), :], k_ref[0, chunk(n), :], nt, preferred_element_type=F32)

    s_next = qk(0)
    for n in range(nch):
        s = s_next
        cross = jnp.dot(q_ref[0, chunk(n), :], st_scr[n], preferred_element_type=F32)
        if n + 1 < nch:
            s_next = qk(n + 1)
        inner = jnp.dot((s * decay).astype(BF16), v_ref[0, chunk(n), :], preferred_element_type=F32)
        y = inner + cross[:, 0:RET_D] * qf_dec + cross[:, RET_D:2 * RET_D] * qb_dec
        mu = jnp.mean(y, axis=-1, keepdims=True)
        yc = y - mu
        var = jnp.mean(yc * yc, axis=-1, keepdims=True)
        yn = yc * lax.rsqrt(var + EPS) * gn
        o_ref[0, chunk(n), :] = (yn * g_ref[0, chunk(n), :].astype(F32)).astype(BF16)


def _retention(r, sf, sb, dl, gn):
    B, L, _ = r.shape
    col = lambda off: pl.BlockSpec((1, L, RET_D), lambda b, h: (b, 0, off + h))
    st_spec = pl.BlockSpec((1, 1, RET_D, RET_D), lambda b, h: (b, h, 0, 0))
    return pl.pallas_call(
        _ret_kernel,
        out_shape=jax.ShapeDtypeStruct((B, L, RET_W), BF16),
        grid=(B, RET_HEADS),
        in_specs=[col(0), col(RET_HEADS), col(2 * RET_HEADS), col(3 * RET_HEADS),
                  st_spec, st_spec,
                  pl.BlockSpec((2, 1, 8, LANES), lambda b, h: (0, h, 0, 0)),
                  pl.BlockSpec((1, 1, RET_D), lambda b, h: (h, 0, 0))],
        out_specs=pl.BlockSpec((1, L, RET_D), lambda b, h: (b, 0, h)),
        scratch_shapes=[pltpu.VMEM((L // RET_CHUNK, 2 * RET_D, RET_D), F32),
                        pltpu.VMEM((L // RET_CHUNK, RET_D, 2 * RET_D), BF16)],
        compiler_params=pltpu.CompilerParams(
            dimension_semantics=("arbitrary", "arbitrary"), vmem_limit_bytes=VMEM_LIMIT),
        name="ret",
    )(r, r, r, r, sf, sb, dl, gn)


def _att_kernel(q_ref, k_ref, v_ref, ck_ref, cv_ref, g_ref, sink_ref, r_ref, x_ref, m_ref, w_ref, fn_ref,
                o_ref):
    L = k_ref.shape[2]
    nb_total = L // ATT_BLOCK
    nb_step = q_ref.shape[1] // ATT_BLOCK
    nhb = ATT_W // LANES
    i = pl.program_id(1)
    r_i = lax.broadcasted_iota(jnp.int32, (ATT_BLOCK, ATT_BLOCK), 0)
    q_i = lax.broadcasted_iota(jnp.int32, (ATT_BLOCK, ATT_BLOCK), 1)
    band_prev = jnp.where(q_i <= r_i, 0.0, NEG).astype(F32)
    band_next = jnp.where(r_i <= q_i, 0.0, NEG).astype(F32)
    nt = (((1,), (1,)), ((), ()))
    tn = (((0,), (0,)), ((), ()))

    def key_rows(jb):
        n = i * nb_step + jb
        blk = lambda t: pl.ds(pl.multiple_of(t * ATT_BLOCK, ATT_BLOCK), ATT_BLOCK)
        return n, (blk(jnp.maximum(n - 1, 0)), blk(n), blk(jnp.minimum(n + 1, nb_total - 1)))

    def scores(jb, kv):
        n, (p_rows, o_rows, n_rows) = key_rows(jb)
        q = q_ref[0, jb * ATT_BLOCK:(jb + 1) * ATT_BLOCK, :]
        qall = jnp.concatenate([q[:, j * LANES:(j + 1) * LANES] for j in range(nhb)], axis=0)
        kall = jnp.concatenate([ck_ref[0, kv], k_ref[0, kv, p_rows, :], k_ref[0, kv, o_rows, :],
                                k_ref[0, kv, n_rows, :]], axis=0)
        return lax.dot_general(kall, qall, nt, preferred_element_type=F32)

    def attend(jb, kv, s):
        n, (p_rows, o_rows, n_rows) = key_rows(jb)
        lc = ck_ref.shape[2]
        bias_prev = jnp.concatenate([band_prev + jnp.where(n == 0, NEG, 0.0)] * nhb, axis=1)
        bias_next = jnp.concatenate([band_next + jnp.where(n == nb_total - 1, NEG, 0.0)] * nhb, axis=1)
        parts = [s[0:lc], s[lc:lc + ATT_BLOCK] + bias_prev, s[lc + ATT_BLOCK:lc + 2 * ATT_BLOCK],
                 s[lc + 2 * ATT_BLOCK:] + bias_next]
        sink = sink_ref[kv:kv + 1, :] * LOG2E
        m = sink
        for t in parts:
            m = jnp.maximum(m, jnp.max(t, axis=0, keepdims=True))
        p = jnp.concatenate([jnp.exp2(t - m).astype(BF16) for t in parts], axis=0)
        vaug = jnp.concatenate([cv_ref[0, kv], v_ref[0, kv, p_rows, :], v_ref[0, kv, o_rows, :],
                                v_ref[0, kv, n_rows, :]], axis=0)
        o = lax.dot_general(vaug, p, tn, preferred_element_type=F32)
        if kv == 0:
            val, ones = slice(0, ATT_DH), slice(ATT_DH, ATT_DH + 1)
        else:
            val, ones = slice(ATT_DH, 2 * ATT_DH), slice(0, 1)
        den = o[ones] + jnp.exp2(sink - m)
        return o[val] * (1.0 / den)

    def gated_block(jb, halves):
        rows = slice(jb * ATT_BLOCK, (jb + 1) * ATT_BLOCK)
        comb = jnp.concatenate(halves, axis=0)
        return jnp.concatenate(
            [(comb[:, j * LANES:(j + 1) * LANES].T
              * g_ref[0, rows, j * LANES:(j + 1) * LANES].astype(F32)).astype(BF16) for j in range(nhb)], axis=1)

    def finish(jb, att_blk):
        rows = slice(jb * ATT_BLOCK, (jb + 1) * ATT_BLOCK)
        mix_in = jnp.concatenate([r_ref[0, rows, :], att_blk], axis=1)
        mixed = jnp.dot(mix_in, w_ref[...], preferred_element_type=F32)
        xn = x_ref[0, rows, :] + m_ref[0, :, 2 * D_MODEL:3 * D_MODEL] * mixed
        o_ref[0, rows, :] = xn * lax.rsqrt(jnp.mean(xn * xn, axis=-1, keepdims=True) + EPS) * fn_ref[...]

    chains = [(jb, kv) for jb in range(nb_step) for kv in range(ATT_KV)]
    s_next = scores(*chains[0])
    halves, pending = [], []
    for c, (jb, kv) in enumerate(chains):
        s = s_next
        if c + 1 < len(chains):
            s_next = scores(*chains[c + 1])
        halves.append(attend(jb, kv, s))
        if pending:
            finish(*pending.pop())
        if kv == ATT_KV - 1:
            pending.append((jb, gated_block(jb, halves)))
            halves = []
    finish(*pending.pop())


def _attention_output(aq, ak, av, ck, cv, ag, sink2, ret, x, mod3, w_out_b, fnw):
    B, L, _ = aq.shape
    lc = ck.shape[2]
    tq = ATT_ROWS
    row = lambda n: pl.BlockSpec((1, tq, n), lambda b, i: (b, i, 0))
    full = lambda n: pl.BlockSpec((1, ATT_KV, n, LANES), lambda b, i: (b, 0, 0, 0))
    return pl.pallas_call(
        _att_kernel,
        out_shape=jax.ShapeDtypeStruct((B, L, D_MODEL), F32),
        grid=(B, L // tq),
        in_specs=[row(ATT_W), full(L), full(L), full(lc), full(lc), row(ATT_W),
                  pl.BlockSpec(sink2.shape, lambda b, i: (0, 0)),
                  row(RET_W), row(D_MODEL),
                  pl.BlockSpec((1, 1, 3 * D_MODEL), lambda b, i: (b, 0, 0)),
                  pl.BlockSpec((RET_W + ATT_W, D_MODEL), lambda b, i: (0, 0)),
                  pl.BlockSpec((1, D_MODEL), lambda b, i: (0, 0))],
        out_specs=row(D_MODEL),
        compiler_params=pltpu.CompilerParams(
            dimension_semantics=("arbitrary", "arbitrary"), vmem_limit_bytes=VMEM_LIMIT),
        name="att",
    )(aq, ak, av, ck, cv, ag, sink2, ret, x, mod3, w_out_b, fnw)


def _rope_tables(L):
    pos = np.arange(L)
    rows, cols = (pos // GRID_W).astype(np.float64), (pos % GRID_W).astype(np.float64)

    def tables(dh):
        nf = dh // 4
        inv = ROPE_BASE ** (-np.arange(nf, dtype=np.float64) / nf)
        ang = np.concatenate([rows[:, None] * inv, cols[:, None] * inv], axis=-1)
        cos, sin = np.cos(ang), np.sin(ang)
        reps = LANES // dh
        return (jnp.asarray(np.tile(np.concatenate([cos, cos], axis=-1), (1, reps)), F32),
                jnp.asarray(np.tile(np.concatenate([-sin, sin], axis=-1), (1, reps)), F32))

    cr, sr = tables(RET_D)
    ca, sa = tables(ATT_DH)
    return cr, sr, ca, sa


def kernel(x, c, ctx, c_ctx, w_ada, b_ada, w_in, ret_decay_logit, ret_gn_w, att_sink, w_out, final_norm_w):
    B, L, _ = x.shape
    assert w_ada.shape[0] == 1, "single-layer trunk"
    perm = _att_pair_perm()

    cc = jnp.concatenate([c, c_ctx[None, :], jnp.zeros((B - 1, D_MODEL), F32)], axis=0)
    mod3 = _modulation(cc, w_ada[0], b_ada[0][None, :]).reshape(2 * B, 1, 3 * D_MODEL)

    in_cols = np.arange(IN_COLS)
    in_cols[OFF_AQ:OFF_AK] = OFF_AQ + perm
    in_cols[OFF_AG:] = OFF_AG + perm
    w_in_b = w_in[0][:, in_cols].astype(BF16)
    out_rows = np.concatenate([np.arange(RET_W), RET_W + perm])
    w_out_b = w_out[0][out_rows].astype(BF16)
    dl = jnp.broadcast_to(ret_decay_logit[0].astype(F32)[:, :, None, None], (2, RET_HEADS, 8, LANES))

    r, aq, ak, av, ag = _project(x, mod3, w_in_b, _rope_tables(L))
    sf, sb, ck, cv = _context(ctx, mod3, w_in_b, dl)
    ret = _retention(r, sf, sb, dl, ret_gn_w[0].reshape(RET_HEADS, 1, RET_D))

    sink2 = jnp.repeat(att_sink[0].astype(F32).reshape(ATT_KV, ATT_HEADS // ATT_KV), LANES, axis=1)
    return _attention_output(aq, ak, av, ck, cv, ag, sink2, ret, x, mod3, w_out_b, final_norm_w[None, :])
```

```python
import numpy as np
import jax
import jax.numpy as jnp
from jax import lax
from jax.experimental import pallas as pl
from jax.experimental.pallas import tpu as pltpu

F32 = jnp.float32
BF16 = jnp.bfloat16

D_MODEL = 1024
GRID_W = 64
RET_HEADS = 4
RET_D = 128
RET_W = RET_HEADS * RET_D
ATT_HEADS = 8
ATT_KV = 2
ATT_DH = 64
ATT_W = ATT_HEADS * ATT_DH
ATT_BLOCK = 128
WINDOW = 128
ROPE_BASE = 10000.0
EPS = 1e-6
NEG = -1e30
LOG2E = 1.4426950408889634
IN_COLS = 4 * RET_W + 2 * ATT_W + 2 * ATT_KV * ATT_DH
OFF_RQ, OFF_RK, OFF_RV, OFF_RG = 0, RET_W, 2 * RET_W, 3 * RET_W
OFF_AQ = 4 * RET_W
OFF_AK = OFF_AQ + ATT_W
OFF_AV = OFF_AK + ATT_KV * ATT_DH
OFF_AG = OFF_AV + ATT_KV * ATT_DH

LANES = 128
VMEM_LIMIT = 48 * 1024 * 1024

PROJ_ROWS = 1024
PROJ_SUB = 256
RET_CHUNK = 256
RET_UNROLL = 16
ATT_ROWS = 1024
ATT_TILE = 32
ATT_KCHUNK = 256
OUT_TILE = 16


def _silu(t):
    return t / (1.0 + jnp.exp(-t))


def _log_sigmoid(t):
    return -(jnp.maximum(-t, 0.0) + jnp.log1p(jnp.exp(-jnp.abs(t))))


def _mod_kernel(c_ref, w_ref, b_ref, o_ref):
    s = _silu(c_ref[...])
    o_ref[...] = jnp.dot(s, w_ref[...], preferred_element_type=F32,
                         precision=lax.Precision.HIGHEST) + b_ref[...]


def _modulation(cc, w_ada, b_ada):
    n = w_ada.shape[1]
    bn = 512
    return pl.pallas_call(
        _mod_kernel,
        out_shape=jax.ShapeDtypeStruct((cc.shape[0], n), F32),
        grid=(n // bn,),
        in_specs=[pl.BlockSpec(cc.shape, lambda i: (0, 0)),
                  pl.BlockSpec((D_MODEL, bn), lambda i: (0, i)),
                  pl.BlockSpec((1, bn), lambda i: (0, i))],
        out_specs=pl.BlockSpec((cc.shape[0], bn), lambda i: (0, i)),
        compiler_params=pltpu.CompilerParams(dimension_semantics=("arbitrary",)),
        name="mod",
    )(cc, w_ada, b_ada)


def _norm_mod(x, m_ref):
    shift = m_ref[0, :, 0:D_MODEL]
    scale = m_ref[0, :, D_MODEL:2 * D_MODEL]
    h = x * lax.rsqrt(jnp.mean(x * x, axis=-1, keepdims=True) + EPS)
    return h * (1.0 + scale) + shift


def _rope_ret(t, cos, sin):
    return t * cos + pltpu.roll(t, 64, 1) * sin


def _rope_att(t, cos, sin, low_half):
    partner = jnp.where(low_half, pltpu.roll(t, 96, 1), pltpu.roll(t, 32, 1))
    return t * cos + partner * sin


def _store_kv_groups(k_ref, v_ref, rows, k, v):
    low = lax.broadcasted_iota(jnp.int32, k.shape, 1) < ATT_DH
    k_ref[0, 0, rows, :] = jnp.where(low, k, 0.0).astype(BF16)
    k_ref[0, 1, rows, :] = jnp.where(low, 0.0, k).astype(BF16)
    vt = v.T.astype(BF16)
    v_ref[0, 0, :, rows] = vt[0:ATT_DH]
    v_ref[0, 1, :, rows] = vt[ATT_DH:2 * ATT_DH]


def _proj_kernel(x_ref, m_ref, w_ref, cr_ref, sr_ref, ca_ref, sa_ref,
                 r_ref, aq_ref, ak_ref, av_ref, ag_ref):
    for t in range(x_ref.shape[1] // PROJ_SUB):
        rows = slice(t * PROJ_SUB, (t + 1) * PROJ_SUB)
        hb = _norm_mod(x_ref[0, rows, :], m_ref).astype(BF16)
        cr, sr, ca, sa = cr_ref[rows, :], sr_ref[rows, :], ca_ref[rows, :], sa_ref[rows, :]
        low_half = (lax.broadcasted_iota(jnp.int32, ca.shape, 1) % ATT_DH) < (ATT_DH // 2)

        def mm(c0, c1):
            return jnp.dot(hb, w_ref[:, c0:c1], preferred_element_type=F32)

        y = mm(OFF_RQ, OFF_RK)
        for h in range(RET_HEADS):
            sl = slice(h * LANES, (h + 1) * LANES)
            r_ref[0, rows, OFF_RQ + h * LANES:OFF_RQ + (h + 1) * LANES] = _rope_ret(y[:, sl], cr, sr).astype(BF16)
        y = mm(OFF_RK, OFF_RV) * (RET_D ** -0.5)
        for h in range(RET_HEADS):
            sl = slice(h * LANES, (h + 1) * LANES)
            r_ref[0, rows, OFF_RK + h * LANES:OFF_RK + (h + 1) * LANES] = _rope_ret(y[:, sl], cr, sr).astype(BF16)
        r_ref[0, rows, OFF_RV:OFF_RG] = mm(OFF_RV, OFF_RG).astype(BF16)
        r_ref[0, rows, OFF_RG:OFF_AQ] = _silu(mm(OFF_RG, OFF_AQ)).astype(BF16)
        y = mm(OFF_AQ, OFF_AK)
        for j in range(ATT_W // LANES):
            sl = slice(j * LANES, (j + 1) * LANES)
            aq_ref[0, rows, sl] = (_rope_att(y[:, sl], ca, sa, low_half) * (ATT_DH ** -0.5 * LOG2E)).astype(BF16)
        y = mm(OFF_AK, OFF_AG)
        _store_kv_groups(ak_ref, av_ref, rows, _rope_att(y[:, 0:LANES], ca, sa, low_half), y[:, LANES:2 * LANES])
        ag_ref[0, rows, :] = _silu(mm(OFF_AG, IN_COLS)).astype(BF16)


def _project(x, mod3, w_in_b, tabs):
    B, L, _ = x.shape
    tm = PROJ_ROWS
    tab_spec = pl.BlockSpec((tm, LANES), lambda b, i: (i, 0))
    row = lambda n: pl.BlockSpec((1, tm, n), lambda b, i: (b, i, 0))
    k_spec = pl.BlockSpec((1, ATT_KV, tm, LANES), lambda b, i: (b, 0, i, 0))
    vt_spec = pl.BlockSpec((1, ATT_KV, ATT_DH, tm), lambda b, i: (b, 0, 0, i))
    return pl.pallas_call(
        _proj_kernel,
        out_shape=(jax.ShapeDtypeStruct((B, L, 4 * RET_W), BF16),
                   jax.ShapeDtypeStruct((B, L, ATT_W), BF16),
                   jax.ShapeDtypeStruct((B, ATT_KV, L, LANES), BF16),
                   jax.ShapeDtypeStruct((B, ATT_KV, ATT_DH, L), BF16),
                   jax.ShapeDtypeStruct((B, L, ATT_W), BF16)),
        grid=(B, L // tm),
        in_specs=[row(D_MODEL),
                  pl.BlockSpec((1, 1, 3 * D_MODEL), lambda b, i: (b, 0, 0)),
                  pl.BlockSpec((D_MODEL, IN_COLS), lambda b, i: (0, 0), pipeline_mode=pl.Buffered(1)),
                  tab_spec, tab_spec, tab_spec, tab_spec],
        out_specs=(row(4 * RET_W), row(ATT_W), k_spec, vt_spec, row(ATT_W)),
        compiler_params=pltpu.CompilerParams(
            dimension_semantics=("arbitrary", "arbitrary"), vmem_limit_bytes=VMEM_LIMIT),
        name="proj",
    )(x, mod3, w_in_b, *tabs)


def _ctx_kernel(x_ref, m_ref, wk_ref, wv_ref, wa_ref, dl_ref, sf_ref, sb_ref, ck_ref, cv_ref):
    lc = x_ref.shape[1]
    hb = _norm_mod(x_ref[0], m_ref).astype(BF16)
    yk = jnp.dot(hb, wk_ref[...], preferred_element_type=F32)
    yv = jnp.dot(hb, wv_ref[...], preferred_element_type=F32)
    ya = jnp.dot(hb, wa_ref[...], preferred_element_type=F32)
    _store_kv_groups(ck_ref, cv_ref, slice(0, lc), ya[:, 0:LANES], ya[:, LANES:2 * LANES])
    lg = _log_sigmoid(dl_ref[...])
    pos = lax.broadcasted_iota(jnp.int32, (lc, 1), 0).astype(F32)
    for h in range(RET_HEADS):
        k = yk[:, h * RET_D:(h + 1) * RET_D] * (RET_D ** -0.5)
        v = yv[:, h * RET_D:(h + 1) * RET_D].astype(BF16)
        wf = jnp.exp(lg[0, h, 0:1, :] * (lc - 1.0 - pos))
        wb = jnp.exp(lg[1, h, 0:1, :] * pos)
        dn = (((0,), (0,)), ((), ()))
        sf_ref[0, h] = lax.dot_general((k * wf).astype(BF16), v, dn, preferred_element_type=F32)
        sb_ref[0, h] = lax.dot_general((k * wb).astype(BF16), v, dn, preferred_element_type=F32)


def _context(ctx, mod3, w_in_b, dl):
    B, lc, _ = ctx.shape
    wcol = lambda off, n: pl.BlockSpec((D_MODEL, n), lambda b: (0, off // n))
    st =jax.ShapeDtypeStruct((B, RET_HEADS, RET_D, RET_D), F32)
    st_spec = pl.BlockSpec((1, RET_HEADS, RET_D, RET_D), lambda b: (b, 0, 0, 0))
    k_spec = pl.BlockSpec((1, ATT_KV, lc, LANES), lambda b: (b, 0, 0, 0))
    vt_spec = pl.BlockSpec((1, ATT_KV, ATT_DH, lc), lambda b: (b, 0, 0, 0))
    ctx_row = mod3.shape[0] // 2
    return pl.pallas_call(
        _ctx_kernel,
        out_shape=(st, st, jax.ShapeDtypeStruct((B, ATT_KV, lc, LANES), BF16),
                   jax.ShapeDtypeStruct((B, ATT_KV, ATT_DH, lc), BF16)),
        grid=(B,),
        in_specs=[pl.BlockSpec((1, lc, D_MODEL), lambda b: (b, 0, 0)),
                  pl.BlockSpec((1, 1, 3 * D_MODEL), lambda b: (ctx_row, 0, 0)),
                  wcol(OFF_RK, RET_W), wcol(OFF_RV, RET_W), wcol(OFF_AK, 2 * LANES),
                  pl.BlockSpec(dl.shape, lambda b: (0, 0, 0, 0))],
        out_specs=(st_spec, st_spec, k_spec, vt_spec),
        compiler_params=pltpu.CompilerParams(
            dimension_semantics=("arbitrary",), vmem_limit_bytes=VMEM_LIMIT),
        name="ctx",
    )(ctx, mod3, w_in_b, w_in_b, w_in_b, dl)


def _ret_kernel(q_ref, k_ref, v_ref, g_ref, sf_ref, sb_ref, dl_ref, gn_ref, o_ref, u_scr, st_scr):
    L = q_ref.shape[1]
    C = RET_CHUNK
    nch = L // C
    lg = _log_sigmoid(dl_ref[...])
    lgf, lgb = lg[0, 0, 0:1, :], lg[1, 0, 0:1, :]
    lgf1, lgb1 = lgf[:, 0:1], lgb[:, 0:1]
    pos = lax.broadcasted_iota(jnp.int32, (C, 1), 0).astype(F32)
    kf_dec = jnp.exp(lgf * (C - 1.0 - pos)).astype(BF16)
    kb_dec = jnp.exp(lgb * pos).astype(BF16)
    qf_dec = jnp.exp(lgf * (pos + 1.0))
    qb_dec = jnp.exp(lgb * (C - pos))
    cf, cb = jnp.exp(lgf * C), jnp.exp(lgb * C)
    diff = (lax.broadcasted_iota(jnp.int32, (C, C), 0)
            - lax.broadcasted_iota(jnp.int32, (C, C), 1)).astype(F32)
    decay = jnp.where(diff >= 0.0, jnp.exp(lgf1 * jnp.maximum(diff, 0.0)),
                      jnp.exp(lgb1 * jnp.maximum(-diff, 0.0)))
    tn = (((0,), (0,)), ((), ()))
    nt = (((1,), (1,)), ((), ()))

    chunk = lambda n: slice(n * C, (n + 1) * C)

    for n in range(nch):
        kc = k_ref[0, chunk(n), :]
        kd = jnp.concatenate([kc * kf_dec, kc * kb_dec], axis=1)
        u_scr[n] = lax.dot_general(kd, v_ref[0, chunk(n), :], tn, preferred_element_type=F32)

    rf, rb = sf_ref[0, 0], sb_ref[0, 0]
    for n in range(nch):
        st_scr[n, :, 0:RET_D] = rf.astype(BF16)
        rf = rf * cf + u_scr[n, 0:RET_D, :]
        m = nch - 1 - n
        st_scr[m, :, RET_D:2 * RET_D] = rb.astype(BF16)
        rb = rb * cb + u_scr[m, RET_D:2 * RET_D, :]

    gn = gn_ref[0]

    def qk(n):
        return lax.dot_general(q_ref[0, chunk(n), :], k_ref[0, chunk(n), :], nt, preferred_element_type=F32)

    s_next = qk(0)
    for n in range(nch):
        s = s_next
        cross = jnp.dot(q_ref[0, chunk(n), :], st_scr[n], preferred_element_type=F32)
        if n + 1 < nch:
            s_next = qk(n + 1)
        inner = jnp.dot((s * decay).astype(BF16), v_ref[0, chunk(n), :], preferred_element_type=F32)
        y = inner + cross[:, 0:RET_D] * qf_dec + cross[:, RET_D:2 * RET_D] * qb_dec
        mu = jnp.mean(y, axis=-1, keepdims=True)
        yc = y - mu
        var = jnp.mean(yc * yc, axis=-1, keepdims=True)
        yn = yc * lax.rsqrt(var + EPS) * gn
        o_ref[0, chunk(n), :] = (yn * g_ref[0, chunk(n), :].astype(F32)).astype(BF16)


def _retention(r, sf, sb, dl, gn):
    B, L, _ = r.shape
    col = lambda off: pl.BlockSpec((1, L, RET_D), lambda b, h: (b, 0, off + h))
    st_spec = pl.BlockSpec((1, 1, RET_D, RET_D), lambda b, h: (b, h, 0, 0))
    return pl.pallas_call(
        _ret_kernel,
        out_shape=jax.ShapeDtypeStruct((B, L, RET_W), BF16),
        grid=(B, RET_HEADS),
        in_specs=[col(0), col(RET_HEADS), col(2 * RET_HEADS), col(3 * RET_HEADS),
                  st_spec, st_spec,
                  pl.BlockSpec((2, 1, 8, LANES), lambda b, h: (0, h, 0, 0)),
                  pl.BlockSpec((1, 1, RET_D), lambda b, h: (h, 0, 0))],
        out_specs=pl.BlockSpec((1, L, RET_D), lambda b, h: (b, 0, h)),
        scratch_shapes=[pltpu.VMEM((L // RET_CHUNK, 2 * RET_D, RET_D), F32),
                        pltpu.VMEM((L // RET_CHUNK, RET_D, 2 * RET_D), BF16)],
        compiler_params=pltpu.CompilerParams(
            dimension_semantics=("arbitrary", "arbitrary"), vmem_limit_bytes=VMEM_LIMIT),
        name="ret",
    )(r, r, r, r, sf, sb, dl, gn)


def _att_kernel(q_ref, k_ref, v_ref, ck_ref, cv_ref, g_ref, sink_ref, r_ref, x_ref, m_ref, w_ref, fn_ref,
                o_ref, s_scr):
    L = k_ref.shape[2]
    nb_total = L // ATT_BLOCK
    nb_step = q_ref.shape[1] // ATT_BLOCK
    nhb = ATT_W // LANES
    i = pl.program_id(1)
    r_i = lax.broadcasted_iota(jnp.int32, (ATT_BLOCK, ATT_BLOCK), 0)
    q_i = lax.broadcasted_iota(jnp.int32, (ATT_BLOCK, ATT_BLOCK), 1)
    band_prev = jnp.where(q_i <= r_i, 0.0, NEG).astype(F32)
    band_next = jnp.where(r_i <= q_i, 0.0, NEG).astype(F32)
    nt = (((1,), (1,)), ((), ()))

    def key_rows(jb):
        n = i * nb_step + jb
        blk = lambda t: pl.ds(pl.multiple_of(t * ATT_BLOCK, ATT_BLOCK), ATT_BLOCK)
        return n, (blk(jnp.maximum(n - 1, 0)), blk(n), blk(jnp.minimum(n + 1, nb_total - 1)))

    def scores(jb, kv):
        n, (p_rows, o_rows, n_rows) = key_rows(jb)
        q = q_ref[0, jb * ATT_BLOCK:(jb + 1) * ATT_BLOCK, :]
        qall = jnp.concatenate([q[:, j * LANES:(j + 1) * LANES] for j in range(nhb)], axis=0)
        kall = jnp.concatenate([ck_ref[0, kv], k_ref[0, kv, p_rows, :], k_ref[0, kv, o_rows, :],
                                k_ref[0, kv, n_rows, :]], axis=0)
        return lax.dot_general(kall, qall, nt, preferred_element_type=F32)

    lc = ck_ref.shape[2]
    n_keys = lc + 3 * ATT_BLOCK
    prev0, next0 = lc, lc + 2 * ATT_BLOCK

    def slabs(t):
        return [t[r:r + 8] for r in range(0, t.shape[0], 8)]

    def mask_and_max(jb, kv, s, slot):
        n, _ = key_rows(jb)
        bias = {prev0: jnp.concatenate([band_prev + jnp.where(n == 0, NEG, 0.0)] * nhb, axis=1),
                next0: jnp.concatenate([band_next + jnp.where(n == nb_total - 1, NEG, 0.0)] * nhb, axis=1)}
        macc = None
        for r in range(0, n_keys, ATT_TILE):
            t = s[r:r + ATT_TILE]
            for b0, b in bias.items():
                if b0 <= r < b0 + ATT_BLOCK:
                    t = t + b[r - b0:r - b0 + ATT_TILE]
            s_scr[slot, r:r + ATT_TILE, :] = t
            for sl in slabs(t):
                macc = sl if macc is None else jnp.maximum(macc, sl)
        return jnp.maximum(sink_ref[kv:kv + 1, :] * LOG2E, jnp.max(macc, axis=0, keepdims=True))

    def attend(jb, kv, slot, m):
        _, (p_rows, o_rows, n_rows) = key_rows(jb)
        vt = jnp.concatenate([cv_ref[0, kv], v_ref[0, kv, :, p_rows], v_ref[0, kv, :, o_rows],
                              v_ref[0, kv, :, n_rows]], axis=1)
        dacc, o = None, None
        for k0 in range(0, n_keys, ATT_KCHUNK):
            k1 = min(k0 + ATT_KCHUNK, n_keys)
            p = []
            for r in range(k0, k1, ATT_TILE):
                e = jnp.exp2(s_scr[slot, r:r + ATT_TILE, :] - m)
                for sl in slabs(e):
                    dacc = sl if dacc is None else dacc + sl
                p.append(e.astype(BF16))
            part = jnp.dot(vt[:, k0:k1], jnp.concatenate(p, axis=0), preferred_element_type=F32)
            o = part if o is None else o + part
        den = jnp.exp2(sink_ref[kv:kv + 1, :] * LOG2E - m) + jnp.sum(dacc, axis=0, keepdims=True)
        return o * (1.0 / den)

    def gated_block(jb, halves):
        rows = slice(jb * ATT_BLOCK, (jb + 1) * ATT_BLOCK)
        comb = jnp.concatenate(halves, axis=0)
        return jnp.concatenate(
            [(comb[:, j * LANES:(j + 1) * LANES].T
              * g_ref[0, rows, j * LANES:(j + 1) * LANES].astype(F32)).astype(BF16) for j in range(nhb)], axis=1)

    def finish(jb, att_blk):
        rows = slice(jb * ATT_BLOCK, (jb + 1) * ATT_BLOCK)
        mix_in = jnp.concatenate([r_ref[0, rows, :], att_blk], axis=1)
        mixed = jnp.dot(mix_in, w_ref[...], preferred_element_type=F32)
        gate, fn = m_ref[0, :, 2 * D_MODEL:3 * D_MODEL], fn_ref[...]
        for r in range(0, ATT_BLOCK, OUT_TILE):
            rr = slice(jb * ATT_BLOCK + r, jb * ATT_BLOCK + r + OUT_TILE)
            xn = x_ref[0, rr, :] + gate * mixed[r:r + OUT_TILE]
            o_ref[0, rr, :] = xn * lax.rsqrt(jnp.mean(xn * xn, axis=-1, keepdims=True) + EPS) * fn

    chains = [(jb, kv) for jb in range(nb_step) for kv in range(ATT_KV)]
    nc = len(chains)
    m_next = mask_and_max(*chains[0], scores(*chains[0]), 0)
    s_next = scores(*chains[1])
    halves, pending = [], []
    for c, (jb, kv) in enumerate(chains):
        m = m_next
        if c + 1 < nc:
            m_next = mask_and_max(*chains[c + 1], s_next, (c + 1) % 2)
        if c + 2 < nc:
            s_next = scores(*chains[c + 2])
        halves.append(attend(jb, kv, c % 2, m))
        if pending:
            finish(*pending.pop())
        if kv == ATT_KV - 1:
            pending.append((jb, gated_block(jb, halves)))
            halves = []
    finish(*pending.pop())


def _attention_output(aq, ak, av, ck, cv, ag, sink2, ret, x, mod3, w_out_b, fnw):
    B, L, _ = aq.shape
    lc = ck.shape[2]
    tq = ATT_ROWS
    row = lambda n: pl.BlockSpec((1, tq, n), lambda b, i: (b, i, 0))
    keys = lambda n: pl.BlockSpec((1, ATT_KV, n, LANES), lambda b, i: (b, 0, 0, 0))
    vals = lambda n: pl.BlockSpec((1, ATT_KV, ATT_DH, n), lambda b, i: (b, 0, 0, 0))
    return pl.pallas_call(
        _att_kernel,
        out_shape=jax.ShapeDtypeStruct((B, L, D_MODEL), F32),
        grid=(B, L // tq),
        in_specs=[row(ATT_W), keys(L), vals(L), keys(lc), vals(lc), row(ATT_W),
                  pl.BlockSpec(sink2.shape, lambda b, i: (0, 0)),
                  row(RET_W), row(D_MODEL),
                  pl.BlockSpec((1, 1, 3 * D_MODEL), lambda b, i: (b, 0, 0)),
                  pl.BlockSpec((RET_W + ATT_W, D_MODEL), lambda b, i: (0, 0)),
                  pl.BlockSpec((1, D_MODEL), lambda b, i: (0, 0))],
        out_specs=row(D_MODEL),
        scratch_shapes=[pltpu.VMEM((2, lc + 3 * ATT_BLOCK, ATT_W), F32)],
        compiler_params=pltpu.CompilerParams(
            dimension_semantics=("arbitrary", "arbitrary"), vmem_limit_bytes=VMEM_LIMIT),
        name="att",
    )(aq, ak, av, ck, cv, ag, sink2, ret, x, mod3, w_out_b, fnw)


def _rope_tables(L):
    pos = np.arange(L)
    rows, cols = (pos // GRID_W).astype(np.float64), (pos % GRID_W).astype(np.float64)

    def tables(dh):
        nf = dh // 4
        inv = ROPE_BASE ** (-np.arange(nf, dtype=np.float64) / nf)
        ang = np.concatenate([rows[:, None] * inv, cols[:, None] * inv], axis=-1)
        cos, sin = np.cos(ang), np.sin(ang)
        reps = LANES // dh
        return (jnp.asarray(np.tile(np.concatenate([cos, cos], axis=-1), (1, reps)), F32),
                jnp.asarray(np.tile(np.concatenate([-sin, sin], axis=-1), (1, reps)), F32))

    cr, sr = tables(RET_D)
    ca, sa = tables(ATT_DH)
    return cr, sr, ca, sa


def kernel(x, c, ctx, c_ctx, w_ada, b_ada, w_in, ret_decay_logit, ret_gn_w, att_sink, w_out, final_norm_w):
    B, L, _ = x.shape
    assert w_ada.shape[0] == 1, "single-layer trunk"

    cc = jnp.concatenate([c, c_ctx[None, :], jnp.zeros((B - 1, D_MODEL), F32)], axis=0)
    mod3 = _modulation(cc, w_ada[0], b_ada[0][None, :]).reshape(2 * B, 1, 3 * D_MODEL)

    def pair_heads(t, axis):
        shp = t.shape[:axis] + (2, ATT_HEADS // 2, ATT_DH) + t.shape[axis + 1:]
        return jnp.swapaxes(t.reshape(shp), axis, axis + 1).reshape(t.shape)

    w, wo = w_in[0], w_out[0]
    w_in_b = jnp.concatenate([w[:, :OFF_AQ], pair_heads(w[:, OFF_AQ:OFF_AK], 1), w[:, OFF_AK:OFF_AG],
                              pair_heads(w[:, OFF_AG:], 1)], axis=1).astype(BF16)
    w_out_b = jnp.concatenate([wo[:RET_W], pair_heads(wo[RET_W:], 0)], axis=0).astype(BF16)
    dl = jnp.broadcast_to(ret_decay_logit[0].astype(F32)[:, :, None, None], (2, RET_HEADS, 8, LANES))

    r, aq, ak, av, ag = _project(x, mod3, w_in_b, _rope_tables(L))
    sf, sb, ck, cv = _context(ctx, mod3, w_in_b, dl)
    ret = _retention(r, sf, sb, dl, ret_gn_w[0].reshape(RET_HEADS, 1, RET_D))

    sink2 = jnp.repeat(att_sink[0].astype(F32).reshape(ATT_KV, ATT_HEADS // ATT_KV), LANES, axis=1)
    return _attention_output(aq, ak, av, ck, cv, ag, sink2, ret, x, mod3, w_out_b, final_norm_w[None, :])
```

```python
import numpy as np
import jax
import jax.numpy as jnp
from jax import lax
from jax.experimental import pallas as pl
from jax.experimental.pallas import tpu as pltpu

F32 = jnp.float32
BF16 = jnp.bfloat16

D_MODEL = 1024
GRID_W = 64
RET_HEADS = 4
RET_D = 128
RET_W = RET_HEADS * RET_D
ATT_HEADS = 8
ATT_KV = 2
ATT_DH = 64
ATT_W = ATT_HEADS * ATT_DH
ATT_BLOCK = 128
WINDOW = 128
ROPE_BASE = 10000.0
EPS = 1e-6
NEG = -1e30
LOG2E = 1.4426950408889634
IN_COLS = 4 * RET_W + 2 * ATT_W + 2 * ATT_KV * ATT_DH
OFF_RQ, OFF_RK, OFF_RV, OFF_RG = 0, RET_W, 2 * RET_W, 3 * RET_W
OFF_AQ = 4 * RET_W
OFF_AK = OFF_AQ + ATT_W
OFF_AV = OFF_AK + ATT_KV * ATT_DH
OFF_AG = OFF_AV + ATT_KV * ATT_DH

LANES = 128
VMEM_LIMIT = 48 * 1024 * 1024

PROJ_ROWS = 1024
PROJ_SUB = 256
RET_CHUNK = 256
ATT_ROWS = 1024


def _silu(t):
    return t / (1.0 + jnp.exp(-t))


def _log_sigmoid(t):
    return -(jnp.maximum(-t, 0.0) + jnp.log1p(jnp.exp(-jnp.abs(t))))


def _mod_kernel(c_ref, w_ref, b_ref, o_ref):
    s = _silu(c_ref[...])
    o_ref[...] = jnp.dot(s, w_ref[...], preferred_element_type=F32,
                         precision=lax.Precision.HIGHEST) + b_ref[...]


def _modulation(cc, w_ada, b_ada):
    n = w_ada.shape[1]
    bn = 512
    return pl.pallas_call(
        _mod_kernel,
        out_shape=jax.ShapeDtypeStruct((cc.shape[0], n), F32),
        grid=(n // bn,),
        in_specs=[pl.BlockSpec(cc.shape, lambda i: (0, 0)),
                  pl.BlockSpec((D_MODEL, bn), lambda i: (0, i)),
                  pl.BlockSpec((1, bn), lambda i: (0, i))],
        out_specs=pl.BlockSpec((cc.shape[0], bn), lambda i: (0, i)),
        compiler_params=pltpu.CompilerParams(dimension_semantics=("arbitrary",)),
        name="mod",
    )(cc, w_ada, b_ada)


def _norm_mod(x, m_ref):
    shift = m_ref[0, :, 0:D_MODEL]
    scale = m_ref[0, :, D_MODEL:2 * D_MODEL]
    h = x * lax.rsqrt(jnp.mean(x * x, axis=-1, keepdims=True) + EPS)
    return h * (1.0 + scale) + shift


def _rope_ret(t, cos, sin):
    return t * cos + pltpu.roll(t, 64, 1) * sin


def _rope_att(t, cos, sin, low_half):
    partner = jnp.where(low_half, pltpu.roll(t, 96, 1), pltpu.roll(t, 32, 1))
    return t * cos + partner * sin


def _store_kv_groups(k_ref, v_ref, rows, k, v):
    low = lax.broadcasted_iota(jnp.int32, k.shape, 1) < ATT_DH
    k_ref[0, 0, rows, :] = jnp.where(low, k, 0.0).astype(BF16)
    k_ref[0, 1, rows, :] = jnp.where(low, 0.0, k).astype(BF16)
    v_ref[0, 0, rows, :] = jnp.where(low, v, 1.0).astype(BF16)
    v_ref[0, 1, rows, :] = jnp.where(low, 1.0, v).astype(BF16)


def _proj_kernel(x_ref, m_ref, w_ref, cr_ref, sr_ref, ca_ref, sa_ref,
                 r_ref, aq_ref, ak_ref, av_ref, ag_ref):
    for t in range(x_ref.shape[1] // PROJ_SUB):
        rows = slice(t * PROJ_SUB, (t + 1) * PROJ_SUB)
        hb = _norm_mod(x_ref[0, rows, :], m_ref).astype(BF16)
        cr, sr, ca, sa = cr_ref[rows, :], sr_ref[rows, :], ca_ref[rows, :], sa_ref[rows, :]
        low_half = (lax.broadcasted_iota(jnp.int32, ca.shape, 1) % ATT_DH) < (ATT_DH // 2)

        def mm(c0, c1):
            return jnp.dot(hb, w_ref[:, c0:c1], preferred_element_type=F32)

        y = mm(OFF_RQ, OFF_RK)
        for h in range(RET_HEADS):
            sl = slice(h * LANES, (h + 1) * LANES)
            r_ref[0, rows, OFF_RQ + h * LANES:OFF_RQ + (h + 1) * LANES] = _rope_ret(y[:, sl], cr, sr).astype(BF16)
        y = mm(OFF_RK, OFF_RV) * (RET_D ** -0.5)
        for h in range(RET_HEADS):
            sl = slice(h * LANES, (h + 1) * LANES)
            r_ref[0, rows, OFF_RK + h * LANES:OFF_RK + (h + 1) * LANES] = _rope_ret(y[:, sl], cr, sr).astype(BF16)
        r_ref[0, rows, OFF_RV:OFF_RG] = mm(OFF_RV, OFF_RG).astype(BF16)
        r_ref[0, rows, OFF_RG:OFF_AQ] = _silu(mm(OFF_RG, OFF_AQ)).astype(BF16)
        y = mm(OFF_AQ, OFF_AK)
        for j in range(ATT_W // LANES):
            sl = slice(j * LANES, (j + 1) * LANES)
            aq_ref[0, rows, sl] = (_rope_att(y[:, sl], ca, sa, low_half) * (ATT_DH ** -0.5 * LOG2E)).astype(BF16)
        y = mm(OFF_AK, OFF_AG)
        _store_kv_groups(ak_ref, av_ref, rows, _rope_att(y[:, 0:LANES], ca, sa, low_half), y[:, LANES:2 * LANES])
        ag_ref[0, rows, :] = _silu(mm(OFF_AG, IN_COLS)).astype(BF16)


def _project(x, mod3, w_in_b, tabs):
    B, L, _ = x.shape
    tm = PROJ_ROWS
    tab_spec = pl.BlockSpec((tm, LANES), lambda b, i: (i, 0))
    row = lambda n: pl.BlockSpec((1, tm, n), lambda b, i: (b, i, 0))
    kv_spec = pl.BlockSpec((1, ATT_KV, tm, LANES), lambda b, i: (b, 0, i, 0))
    kv_shape = jax.ShapeDtypeStruct((B, ATT_KV, L, LANES), BF16)
    return pl.pallas_call(
        _proj_kernel,
        out_shape=(jax.ShapeDtypeStruct((B, L, 4 * RET_W), BF16),
                   jax.ShapeDtypeStruct((B, L, ATT_W), BF16),
                   kv_shape, kv_shape,
                   jax.ShapeDtypeStruct((B, L, ATT_W), BF16)),
        grid=(B, L // tm),
        in_specs=[row(D_MODEL),
                  pl.BlockSpec((1, 1, 3 * D_MODEL), lambda b, i: (b, 0, 0)),
                  pl.BlockSpec((D_MODEL, IN_COLS), lambda b, i: (0, 0), pipeline_mode=pl.Buffered(1)),
                  tab_spec, tab_spec, tab_spec, tab_spec],
        out_specs=(row(4 * RET_W), row(ATT_W), kv_spec, kv_spec, row(ATT_W)),
        compiler_params=pltpu.CompilerParams(
            dimension_semantics=("arbitrary", "arbitrary"), vmem_limit_bytes=VMEM_LIMIT),
        name="proj",
    )(x, mod3, w_in_b, *tabs)


def _ctx_kernel(x_ref, m_ref, wk_ref, wv_ref, wa_ref, dl_ref, sf_ref, sb_ref, ck_ref, cv_ref):
    lc = x_ref.shape[1]
    hb = _norm_mod(x_ref[0], m_ref).astype(BF16)
    yk = jnp.dot(hb, wk_ref[...], preferred_element_type=F32)
    yv = jnp.dot(hb, wv_ref[...], preferred_element_type=F32)
    ya = jnp.dot(hb, wa_ref[...], preferred_element_type=F32)
    _store_kv_groups(ck_ref, cv_ref, slice(0, lc), ya[:, 0:LANES], ya[:, LANES:2 * LANES])
    lg = _log_sigmoid(dl_ref[...])
    pos = lax.broadcasted_iota(jnp.int32, (lc, 1), 0).astype(F32)
    for h in range(RET_HEADS):
        k = yk[:, h * RET_D:(h + 1) * RET_D] * (RET_D ** -0.5)
        v = yv[:, h * RET_D:(h + 1) * RET_D].astype(BF16)
        wf = jnp.exp(lg[0, h, 0:1, :] * (lc - 1.0 - pos))
        wb = jnp.exp(lg[1, h, 0:1, :] * pos)
        dn = (((0,), (0,)), ((), ()))
        sf_ref[0, h] = lax.dot_general((k * wf).astype(BF16), v, dn, preferred_element_type=F32)
        sb_ref[0, h] = lax.dot_general((k * wb).astype(BF16), v, dn, preferred_element_type=F32)


def _context(ctx, mod3, w_in_b, dl):
    B, lc, _ = ctx.shape
    wcol = lambda off, n: pl.BlockSpec((D_MODEL, n), lambda b: (0, off // n))
    st = jax.ShapeDtypeStruct((B, RET_HEADS, RET_D, RET_D), F32)
    kv = jax.ShapeDtypeStruct((B, ATT_KV, lc, LANES), BF16)
    st_spec = pl.BlockSpec((1, RET_HEADS, RET_D, RET_D), lambda b: (b, 0, 0, 0))
    kv_spec = pl.BlockSpec((1, ATT_KV, lc, LANES), lambda b: (b, 0, 0, 0))
    ctx_row = mod3.shape[0] // 2
    return pl.pallas_call(
        _ctx_kernel,
        out_shape=(st, st, kv, kv),
        grid=(B,),
        in_specs=[pl.BlockSpec((1, lc, D_MODEL), lambda b: (b, 0, 0)),
                  pl.BlockSpec((1, 1, 3 * D_MODEL), lambda b: (ctx_row, 0, 0)),
                  wcol(OFF_RK, RET_W), wcol(OFF_RV, RET_W), wcol(OFF_AK, 2 * LANES),
                  pl.BlockSpec(dl.shape, lambda b: (0, 0, 0, 0))],
        out_specs=(st_spec, st_spec, kv_spec, kv_spec),
        compiler_params=pltpu.CompilerParams(
            dimension_semantics=("arbitrary",), vmem_limit_bytes=VMEM_LIMIT),
        name="ctx",
    )(ctx, mod3, w_in_b, w_in_b, w_in_b, dl)


def _ret_kernel(q_ref, k_ref, v_ref, g_ref, sf_ref, sb_ref, dl_ref, gn_ref, o_ref,
                uf_scr, rb_scr, kdec_scr, qdec_scr, cdec_scr, decay_scr):
    L = q_ref.shape[1]
    C = RET_CHUNK
    nch = L // C

    @pl.when(pl.program_id(1) == 0)
    def _():
        lg = _log_sigmoid(dl_ref[...])
        lgf, lgb = lg[0, 0, 0:1, :], lg[1, 0, 0:1, :]
        lgf1, lgb1 = lgf[:, 0:1], lgb[:, 0:1]
        pos = lax.broadcasted_iota(jnp.int32, (C, 1), 0).astype(F32)
        kdec_scr[0] = jnp.exp(lgf * (C - 1.0 - pos)).astype(BF16)
        kdec_scr[1] = jnp.exp(lgb * pos).astype(BF16)
        qdec_scr[0] = jnp.exp(lgf * (pos + 1.0))
        qdec_scr[1] = jnp.exp(lgb * (C - pos))
        cdec_scr[0] = jnp.broadcast_to(jnp.exp(lgf * C), (8, LANES))
        cdec_scr[1] = jnp.broadcast_to(jnp.exp(lgb * C), (8, LANES))
        diff = (lax.broadcasted_iota(jnp.int32, (C, C), 0)
                - lax.broadcasted_iota(jnp.int32, (C, C), 1)).astype(F32)
        decay_scr[...] = jnp.where(diff >= 0.0, jnp.exp(lgf1 * jnp.maximum(diff, 0.0)),
                                   jnp.exp(lgb1 * jnp.maximum(-diff, 0.0)))

    kf_dec, kb_dec = kdec_scr[0], kdec_scr[1]
    qf_dec, qb_dec = qdec_scr[0], qdec_scr[1]
    cf, cb = cdec_scr[0, 0:1, :], cdec_scr[1, 0:1, :]
    decay = decay_scr[...]
    tn = (((0,), (0,)), ((), ()))
    nt = (((1,), (1,)), ((), ()))

    chunk = lambda n: slice(n * C, (n + 1) * C)

    rb = sb_ref[0, 0]
    for n in reversed(range(nch)):
        kc = k_ref[0, chunk(n), :]
        kd = jnp.concatenate([kc * kf_dec, kc * kb_dec], axis=1)
        u = lax.dot_general(kd, v_ref[0, chunk(n), :], tn, preferred_element_type=F32)
        uf_scr[n] = u[0:RET_D]
        rb_scr[n] = rb.astype(BF16)
        rb = rb * cb + u[RET_D:2 * RET_D]

    gn = gn_ref[0]

    def qk(n):
        return lax.dot_general(q_ref[0, chunk(n), :], k_ref[0, chunk(n), :], nt, preferred_element_type=F32)

    rf = sf_ref[0, 0]
    s_next = qk(0)
    for n in range(nch):
        s = s_next
        states = jnp.concatenate([rf.astype(BF16), rb_scr[n]], axis=1)
        cross = jnp.dot(q_ref[0, chunk(n), :], states, preferred_element_type=F32)
        if n + 1 < nch:
            s_next = qk(n + 1)
            rf = rf * cf + uf_scr[n]
        inner = jnp.dot((s * decay).astype(BF16), v_ref[0, chunk(n), :], preferred_element_type=F32)
        y = inner + cross[:, 0:RET_D] * qf_dec + cross[:, RET_D:2 * RET_D] * qb_dec
        mu = jnp.mean(y, axis=-1, keepdims=True)
        yc = y - mu
        var = jnp.mean(yc * yc, axis=-1, keepdims=True)
        yn = yc * lax.rsqrt(var + EPS) * gn
        o_ref[0, chunk(n), :] = (yn * g_ref[0, chunk(n), :].astype(F32)).astype(BF16)


def _retention(r, sf, sb, dl, gn):
    B, L, _ = r.shape
    col = lambda off: pl.BlockSpec((1, L, RET_D), lambda h, b: (b, 0, off + h))
    st_spec = pl.BlockSpec((1, 1, RET_D, RET_D), lambda h, b: (b, h, 0, 0))
    C = RET_CHUNK
    return pl.pallas_call(
        _ret_kernel,
        out_shape=jax.ShapeDtypeStruct((B, L, RET_W), BF16),
        grid=(RET_HEADS, B),
        in_specs=[col(0), col(RET_HEADS), col(2 * RET_HEADS), col(3 * RET_HEADS),
                  st_spec, st_spec,
                  pl.BlockSpec((2, 1, 8, LANES), lambda h, b: (0, h, 0, 0)),
                  pl.BlockSpec((1, 1, RET_D), lambda h, b: (h, 0, 0))],
        out_specs=pl.BlockSpec((1, L, RET_D), lambda h, b: (b, 0, h)),
        scratch_shapes=[pltpu.VMEM((L // C, RET_D, RET_D), F32),
                        pltpu.VMEM((L // C, RET_D, RET_D), BF16),
                        pltpu.VMEM((2, C, LANES), BF16),
                        pltpu.VMEM((2, C, LANES), F32),
                        pltpu.VMEM((2, 8, LANES), F32),
                        pltpu.VMEM((C, C), F32)],
        compiler_params=pltpu.CompilerParams(
            dimension_semantics=("arbitrary", "arbitrary"), vmem_limit_bytes=VMEM_LIMIT),
        name="ret",
    )(r, r, r, r, sf, sb, dl, gn)


def _att_kernel(q_ref, k_ref, v_ref, ck_ref, cv_ref, g_ref, sink_ref, r_ref, x_ref, m_ref, w_ref, fn_ref,
                o_ref):
    L = k_ref.shape[2]
    nb_total = L // ATT_BLOCK
    nb_step = q_ref.shape[1] // ATT_BLOCK
    nhb = ATT_W // LANES
    i = pl.program_id(1)
    r_i = lax.broadcasted_iota(jnp.int32, (ATT_BLOCK, ATT_BLOCK), 0)
    q_i = lax.broadcasted_iota(jnp.int32, (ATT_BLOCK, ATT_BLOCK), 1)
    band_prev = jnp.where(q_i <= r_i, 0.0, NEG).astype(F32)
    band_next = jnp.where(r_i <= q_i, 0.0, NEG).astype(F32)
    nt = (((1,), (1,)), ((), ()))
    tn = (((0,), (0,)), ((), ()))

    def key_rows(jb):
        n = i * nb_step + jb
        blk = lambda t: pl.ds(pl.multiple_of(t * ATT_BLOCK, ATT_BLOCK), ATT_BLOCK)
        return n, (blk(jnp.maximum(n - 1, 0)), blk(n), blk(jnp.minimum(n + 1, nb_total - 1)))

    def scores(jb, kv):
        n, (p_rows, o_rows, n_rows) = key_rows(jb)
        q = q_ref[0, jb * ATT_BLOCK:(jb + 1) * ATT_BLOCK, :]
        qall = jnp.concatenate([q[:, j * LANES:(j + 1) * LANES] for j in range(nhb)], axis=0)
        kall = jnp.concatenate([ck_ref[0, kv], k_ref[0, kv, p_rows, :], k_ref[0, kv, o_rows, :],
                                k_ref[0, kv, n_rows, :]], axis=0)
        return lax.dot_general(kall, qall, nt, preferred_element_type=F32)

    def attend(jb, kv, s):
        n, (p_rows, o_rows, n_rows) = key_rows(jb)
        lc = ck_ref.shape[2]
        bias_prev = jnp.concatenate([band_prev + jnp.where(n == 0, NEG, 0.0)] * nhb, axis=1)
        bias_next = jnp.concatenate([band_next + jnp.where(n == nb_total - 1, NEG, 0.0)] * nhb, axis=1)
        parts = [s[0:lc], s[lc:lc + ATT_BLOCK] + bias_prev, s[lc + ATT_BLOCK:lc + 2 * ATT_BLOCK],
                 s[lc + 2 * ATT_BLOCK:] + bias_next]
        sink = sink_ref[kv:kv + 1, :] * LOG2E
        m = sink
        for t in parts:
            m = jnp.maximum(m, jnp.max(t, axis=0, keepdims=True))
        p = jnp.concatenate([jnp.exp2(t - m).astype(BF16) for t in parts], axis=0)
        vaug = jnp.concatenate([cv_ref[0, kv], v_ref[0, kv, p_rows, :], v_ref[0, kv, o_rows, :],
                                v_ref[0, kv, n_rows, :]], axis=0)
        o = lax.dot_general(vaug, p, tn, preferred_element_type=F32)
        if kv == 0:
            val, ones = slice(0, ATT_DH), slice(ATT_DH, ATT_DH + 1)
        else:
            val, ones = slice(ATT_DH, 2 * ATT_DH), slice(0, 1)
        den = o[ones] + jnp.exp2(sink - m)
        return o[val] * (1.0 / den)

    def gated_block(jb, halves):
        rows = slice(jb * ATT_BLOCK, (jb + 1) * ATT_BLOCK)
        comb = jnp.concatenate(halves, axis=0)
        return jnp.concatenate(
            [(comb[:, j * LANES:(j + 1) * LANES].T
              * g_ref[0, rows, j * LANES:(j + 1) * LANES].astype(F32)).astype(BF16) for j in range(nhb)], axis=1)

    def finish(jb, att_blk):
        rows = slice(jb * ATT_BLOCK, (jb + 1) * ATT_BLOCK)
        mix_in = jnp.concatenate([r_ref[0, rows, :], att_blk], axis=1)
        mixed = jnp.dot(mix_in, w_ref[...], preferred_element_type=F32)
        xn = x_ref[0, rows, :] + m_ref[0, :, 2 * D_MODEL:3 * D_MODEL] * mixed
        o_ref[0, rows, :] = xn * lax.rsqrt(jnp.mean(xn * xn, axis=-1, keepdims=True) + EPS) * fn_ref[...]

    chains = [(jb, kv) for jb in range(nb_step) for kv in range(ATT_KV)]
    s_next = scores(*chains[0])
    halves, pending = [], []
    for c, (jb, kv) in enumerate(chains):
        s = s_next
        if c + 1 < len(chains):
            s_next = scores(*chains[c + 1])
        halves.append(attend(jb, kv, s))
        if pending:
            finish(*pending.pop())
        if kv == ATT_KV - 1:
            pending.append((jb, gated_block(jb, halves)))
            halves = []
    finish(*pending.pop())


def _attention_output(aq, ak, av, ck, cv, ag, sink2, ret, x, mod3, w_out_b, fnw):
    B, L, _ = aq.shape
    lc = ck.shape[2]
    tq = ATT_ROWS
    row = lambda n: pl.BlockSpec((1, tq, n), lambda b, i: (b, i, 0))
    full = lambda n: pl.BlockSpec((1, ATT_KV, n, LANES), lambda b, i: (b, 0, 0, 0))
    return pl.pallas_call(
        _att_kernel,
        out_shape=jax.ShapeDtypeStruct((B, L, D_MODEL), F32),
        grid=(B, L // tq),
        in_specs=[row(ATT_W), full(L), full(L), full(lc), full(lc), row(ATT_W),
                  pl.BlockSpec(sink2.shape, lambda b, i: (0, 0)),
                  row(RET_W), row(D_MODEL),
                  pl.BlockSpec((1, 1, 3 * D_MODEL), lambda b, i: (b, 0, 0)),
                  pl.BlockSpec((RET_W + ATT_W, D_MODEL), lambda b, i: (0, 0)),
                  pl.BlockSpec((1, D_MODEL), lambda b, i: (0, 0))],
        out_specs=row(D_MODEL),
        compiler_params=pltpu.CompilerParams(
            dimension_semantics=("arbitrary", "arbitrary"), vmem_limit_bytes=VMEM_LIMIT),
        name="att",
    )(aq, ak, av, ck, cv, ag, sink2, ret, x, mod3, w_out_b, fnw)


def _rope_tables(L):
    pos = np.arange(L)
    rows, cols = (pos // GRID_W).astype(np.float64), (pos % GRID_W).astype(np.float64)

    def tables(dh):
        nf = dh // 4
        inv = ROPE_BASE ** (-np.arange(nf, dtype=np.float64) / nf)
        ang = np.concatenate([rows[:, None] * inv, cols[:, None] * inv], axis=-1)
        cos, sin = np.cos(ang), np.sin(ang)
        reps = LANES // dh
        return (jnp.asarray(np.tile(np.concatenate([cos, cos], axis=-1), (1, reps)), F32),
                jnp.asarray(np.tile(np.concatenate([-sin, sin], axis=-1), (1, reps)), F32))

    cr, sr = tables(RET_D)
    ca, sa = tables(ATT_DH)
    return cr, sr, ca, sa


def kernel(x, c, ctx, c_ctx, w_ada, b_ada, w_in, ret_decay_logit, ret_gn_w, att_sink, w_out, final_norm_w):
    B, L, _ = x.shape
    assert w_ada.shape[0] == 1, "single-layer trunk"

    cc = jnp.concatenate([c, c_ctx[None, :], jnp.zeros((B - 1, D_MODEL), F32)], axis=0)
    mod3 = _modulation(cc, w_ada[0], b_ada[0][None, :]).reshape(2 * B, 1, 3 * D_MODEL)

    def pair_heads(t, axis):
        shp = t.shape[:axis] + (2, ATT_HEADS // 2, ATT_DH) + t.shape[axis + 1:]
        return jnp.swapaxes(t.reshape(shp), axis, axis + 1).reshape(t.shape)

    w, wo = w_in[0], w_out[0]
    w_in_b = jnp.concatenate([w[:, :OFF_AQ], pair_heads(w[:, OFF_AQ:OFF_AK], 1), w[:, OFF_AK:OFF_AG],
                              pair_heads(w[:, OFF_AG:], 1)], axis=1).astype(BF16)
    w_out_b = jnp.concatenate([wo[:RET_W], pair_heads(wo[RET_W:], 0)], axis=0).astype(BF16)
    dl = jnp.broadcast_to(ret_decay_logit[0].astype(F32)[:, :, None, None], (2, RET_HEADS, 8, LANES))

    r, aq, ak, av, ag = _project(x, mod3, w_in_b, _rope_tables(L))
    sf, sb, ck, cv = _context(ctx, mod3, w_in_b, dl)
    ret = _retention(r, sf, sb, dl, ret_gn_w[0].reshape(RET_HEADS, 1, RET_D))

    sink2 = jnp.repeat(att_sink[0].astype(F32).reshape(ATT_KV, ATT_HEADS // ATT_KV), LANES, axis=1)
    return _attention_output(aq, ak, av, ck, cv, ag, sink2, ret, x, mod3, w_out_b, final_norm_w[None, :])
```

```python
import numpy as np
import jax
import jax.numpy as jnp
from jax import lax
from jax.experimental import pallas as pl
from jax.experimental.pallas import tpu as pltpu

F32 = jnp.float32
BF16 = jnp.bfloat16

D_MODEL = 1024
GRID_W = 64
RET_HEADS = 4
RET_D = 128
RET_W = RET_HEADS * RET_D
ATT_HEADS = 8
ATT_KV = 2
ATT_DH = 64
ATT_W = ATT_HEADS * ATT_DH
ATT_BLOCK = 128
WINDOW = 128
ROPE_BASE = 10000.0
EPS = 1e-6
NEG = -1e30
LOG2E = 1.4426950408889634
IN_COLS = 4 * RET_W + 2 * ATT_W + 2 * ATT_KV * ATT_DH
OFF_RQ, OFF_RK, OFF_RV, OFF_RG = 0, RET_W, 2 * RET_W, 3 * RET_W
OFF_AQ = 4 * RET_W
OFF_AK = OFF_AQ + ATT_W
OFF_AV = OFF_AK + ATT_KV * ATT_DH
OFF_AG = OFF_AV + ATT_KV * ATT_DH

LANES = 128
VMEM_LIMIT = 48 * 1024 * 1024

PROJ_ROWS = 1024
PROJ_SUB = 256
RET_CHUNK = 256
ATT_ROWS = 1024


def _silu(t):
    return t / (1.0 + jnp.exp(-t))


def _log_sigmoid(t):
    return -(jnp.maximum(-t, 0.0) + jnp.log1p(jnp.exp(-jnp.abs(t))))


def _mod_kernel(c_ref, w_ref, b_ref, o_ref):
    s = _silu(c_ref[...])
    o_ref[...] = jnp.dot(s, w_ref[...], preferred_element_type=F32,
                         precision=lax.Precision.HIGHEST) + b_ref[...]


def _modulation(cc, w_ada, b_ada):
    n = w_ada.shape[1]
    bn = 512
    return pl.pallas_call(
        _mod_kernel,
        out_shape=jax.ShapeDtypeStruct((cc.shape[0], n), F32),
        grid=(n // bn,),
        in_specs=[pl.BlockSpec(cc.shape, lambda i: (0, 0)),
                  pl.BlockSpec((D_MODEL, bn), lambda i: (0, i)),
                  pl.BlockSpec((1, bn), lambda i: (0, i))],
        out_specs=pl.BlockSpec((cc.shape[0], bn), lambda i: (0, i)),
        compiler_params=pltpu.CompilerParams(dimension_semantics=("arbitrary",)),
        name="mod",
    )(cc, w_ada, b_ada)


def _norm_mod(x, m_ref):
    shift = m_ref[0, :, 0:D_MODEL]
    scale = m_ref[0, :, D_MODEL:2 * D_MODEL]
    h = x * lax.rsqrt(jnp.mean(x * x, axis=-1, keepdims=True) + EPS)
    return h * (1.0 + scale) + shift


def _rope_ret(t, cos, sin):
    return t * cos + pltpu.roll(t, 64, 1) * sin


def _rope_att(t, cos, sin, low_half):
    partner = jnp.where(low_half, pltpu.roll(t, 96, 1), pltpu.roll(t, 32, 1))
    return t * cos + partner * sin


def _store_kv_groups(k_ref, v_ref, rows, k, v):
    low = lax.broadcasted_iota(jnp.int32, k.shape, 1) < ATT_DH
    k_ref[0, 0, rows, :] = jnp.where(low, k, 0.0).astype(BF16)
    k_ref[0, 1, rows, :] = jnp.where(low, 0.0, k).astype(BF16)
    v_ref[0, 0, rows, :] = jnp.where(low, v, 1.0).astype(BF16)
    v_ref[0, 1, rows, :] = jnp.where(low, 1.0, v).astype(BF16)


def _proj_kernel(x_ref, m_ref, w_ref, cr_ref, sr_ref, ca_ref, sa_ref,
                 r_ref, rvt_ref, aq_ref, ak_ref, av_ref, ag_ref):
    for t in range(x_ref.shape[1] // PROJ_SUB):
        rows = slice(t * PROJ_SUB, (t + 1) * PROJ_SUB)
        hb = _norm_mod(x_ref[0, rows, :], m_ref).astype(BF16)
        cr, sr, ca, sa = cr_ref[rows, :], sr_ref[rows, :], ca_ref[rows, :], sa_ref[rows, :]
        low_half = (lax.broadcasted_iota(jnp.int32, ca.shape, 1) % ATT_DH) < (ATT_DH // 2)

        def mm(c0, c1):
            return jnp.dot(hb, w_ref[:, c0:c1], preferred_element_type=F32)

        y = mm(OFF_RQ, OFF_RK)
        for h in range(RET_HEADS):
            sl = slice(h * LANES, (h + 1) * LANES)
            r_ref[0, rows, OFF_RQ + h * LANES:OFF_RQ + (h + 1) * LANES] = _rope_ret(y[:, sl], cr, sr).astype(BF16)
        y = mm(OFF_RK, OFF_RV) * (RET_D ** -0.5)
        for h in range(RET_HEADS):
            sl = slice(h * LANES, (h + 1) * LANES)
            r_ref[0, rows, OFF_RK + h * LANES:OFF_RK + (h + 1) * LANES] = _rope_ret(y[:, sl], cr, sr).astype(BF16)
        y = mm(OFF_RV, OFF_RG)
        for h in range(RET_HEADS):
            sl = slice(h * LANES, (h + 1) * LANES)
            rvt_ref[0, sl, rows] = y[:, sl].T.astype(BF16)
        r_ref[0, rows, 2 * RET_W:3 * RET_W] = _silu(mm(OFF_RG, OFF_AQ)).astype(BF16)
        y = mm(OFF_AQ, OFF_AK)
        for j in range(ATT_W // LANES):
            sl = slice(j * LANES, (j + 1) * LANES)
            aq_ref[0, rows, sl] = (_rope_att(y[:, sl], ca, sa, low_half) * (ATT_DH ** -0.5 * LOG2E)).astype(BF16)
        y = mm(OFF_AK, OFF_AG)
        _store_kv_groups(ak_ref, av_ref, rows, _rope_att(y[:, 0:LANES], ca, sa, low_half), y[:, LANES:2 * LANES])
        ag_ref[0, rows, :] = _silu(mm(OFF_AG, IN_COLS)).astype(BF16)


def _project(x, mod3, w_in_b, tabs):
    B, L, _ = x.shape
    tm = PROJ_ROWS
    tab_spec = pl.BlockSpec((tm, LANES), lambda b, i: (i, 0))
    row = lambda n: pl.BlockSpec((1, tm, n), lambda b, i: (b, i, 0))
    kv_spec = pl.BlockSpec((1, ATT_KV, tm, LANES), lambda b, i: (b, 0, i, 0))
    kv_shape = jax.ShapeDtypeStruct((B, ATT_KV, L, LANES), BF16)
    return pl.pallas_call(
        _proj_kernel,
        out_shape=(jax.ShapeDtypeStruct((B, L, 3 * RET_W), BF16),
                   jax.ShapeDtypeStruct((B, RET_W, L), BF16),
                   jax.ShapeDtypeStruct((B, L, ATT_W), BF16),
                   kv_shape, kv_shape,
                   jax.ShapeDtypeStruct((B, L, ATT_W), BF16)),
        grid=(B, L // tm),
        in_specs=[row(D_MODEL),
                  pl.BlockSpec((1, 1, 3 * D_MODEL), lambda b, i: (b, 0, 0)),
                  pl.BlockSpec((D_MODEL, IN_COLS), lambda b, i: (0, 0), pipeline_mode=pl.Buffered(1)),
                  tab_spec, tab_spec, tab_spec, tab_spec],
        out_specs=(row(3 * RET_W), pl.BlockSpec((1, RET_W, tm), lambda b, i: (b, 0, i)),
                   row(ATT_W), kv_spec, kv_spec, row(ATT_W)),
        compiler_params=pltpu.CompilerParams(
            dimension_semantics=("arbitrary", "arbitrary"), vmem_limit_bytes=VMEM_LIMIT),
        name="proj",
    )(x, mod3, w_in_b, *tabs)


def _ctx_kernel(x_ref, m_ref, wk_ref, wv_ref, wa_ref, dl_ref, sf_ref, sb_ref, ck_ref, cv_ref):
    lc = x_ref.shape[1]
    hb = _norm_mod(x_ref[0], m_ref).astype(BF16)
    yk = jnp.dot(hb, wk_ref[...], preferred_element_type=F32)
    yv = jnp.dot(hb, wv_ref[...], preferred_element_type=F32)
    ya = jnp.dot(hb, wa_ref[...], preferred_element_type=F32)
    _store_kv_groups(ck_ref, cv_ref, slice(0, lc), ya[:, 0:LANES], ya[:, LANES:2 * LANES])
    lg = _log_sigmoid(dl_ref[...])
    pos = lax.broadcasted_iota(jnp.int32, (lc, 1), 0).astype(F32)
    for h in range(RET_HEADS):
        k = yk[:, h * RET_D:(h + 1) * RET_D] * (RET_D ** -0.5)
        v = yv[:, h * RET_D:(h + 1) * RET_D].astype(BF16)
        wf = jnp.exp(lg[0, h, 0:1, :] * (lc - 1.0 - pos))
        wb = jnp.exp(lg[1, h, 0:1, :] * pos)
        dn = (((0,), (0,)), ((), ()))
        sf_ref[0, h] = lax.dot_general(v, (k * wf).astype(BF16), dn, preferred_element_type=F32)
        sb_ref[0, h] = lax.dot_general(v, (k * wb).astype(BF16), dn, preferred_element_type=F32)


def _context(ctx, mod3, w_in_b, dl):
    B, lc, _ = ctx.shape
    wcol = lambda off, n: pl.BlockSpec((D_MODEL, n), lambda b: (0, off // n))
    st = jax.ShapeDtypeStruct((B, RET_HEADS, RET_D, RET_D), F32)
    kv = jax.ShapeDtypeStruct((B, ATT_KV, lc, LANES), BF16)
    st_spec = pl.BlockSpec((1, RET_HEADS, RET_D, RET_D), lambda b: (b, 0, 0, 0))
    kv_spec = pl.BlockSpec((1, ATT_KV, lc, LANES), lambda b: (b, 0, 0, 0))
    ctx_row = mod3.shape[0] // 2
    return pl.pallas_call(
        _ctx_kernel,
        out_shape=(st, st, kv, kv),
        grid=(B,),
        in_specs=[pl.BlockSpec((1, lc, D_MODEL), lambda b: (b, 0, 0)),
                  pl.BlockSpec((1, 1, 3 * D_MODEL), lambda b: (ctx_row, 0, 0)),
                  wcol(OFF_RK, RET_W), wcol(OFF_RV, RET_W), wcol(OFF_AK, 2 * LANES),
                  pl.BlockSpec(dl.shape, lambda b: (0, 0, 0, 0))],
        out_specs=(st_spec, st_spec, kv_spec, kv_spec),
        compiler_params=pltpu.CompilerParams(
            dimension_semantics=("arbitrary",), vmem_limit_bytes=VMEM_LIMIT),
        name="ctx",
    )(ctx, mod3, w_in_b, w_in_b, w_in_b, dl)


def _ret_kernel(q_ref, k_ref, vt_ref, g_ref, sf_ref, sb_ref, dl_ref, gn_ref, o_ref,
                uf_scr, rb_scr, kdec_scr, qdec_scr, cdec_scr, decay_scr):
    L = q_ref.shape[1]
    C = RET_CHUNK
    nch = L // C

    @pl.when(pl.program_id(1) == 0)
    def _():
        lg = _log_sigmoid(dl_ref[...])
        lgf, lgb = lg[0, 0, 0:1, :], lg[1, 0, 0:1, :]
        lgf1, lgb1 = lgf[:, 0:1], lgb[:, 0:1]
        pos = lax.broadcasted_iota(jnp.int32, (C, 1), 0).astype(F32)
        kdec_scr[0] = jnp.exp(lgf * (C - 1.0 - pos)).astype(BF16)
        kdec_scr[1] = jnp.exp(lgb * pos).astype(BF16)
        posl = lax.broadcasted_iota(jnp.int32, (8, C), 1).astype(F32)
        qdec_scr[0] = jnp.exp(lgf1 * (posl + 1.0))
        qdec_scr[1] = jnp.exp(lgb1 * (C - posl))
        cdec_scr[0] = jnp.broadcast_to(jnp.exp(lgf * C), (8, LANES))
        cdec_scr[1] = jnp.broadcast_to(jnp.exp(lgb * C), (8, LANES))
        diff = (lax.broadcasted_iota(jnp.int32, (C, C), 1)
                - lax.broadcasted_iota(jnp.int32, (C, C), 0)).astype(F32)
        decay_scr[...] = jnp.where(diff >= 0.0, jnp.exp(lgf1 * jnp.maximum(diff, 0.0)),
                                   jnp.exp(lgb1 * jnp.maximum(-diff, 0.0)))

    kf_dec, kb_dec = kdec_scr[0], kdec_scr[1]
    qf_dec, qb_dec = qdec_scr[0, 0:1, :], qdec_scr[1, 0:1, :]
    cf, cb = cdec_scr[0, 0:1, :], cdec_scr[1, 0:1, :]
    decay_t = decay_scr[...]
    nt = (((1,), (1,)), ((), ()))

    chunk = lambda n: slice(n * C, (n + 1) * C)

    rb = sb_ref[0, 0]
    for n in reversed(range(nch)):
        kc = k_ref[0, chunk(n), :]
        kd = jnp.concatenate([kc * kf_dec, kc * kb_dec], axis=1)
        u = jnp.dot(vt_ref[0, :, chunk(n)], kd, preferred_element_type=F32)
        uf_scr[n] = u[:, 0:RET_D]
        rb_scr[n] = rb.astype(BF16)
        rb = rb * cb + u[:, RET_D:2 * RET_D]

    gn = gn_ref[0]

    def kq(n):
        return lax.dot_general(k_ref[0, chunk(n), :], q_ref[0, chunk(n), :], nt, preferred_element_type=F32)

    rf = sf_ref[0, 0]
    s_next = kq(0)
    for n in range(nch):
        s = s_next
        states = jnp.concatenate([rf.astype(BF16), rb_scr[n]], axis=0)
        cross = lax.dot_general(states, q_ref[0, chunk(n), :], nt, preferred_element_type=F32)
        if n + 1 < nch:
            s_next = kq(n + 1)
            rf = rf * cf + uf_scr[n]
        inner = jnp.dot(vt_ref[0, :, chunk(n)], (s * decay_t).astype(BF16), preferred_element_type=F32)
        y = inner + cross[0:RET_D] * qf_dec + cross[RET_D:2 * RET_D] * qb_dec
        mu = jnp.mean(y, axis=0, keepdims=True)
        yc = y - mu
        var = jnp.mean(yc * yc, axis=0, keepdims=True)
        yn = (yc * lax.rsqrt(var + EPS)).T * gn
        o_ref[0, chunk(n), :] = (yn * g_ref[0, chunk(n), :].astype(F32)).astype(BF16)


def _retention(r, rvt, sf, sb, dl, gn):
    B, L, _ = r.shape
    col = lambda off: pl.BlockSpec((1, L, RET_D), lambda h, b: (b, 0, off + h))
    st_spec = pl.BlockSpec((1, 1, RET_D, RET_D), lambda h, b: (b, h, 0, 0))
    C = RET_CHUNK
    return pl.pallas_call(
        _ret_kernel,
        out_shape=jax.ShapeDtypeStruct((B, L, RET_W), BF16),
        grid=(RET_HEADS, B),
        in_specs=[col(0), col(RET_HEADS),
                  pl.BlockSpec((1, RET_D, L), lambda h, b: (b, h, 0)),
                  col(2 * RET_HEADS),
                  st_spec, st_spec,
                  pl.BlockSpec((2, 1, 8, LANES), lambda h, b: (0, h, 0, 0)),
                  pl.BlockSpec((1, 1, RET_D), lambda h, b: (h, 0, 0))],
        out_specs=pl.BlockSpec((1, L, RET_D), lambda h, b: (b, 0, h)),
        scratch_shapes=[pltpu.VMEM((L // C, RET_D, RET_D), F32),
                        pltpu.VMEM((L // C, RET_D, RET_D), BF16),
                        pltpu.VMEM((2, C, LANES), BF16),
                        pltpu.VMEM((2, 8, C), F32),
                        pltpu.VMEM((2, 8, LANES), F32),
                        pltpu.VMEM((C, C), F32)],
        compiler_params=pltpu.CompilerParams(
            dimension_semantics=("arbitrary", "arbitrary"), vmem_limit_bytes=VMEM_LIMIT),
        name="ret",
    )(r, r, rvt, r, sf, sb, dl, gn)


def _att_kernel(q_ref, k_ref, v_ref, ck_ref, cv_ref, g_ref, sink_ref, r_ref, x_ref, m_ref, w_ref, fn_ref,
                o_ref):
    L = k_ref.shape[2]
    nb_total = L // ATT_BLOCK
    nb_step = q_ref.shape[1] // ATT_BLOCK
    nhb = ATT_W // LANES
    i = pl.program_id(1)
    r_i = lax.broadcasted_iota(jnp.int32, (ATT_BLOCK, ATT_BLOCK), 0)
    q_i = lax.broadcasted_iota(jnp.int32, (ATT_BLOCK, ATT_BLOCK), 1)
    band_prev = jnp.where(q_i <= r_i, 0.0, NEG).astype(F32)
    band_next = jnp.where(r_i <= q_i, 0.0, NEG).astype(F32)
    nt = (((1,), (1,)), ((), ()))
    tn = (((0,), (0,)), ((), ()))

    def key_rows(jb):
        n = i * nb_step + jb
        blk = lambda t: pl.ds(pl.multiple_of(t * ATT_BLOCK, ATT_BLOCK), ATT_BLOCK)
        return n, (blk(jnp.maximum(n - 1, 0)), blk(n), blk(jnp.minimum(n + 1, nb_total - 1)))

    def scores(jb, kv):
        n, (p_rows, o_rows, n_rows) = key_rows(jb)
        q = q_ref[0, jb * ATT_BLOCK:(jb + 1) * ATT_BLOCK, :]
        qall = jnp.concatenate([q[:, j * LANES:(j + 1) * LANES] for j in range(nhb)], axis=0)
        kall = jnp.concatenate([ck_ref[0, kv], k_ref[0, kv, p_rows, :], k_ref[0, kv, o_rows, :],
                                k_ref[0, kv, n_rows, :]], axis=0)
        return lax.dot_general(kall, qall, nt, preferred_element_type=F32)

    def attend(jb, kv, s):
        n, (p_rows, o_rows, n_rows) = key_rows(jb)
        lc = ck_ref.shape[2]
        bias_prev = jnp.concatenate([band_prev + jnp.where(n == 0, NEG, 0.0)] * nhb, axis=1)
        bias_next = jnp.concatenate([band_next + jnp.where(n == nb_total - 1, NEG, 0.0)] * nhb, axis=1)
        parts = [s[0:lc], s[lc:lc + ATT_BLOCK] + bias_prev, s[lc + ATT_BLOCK:lc + 2 * ATT_BLOCK],
                 s[lc + 2 * ATT_BLOCK:] + bias_next]
        sink = sink_ref[kv:kv + 1, :] * LOG2E
        m = sink
        for t in parts:
            m = jnp.maximum(m, jnp.max(t, axis=0, keepdims=True))
        p = jnp.concatenate([jnp.exp2(t - m).astype(BF16) for t in parts], axis=0)
        vaug = jnp.concatenate([cv_ref[0, kv], v_ref[0, kv, p_rows, :], v_ref[0, kv, o_rows, :],
                                v_ref[0, kv, n_rows, :]], axis=0)
        o = lax.dot_general(vaug, p, tn, preferred_element_type=F32)
        if kv == 0:
            val, ones = slice(0, ATT_DH), slice(ATT_DH, ATT_DH + 1)
        else:
            val, ones = slice(ATT_DH, 2 * ATT_DH), slice(0, 1)
        den = o[ones] + jnp.exp2(sink - m)
        return o[val] * (1.0 / den)

    def gated_block(jb, halves):
        rows = slice(jb * ATT_BLOCK, (jb + 1) * ATT_BLOCK)
        comb = jnp.concatenate(halves, axis=0)
        return jnp.concatenate(
            [(comb[:, j * LANES:(j + 1) * LANES].T
              * g_ref[0, rows, j * LANES:(j + 1) * LANES].astype(F32)).astype(BF16) for j in range(nhb)], axis=1)

    def finish(jb, att_blk):
        rows = slice(jb * ATT_BLOCK, (jb + 1) * ATT_BLOCK)
        mix_in = jnp.concatenate([r_ref[0, rows, :], att_blk], axis=1)
        mixed = jnp.dot(mix_in, w_ref[...], preferred_element_type=F32)
        xn = x_ref[0, rows, :] + m_ref[0, :, 2 * D_MODEL:3 * D_MODEL] * mixed
        o_ref[0, rows, :] = xn * lax.rsqrt(jnp.mean(xn * xn, axis=-1, keepdims=True) + EPS) * fn_ref[...]

    chains = [(jb, kv) for jb in range(nb_step) for kv in range(ATT_KV)]
    s_next = scores(*chains[0])
    halves, pending = [], []
    for c, (jb, kv) in enumerate(chains):
        s = s_next
        if c + 1 < len(chains):
            s_next = scores(*chains[c + 1])
        halves.append(attend(jb, kv, s))
        if pending:
            finish(*pending.pop())
        if kv == ATT_KV - 1:
            pending.append((jb, gated_block(jb, halves)))
            halves = []
    finish(*pending.pop())


def _attention_output(aq, ak, av, ck, cv, ag, sink2, ret, x, mod3, w_out_b, fnw):
    B, L, _ = aq.shape
    lc = ck.shape[2]
    tq = ATT_ROWS
    row = lambda n: pl.BlockSpec((1, tq, n), lambda b, i: (b, i, 0))
    full = lambda n: pl.BlockSpec((1, ATT_KV, n, LANES), lambda b, i: (b, 0, 0, 0))
    return pl.pallas_call(
        _att_kernel,
        out_shape=jax.ShapeDtypeStruct((B, L, D_MODEL), F32),
        grid=(B, L // tq),
        in_specs=[row(ATT_W), full(L), full(L), full(lc), full(lc), row(ATT_W),
                  pl.BlockSpec(sink2.shape, lambda b, i: (0, 0)),
                  row(RET_W), row(D_MODEL),
                  pl.BlockSpec((1, 1, 3 * D_MODEL), lambda b, i: (b, 0, 0)),
                  pl.BlockSpec((RET_W + ATT_W, D_MODEL), lambda b, i: (0, 0)),
                  pl.BlockSpec((1, D_MODEL), lambda b, i: (0, 0))],
        out_specs=row(D_MODEL),
        compiler_params=pltpu.CompilerParams(
            dimension_semantics=("arbitrary", "arbitrary"), vmem_limit_bytes=VMEM_LIMIT),
        name="att",
    )(aq, ak, av, ck, cv, ag, sink2, ret, x, mod3, w_out_b, fnw)


def _rope_tables(L):
    pos = np.arange(L)
    rows, cols = (pos // GRID_W).astype(np.float64), (pos % GRID_W).astype(np.float64)

    def tables(dh):
        nf = dh // 4
        inv = ROPE_BASE ** (-np.arange(nf, dtype=np.float64) / nf)
        ang = np.concatenate([rows[:, None] * inv, cols[:, None] * inv], axis=-1)
        cos, sin = np.cos(ang), np.sin(ang)
        reps = LANES // dh
        return (jnp.asarray(np.tile(np.concatenate([cos, cos], axis=-1), (1, reps)), F32),
                jnp.asarray(np.tile(np.concatenate([-sin, sin], axis=-1), (1, reps)), F32))

    cr, sr = tables(RET_D)
    ca, sa = tables(ATT_DH)
    return cr, sr, ca, sa


def kernel(x, c, ctx, c_ctx, w_ada, b_ada, w_in, ret_decay_logit, ret_gn_w, att_sink, w_out, final_norm_w):
    B, L, _ = x.shape
    assert w_ada.shape[0] == 1, "single-layer trunk"

    cc = jnp.concatenate([c, c_ctx[None, :], jnp.zeros((B - 1, D_MODEL), F32)], axis=0)
    mod3 = _modulation(cc, w_ada[0], b_ada[0][None, :]).reshape(2 * B, 1, 3 * D_MODEL)

    def pair_heads(t, axis):
        shp = t.shape[:axis] + (2, ATT_HEADS // 2, ATT_DH) + t.shape[axis + 1:]
        return jnp.swapaxes(t.reshape(shp), axis, axis + 1).reshape(t.shape)

    w, wo = w_in[0], w_out[0]
    w_in_b = jnp.concatenate([w[:, :OFF_AQ], pair_heads(w[:, OFF_AQ:OFF_AK], 1), w[:, OFF_AK:OFF_AG],
                              pair_heads(w[:, OFF_AG:], 1)], axis=1).astype(BF16)
    w_out_b = jnp.concatenate([wo[:RET_W], pair_heads(wo[RET_W:], 0)], axis=0).astype(BF16)
    dl = jnp.broadcast_to(ret_decay_logit[0].astype(F32)[:, :, None, None], (2, RET_HEADS, 8, LANES))

    r, rvt, aq, ak, av, ag = _project(x, mod3, w_in_b, _rope_tables(L))
    sf, sb, ck, cv = _context(ctx, mod3, w_in_b, dl)
    ret = _retention(r, rvt, sf, sb, dl, ret_gn_w[0].reshape(RET_HEADS, 1, RET_D))

    sink2 = jnp.repeat(att_sink[0].astype(F32).reshape(ATT_KV, ATT_HEADS // ATT_KV), LANES, axis=1)
    return _attention_output(aq, ak, av, ck, cv, ag, sink2, ret, x, mod3, w_out_b, final_norm_w[None, :])
```

```python
import numpy as np
import jax
import jax.numpy as jnp
from jax import lax
from jax.experimental import pallas as pl
from jax.experimental.pallas import tpu as pltpu

F32 = jnp.float32
BF16 = jnp.bfloat16

D_MODEL = 1024
GRID_W = 64
RET_HEADS = 4
RET_D = 128
RET_W = RET_HEADS * RET_D
ATT_HEADS = 8
ATT_KV = 2
ATT_DH = 64
ATT_W = ATT_HEADS * ATT_DH
ATT_BLOCK = 128
WINDOW = 128
ROPE_BASE = 10000.0
EPS = 1e-6
NEG = -1e30
LOG2E = 1.4426950408889634
IN_COLS = 4 * RET_W + 2 * ATT_W + 2 * ATT_KV * ATT_DH
OFF_RQ, OFF_RK, OFF_RV, OFF_RG = 0, RET_W, 2 * RET_W, 3 * RET_W
OFF_AQ = 4 * RET_W
OFF_AK = OFF_AQ + ATT_W
OFF_AV = OFF_AK + ATT_KV * ATT_DH
OFF_AG = OFF_AV + ATT_KV * ATT_DH

LANES = 128
VMEM_LIMIT = 48 * 1024 * 1024

PROJ_ROWS = 1024
PROJ_SUB = 256
RET_CHUNK = 256
RET_PAIR = 2
ATT_ROWS = 1024


def _silu(t):
    return t / (1.0 + jnp.exp(-t))


def _log_sigmoid(t):
    return -(jnp.maximum(-t, 0.0) + jnp.log1p(jnp.exp(-jnp.abs(t))))


def _mod_kernel(c_ref, w_ref, b_ref, o_ref):
    s = _silu(c_ref[...])
    o_ref[...] = jnp.dot(s, w_ref[...], preferred_element_type=F32,
                         precision=lax.Precision.HIGHEST) + b_ref[...]


def _modulation(cc, w_ada, b_ada):
    n = w_ada.shape[1]
    bn = 512
    return pl.pallas_call(
        _mod_kernel,
        out_shape=jax.ShapeDtypeStruct((cc.shape[0], n), F32),
        grid=(n // bn,),
        in_specs=[pl.BlockSpec(cc.shape, lambda i: (0, 0)),
                  pl.BlockSpec((D_MODEL, bn), lambda i: (0, i)),
                  pl.BlockSpec((1, bn), lambda i: (0, i))],
        out_specs=pl.BlockSpec((cc.shape[0], bn), lambda i: (0, i)),
        compiler_params=pltpu.CompilerParams(dimension_semantics=("arbitrary",)),
        name="mod",
    )(cc, w_ada, b_ada)


def _norm_mod(x, m_ref):
    shift = m_ref[0, :, 0:D_MODEL]
    scale = m_ref[0, :, D_MODEL:2 * D_MODEL]
    h = x * lax.rsqrt(jnp.mean(x * x, axis=-1, keepdims=True) + EPS)
    return h * (1.0 + scale) + shift


def _rope_ret(t, cos, sin):
    return t * cos + pltpu.roll(t, 64, 1) * sin


def _rope_att(t, cos, sin, low_half):
    partner = jnp.where(low_half, pltpu.roll(t, 96, 1), pltpu.roll(t, 32, 1))
    return t * cos + partner * sin


def _store_kv_groups(k_ref, v_ref, rows, k, v):
    low = lax.broadcasted_iota(jnp.int32, k.shape, 1) < ATT_DH
    k_ref[0, 0, rows, :] = jnp.where(low, k, 0.0).astype(BF16)
    k_ref[0, 1, rows, :] = jnp.where(low, 0.0, k).astype(BF16)
    v_ref[0, 0, rows, :] = jnp.where(low, v, 1.0).astype(BF16)
    v_ref[0, 1, rows, :] = jnp.where(low, 1.0, v).astype(BF16)


def _proj_kernel(x_ref, m_ref, w_ref, cr_ref, sr_ref, ca_ref, sa_ref,
                 r_ref, rvt_ref, aq_ref, ak_ref, av_ref, ag_ref):
    for t in range(x_ref.shape[1] // PROJ_SUB):
        rows = slice(t * PROJ_SUB, (t + 1) * PROJ_SUB)
        hb = _norm_mod(x_ref[0, rows, :], m_ref).astype(BF16)
        cr, sr, ca, sa = cr_ref[rows, :], sr_ref[rows, :], ca_ref[rows, :], sa_ref[rows, :]
        low_half = (lax.broadcasted_iota(jnp.int32, ca.shape, 1) % ATT_DH) < (ATT_DH // 2)

        def mm(c0, c1):
            return jnp.dot(hb, w_ref[:, c0:c1], preferred_element_type=F32)

        y = mm(OFF_RQ, OFF_RK)
        for h in range(RET_HEADS):
            sl = slice(h * LANES, (h + 1) * LANES)
            r_ref[0, rows, OFF_RQ + h * LANES:OFF_RQ + (h + 1) * LANES] = _rope_ret(y[:, sl], cr, sr).astype(BF16)
        y = mm(OFF_RK, OFF_RV) * (RET_D ** -0.5)
        for h in range(RET_HEADS):
            sl = slice(h * LANES, (h + 1) * LANES)
            r_ref[0, rows, OFF_RK + h * LANES:OFF_RK + (h + 1) * LANES] = _rope_ret(y[:, sl], cr, sr).astype(BF16)
        y = mm(OFF_RV, OFF_RG)
        for h in range(RET_HEADS):
            sl = slice(h * LANES, (h + 1) * LANES)
            rvt_ref[0, sl, rows] = y[:, sl].T.astype(BF16)
        r_ref[0, rows, 2 * RET_W:3 * RET_W] = _silu(mm(OFF_RG, OFF_AQ)).astype(BF16)
        y = mm(OFF_AQ, OFF_AK)
        for j in range(ATT_W // LANES):
            sl = slice(j * LANES, (j + 1) * LANES)
            aq_ref[0, rows, sl] = (_rope_att(y[:, sl], ca, sa, low_half) * (ATT_DH ** -0.5 * LOG2E)).astype(BF16)
        y = mm(OFF_AK, OFF_AG)
        _store_kv_groups(ak_ref, av_ref, rows, _rope_att(y[:, 0:LANES], ca, sa, low_half), y[:, LANES:2 * LANES])
        ag_ref[0, rows, :] = _silu(mm(OFF_AG, IN_COLS)).astype(BF16)


def _project(x, mod3, w_in_b, tabs):
    B, L, _ = x.shape
    tm = PROJ_ROWS
    tab_spec = pl.BlockSpec((tm, LANES), lambda b, i: (i, 0))
    row = lambda n: pl.BlockSpec((1, tm, n), lambda b, i: (b, i, 0))
    kv_spec = pl.BlockSpec((1, ATT_KV, tm, LANES), lambda b, i: (b, 0, i, 0))
    kv_shape = jax.ShapeDtypeStruct((B, ATT_KV, L, LANES), BF16)
    return pl.pallas_call(
        _proj_kernel,
        out_shape=(jax.ShapeDtypeStruct((B, L, 3 * RET_W), BF16),
                   jax.ShapeDtypeStruct((B, RET_W, L), BF16),
                   jax.ShapeDtypeStruct((B, L, ATT_W), BF16),
                   kv_shape, kv_shape,
                   jax.ShapeDtypeStruct((B, L, ATT_W), BF16)),
        grid=(B, L // tm),
        in_specs=[row(D_MODEL),
                  pl.BlockSpec((1, 1, 3 * D_MODEL), lambda b, i: (b, 0, 0)),
                  pl.BlockSpec((D_MODEL, IN_COLS), lambda b, i: (0, 0), pipeline_mode=pl.Buffered(1)),
                  tab_spec, tab_spec, tab_spec, tab_spec],
        out_specs=(row(3 * RET_W), pl.BlockSpec((1, RET_W, tm), lambda b, i: (b, 0, i)),
                   row(ATT_W), kv_spec, kv_spec, row(ATT_W)),
        compiler_params=pltpu.CompilerParams(
            dimension_semantics=("arbitrary", "arbitrary"), vmem_limit_bytes=VMEM_LIMIT),
        name="proj",
    )(x, mod3, w_in_b, *tabs)


def _ctx_kernel(x_ref, m_ref, wk_ref, wv_ref, wa_ref, dl_ref, sf_ref, sb_ref, ck_ref, cv_ref):
    lc = x_ref.shape[1]
    hb = _norm_mod(x_ref[0], m_ref).astype(BF16)
    yk = jnp.dot(hb, wk_ref[...], preferred_element_type=F32)
    yv = jnp.dot(hb, wv_ref[...], preferred_element_type=F32)
    ya = jnp.dot(hb, wa_ref[...], preferred_element_type=F32)
    _store_kv_groups(ck_ref, cv_ref, slice(0, lc), ya[:, 0:LANES], ya[:, LANES:2 * LANES])
    lg = _log_sigmoid(dl_ref[...])
    pos = lax.broadcasted_iota(jnp.int32, (lc, 1), 0).astype(F32)
    for h in range(RET_HEADS):
        k = yk[:, h * RET_D:(h + 1) * RET_D] * (RET_D ** -0.5)
        v = yv[:, h * RET_D:(h + 1) * RET_D].astype(BF16)
        wf = jnp.exp(lg[0, h, 0:1, :] * (lc - 1.0 - pos))
        wb = jnp.exp(lg[1, h, 0:1, :] * pos)
        dn = (((0,), (0,)), ((), ()))
        sf_ref[0, h] = lax.dot_general(v, (k * wf).astype(BF16), dn, preferred_element_type=F32)
        sb_ref[0, h] = lax.dot_general(v, (k * wb).astype(BF16), dn, preferred_element_type=F32)


def _context(ctx, mod3, w_in_b, dl):
    B, lc, _ = ctx.shape
    wcol = lambda off, n: pl.BlockSpec((D_MODEL, n), lambda b: (0, off // n))
    st = jax.ShapeDtypeStruct((B, RET_HEADS, RET_D, RET_D), F32)
    kv = jax.ShapeDtypeStruct((B, ATT_KV, lc, LANES), BF16)
    st_spec = pl.BlockSpec((1, RET_HEADS, RET_D, RET_D), lambda b: (b, 0, 0, 0))
    kv_spec = pl.BlockSpec((1, ATT_KV, lc, LANES), lambda b: (b, 0, 0, 0))
    ctx_row = mod3.shape[0] // 2
    return pl.pallas_call(
        _ctx_kernel,
        out_shape=(st, st, kv, kv),
        grid=(B,),
        in_specs=[pl.BlockSpec((1, lc, D_MODEL), lambda b: (b, 0, 0)),
                  pl.BlockSpec((1, 1, 3 * D_MODEL), lambda b: (ctx_row, 0, 0)),
                  wcol(OFF_RK, RET_W), wcol(OFF_RV, RET_W), wcol(OFF_AK, 2 * LANES),
                  pl.BlockSpec(dl.shape, lambda b: (0, 0, 0, 0))],
        out_specs=(st_spec, st_spec, kv_spec, kv_spec),
        compiler_params=pltpu.CompilerParams(
            dimension_semantics=("arbitrary",), vmem_limit_bytes=VMEM_LIMIT),
        name="ctx",
    )(ctx, mod3, w_in_b, w_in_b, w_in_b, dl)


def _ret_kernel(q_ref, k_ref, vt_ref, g_ref, sf_ref, sb_ref, dl_ref, gn_ref, o_ref,
                uf_scr, rb_scr, kdec_scr, qdec_scr, cdec_scr, decay_scr):
    L = q_ref.shape[1]
    C = RET_CHUNK
    nch = L // C
    heads = range(RET_PAIR)
    lanes = lambda h: slice(h * RET_D, (h + 1) * RET_D)

    @pl.when(pl.program_id(1) == 0)
    def _():
        lg = _log_sigmoid(dl_ref[...])
        pos = lax.broadcasted_iota(jnp.int32, (C, 1), 0).astype(F32)
        posl = lax.broadcasted_iota(jnp.int32, (8, C), 1).astype(F32)
        diff = (lax.broadcasted_iota(jnp.int32, (C, C), 1)
                - lax.broadcasted_iota(jnp.int32, (C, C), 0)).astype(F32)
        for h in heads:
            lgf, lgb = lg[0, h, 0:1, :], lg[1, h, 0:1, :]
            lgf1, lgb1 = lgf[:, 0:1], lgb[:, 0:1]
            kdec_scr[h, 0] = jnp.exp(lgf * (C - 1.0 - pos))
            kdec_scr[h, 1] = jnp.exp(lgb * pos)
            qdec_scr[h, 0] = jnp.exp(lgf1 * (posl + 1.0))
            qdec_scr[h, 1] = jnp.exp(lgb1 * (C - posl))
            cdec_scr[h, 0] = jnp.broadcast_to(jnp.exp(lgf * C), (8, LANES))
            cdec_scr[h, 1] = jnp.broadcast_to(jnp.exp(lgb * C), (8, LANES))
            decay_scr[h] = jnp.where(diff >= 0.0, jnp.exp(lgf1 * jnp.maximum(diff, 0.0)),
                                     jnp.exp(lgb1 * jnp.maximum(-diff, 0.0)))

    nt = (((1,), (1,)), ((), ()))
    chunk = lambda n: slice(n * C, (n + 1) * C)

    rb = [sb_ref[0, h] for h in heads]
    for n in reversed(range(nch)):
        for h in heads:
            kc = k_ref[0, chunk(n), lanes(h)].astype(F32)
            kd = jnp.concatenate([(kc * kdec_scr[h, 0]).astype(BF16), (kc * kdec_scr[h, 1]).astype(BF16)],
                                 axis=1)
            u = jnp.dot(vt_ref[0, lanes(h), chunk(n)], kd, preferred_element_type=F32)
            uf_scr[h, n] = u[:, 0:RET_D]
            rb_scr[h, n] = rb[h].astype(BF16)
            rb[h] = rb[h] * cdec_scr[h, 1, 0:1, :] + u[:, RET_D:2 * RET_D]

    def kq(h, n):
        return lax.dot_general(k_ref[0, chunk(n), lanes(h)], q_ref[0, chunk(n), lanes(h)], nt,
                               preferred_element_type=F32)

    rf = [sf_ref[0, h] for h in heads]
    s_next = [kq(h, 0) for h in heads]
    for n in range(nch):
        for h in heads:
            s = s_next[h]
            q = q_ref[0, chunk(n), lanes(h)]
            states = jnp.concatenate([rf[h].astype(BF16), rb_scr[h, n]], axis=0)
            cross = lax.dot_general(states, q, nt, preferred_element_type=F32)
            if n + 1 < nch:
                s_next[h] = kq(h, n + 1)
                rf[h] = rf[h] * cdec_scr[h, 0, 0:1, :] + uf_scr[h, n]
            inner = jnp.dot(vt_ref[0, lanes(h), chunk(n)], (s * decay_scr[h]).astype(BF16),
                            preferred_element_type=F32)
            y = (inner + cross[0:RET_D] * qdec_scr[h, 0, 0:1, :]
                 + cross[RET_D:2 * RET_D] * qdec_scr[h, 1, 0:1, :])
            mu = jnp.mean(y, axis=0, keepdims=True)
            yc = y - mu
            var = jnp.mean(yc * yc, axis=0, keepdims=True)
            yn = (yc * lax.rsqrt(var + EPS)).T * gn_ref[h]
            o_ref[0, chunk(n), lanes(h)] = (yn * g_ref[0, chunk(n), lanes(h)].astype(F32)).astype(BF16)


def _retention(r, rvt, sf, sb, dl, gn):
    B, L, _ = r.shape
    P, C = RET_PAIR, RET_CHUNK
    col = lambda off: pl.BlockSpec((1, L, P * RET_D), lambda hp, b: (b, 0, off + hp))
    st_spec = pl.BlockSpec((1, P, RET_D, RET_D), lambda hp, b: (b, hp, 0, 0))
    npair = RET_HEADS // P
    return pl.pallas_call(
        _ret_kernel,
        out_shape=jax.ShapeDtypeStruct((B, L, RET_W), BF16),
        grid=(npair, B),
        in_specs=[col(0), col(npair),
                  pl.BlockSpec((1, P * RET_D, L), lambda hp, b: (b, hp, 0)),
                  col(2 * npair),
                  st_spec, st_spec,
                  pl.BlockSpec((2, P, 8, LANES), lambda hp, b: (0, hp, 0, 0)),
                  pl.BlockSpec((P, 1, RET_D), lambda hp, b: (hp, 0, 0))],
        out_specs=pl.BlockSpec((1, L, P * RET_D), lambda hp, b: (b, 0, hp)),
        scratch_shapes=[pltpu.VMEM((P, L // C, RET_D, RET_D), F32),
                        pltpu.VMEM((P, L // C, RET_D, RET_D), BF16),
                        pltpu.VMEM((P, 2, C, LANES), F32),
                        pltpu.VMEM((P, 2, 8, C), F32),
                        pltpu.VMEM((P, 2, 8, LANES), F32),
                        pltpu.VMEM((P, C, C), F32)],
        compiler_params=pltpu.CompilerParams(
            dimension_semantics=("arbitrary", "arbitrary"), vmem_limit_bytes=VMEM_LIMIT),
        name="ret",
    )(r, r, rvt, r, sf, sb, dl, gn)


def _att_kernel(q_ref, k_ref, v_ref, ck_ref, cv_ref, g_ref, sink_ref, r_ref, x_ref, m_ref, w_ref, fn_ref,
                o_ref):
    L = k_ref.shape[2]
    nb_total = L // ATT_BLOCK
    nb_step = q_ref.shape[1] // ATT_BLOCK
    nhb = ATT_W // LANES
    i = pl.program_id(1)
    r_i = lax.broadcasted_iota(jnp.int32, (ATT_BLOCK, ATT_BLOCK), 0)
    q_i = lax.broadcasted_iota(jnp.int32, (ATT_BLOCK, ATT_BLOCK), 1)
    band_prev = jnp.where(q_i <= r_i, 0.0, NEG).astype(F32)
    band_next = jnp.where(r_i <= q_i, 0.0, NEG).astype(F32)
    nt = (((1,), (1,)), ((), ()))
    tn = (((0,), (0,)), ((), ()))

    def key_rows(jb):
        n = i * nb_step + jb
        blk = lambda t: pl.ds(pl.multiple_of(t * ATT_BLOCK, ATT_BLOCK), ATT_BLOCK)
        return n, (blk(jnp.maximum(n - 1, 0)), blk(n), blk(jnp.minimum(n + 1, nb_total - 1)))

    def scores(jb, kv):
        n, (p_rows, o_rows, n_rows) = key_rows(jb)
        q = q_ref[0, jb * ATT_BLOCK:(jb + 1) * ATT_BLOCK, :]
        qall = jnp.concatenate([q[:, j * LANES:(j + 1) * LANES] for j in range(nhb)], axis=0)
        kall = jnp.concatenate([ck_ref[0, kv], k_ref[0, kv, p_rows, :], k_ref[0, kv, o_rows, :],
                                k_ref[0, kv, n_rows, :]], axis=0)
        return lax.dot_general(kall, qall, nt, preferred_element_type=F32)

    def attend(jb, kv, s):
        n, (p_rows, o_rows, n_rows) = key_rows(jb)
        lc = ck_ref.shape[2]
        bias_prev = jnp.concatenate([band_prev + jnp.where(n == 0, NEG, 0.0)] * nhb, axis=1)
        bias_next = jnp.concatenate([band_next + jnp.where(n == nb_total - 1, NEG, 0.0)] * nhb, axis=1)
        parts = [s[0:lc], s[lc:lc + ATT_BLOCK] + bias_prev, s[lc + ATT_BLOCK:lc + 2 * ATT_BLOCK],
                 s[lc + 2 * ATT_BLOCK:] + bias_next]
        sink = sink_ref[kv:kv + 1, :] * LOG2E
        m = sink
        for t in parts:
            m = jnp.maximum(m, jnp.max(t, axis=0, keepdims=True))
        p = jnp.concatenate([jnp.exp2(t - m).astype(BF16) for t in parts], axis=0)
        vaug = jnp.concatenate([cv_ref[0, kv], v_ref[0, kv, p_rows, :], v_ref[0, kv, o_rows, :],
                                v_ref[0, kv, n_rows, :]], axis=0)
        o = lax.dot_general(vaug, p, tn, preferred_element_type=F32)
        if kv == 0:
            val, ones = slice(0, ATT_DH), slice(ATT_DH, ATT_DH + 1)
        else:
            val, ones = slice(ATT_DH, 2 * ATT_DH), slice(0, 1)
        den = o[ones] + jnp.exp2(sink - m)
        return o[val] * (1.0 / den)

    def gated_block(jb, halves):
        rows = slice(jb * ATT_BLOCK, (jb + 1) * ATT_BLOCK)
        comb = jnp.concatenate(halves, axis=0)
        return jnp.concatenate(
            [(comb[:, j * LANES:(j + 1) * LANES].T
              * g_ref[0, rows, j * LANES:(j + 1) * LANES].astype(F32)).astype(BF16) for j in range(nhb)], axis=1)

    def finish(jb, att_blk):
        rows = slice(jb * ATT_BLOCK, (jb + 1) * ATT_BLOCK)
        mix_in = jnp.concatenate([r_ref[0, rows, :], att_blk], axis=1)
        mixed = jnp.dot(mix_in, w_ref[...], preferred_element_type=F32)
        xn = x_ref[0, rows, :] + m_ref[0, :, 2 * D_MODEL:3 * D_MODEL] * mixed
        o_ref[0, rows, :] = xn * lax.rsqrt(jnp.mean(xn * xn, axis=-1, keepdims=True) + EPS) * fn_ref[...]

    chains = [(jb, kv) for jb in range(nb_step) for kv in range(ATT_KV)]
    s_next = scores(*chains[0])
    halves, pending = [], []
    for c, (jb, kv) in enumerate(chains):
        s = s_next
        if c + 1 < len(chains):
            s_next = scores(*chains[c + 1])
        halves.append(attend(jb, kv, s))
        if pending:
            finish(*pending.pop())
        if kv == ATT_KV - 1:
            pending.append((jb, gated_block(jb, halves)))
            halves = []
    finish(*pending.pop())


def _attention_output(aq, ak, av, ck, cv, ag, sink2, ret, x, mod3, w_out_b, fnw):
    B, L, _ = aq.shape
    lc = ck.shape[2]
    tq = ATT_ROWS
    row = lambda n: pl.BlockSpec((1, tq, n), lambda b, i: (b, i, 0))
    full = lambda n: pl.BlockSpec((1, ATT_KV, n, LANES), lambda b, i: (b, 0, 0, 0))
    return pl.pallas_call(
        _att_kernel,
        out_shape=jax.ShapeDtypeStruct((B, L, D_MODEL), F32),
        grid=(B, L // tq),
        in_specs=[row(ATT_W), full(L), full(L), full(lc), full(lc), row(ATT_W),
                  pl.BlockSpec(sink2.shape, lambda b, i: (0, 0)),
                  row(RET_W), row(D_MODEL),
                  pl.BlockSpec((1, 1, 3 * D_MODEL), lambda b, i: (b, 0, 0)),
                  pl.BlockSpec((RET_W + ATT_W, D_MODEL), lambda b, i: (0, 0)),
                  pl.BlockSpec((1, D_MODEL), lambda b, i: (0, 0))],
        out_specs=row(D_MODEL),
        compiler_params=pltpu.CompilerParams(
            dimension_semantics=("arbitrary", "arbitrary"), vmem_limit_bytes=VMEM_LIMIT),
        name="att",
    )(aq, ak, av, ck, cv, ag, sink2, ret, x, mod3, w_out_b, fnw)


def _rope_tables(L):
    pos = np.arange(L)
    rows, cols = (pos // GRID_W).astype(np.float64), (pos % GRID_W).astype(np.float64)

    def tables(dh):
        nf = dh // 4
        inv = ROPE_BASE ** (-np.arange(nf, dtype=np.float64) / nf)
        ang = np.concatenate([rows[:, None] * inv, cols[:, None] * inv], axis=-1)
        cos, sin = np.cos(ang), np.sin(ang)
        reps = LANES // dh
        return (jnp.asarray(np.tile(np.concatenate([cos, cos], axis=-1), (1, reps)), F32),
                jnp.asarray(np.tile(np.concatenate([-sin, sin], axis=-1), (1, reps)), F32))

    cr, sr = tables(RET_D)
    ca, sa = tables(ATT_DH)
    return cr, sr, ca, sa


def kernel(x, c, ctx, c_ctx, w_ada, b_ada, w_in, ret_decay_logit, ret_gn_w, att_sink, w_out, final_norm_w):
    B, L, _ = x.shape
    assert w_ada.shape[0] == 1, "single-layer trunk"

    cc = jnp.concatenate([c, c_ctx[None, :], jnp.zeros((B - 1, D_MODEL), F32)], axis=0)
    mod3 = _modulation(cc, w_ada[0], b_ada[0][None, :]).reshape(2 * B, 1, 3 * D_MODEL)

    def pair_heads(t, axis):
        shp = t.shape[:axis] + (2, ATT_HEADS // 2, ATT_DH) + t.shape[axis + 1:]
        return jnp.swapaxes(t.reshape(shp), axis, axis + 1).reshape(t.shape)

    w, wo = w_in[0], w_out[0]
    w_in_b = jnp.concatenate([w[:, :OFF_AQ], pair_heads(w[:, OFF_AQ:OFF_AK], 1), w[:, OFF_AK:OFF_AG],
                              pair_heads(w[:, OFF_AG:], 1)], axis=1).astype(BF16)
    w_out_b = jnp.concatenate([wo[:RET_W], pair_heads(wo[RET_W:], 0)], axis=0).astype(BF16)
    dl = jnp.broadcast_to(ret_decay_logit[0].astype(F32)[:, :, None, None], (2, RET_HEADS, 8, LANES))

    r, rvt, aq, ak, av, ag = _project(x, mod3, w_in_b, _rope_tables(L))
    sf, sb, ck, cv = _context(ctx, mod3, w_in_b, dl)
    ret = _retention(r, rvt, sf, sb, dl, ret_gn_w[0].reshape(RET_HEADS, 1, RET_D))

    sink2 = jnp.repeat(att_sink[0].astype(F32).reshape(ATT_KV, ATT_HEADS // ATT_KV), LANES, axis=1)
    return _attention_output(aq, ak, av, ck, cv, ag, sink2, ret, x, mod3, w_out_b, final_norm_w[None, :])
```

```python
import numpy as np
import jax
import jax.numpy as jnp
from jax import lax
from jax.experimental import pallas as pl
from jax.experimental.pallas import tpu as pltpu

F32 = jnp.float32
BF16 = jnp.bfloat16

D_MODEL = 1024
GRID_W = 64
RET_HEADS = 4
RET_D = 128
RET_W = RET_HEADS * RET_D
ATT_HEADS = 8
ATT_KV = 2
ATT_DH = 64
ATT_W = ATT_HEADS * ATT_DH
ATT_BLOCK = 128
WINDOW = 128
ROPE_BASE = 10000.0
EPS = 1e-6
NEG = -1e30
LOG2E = 1.4426950408889634
IN_COLS = 4 * RET_W + 2 * ATT_W + 2 * ATT_KV * ATT_DH
OFF_RQ, OFF_RK, OFF_RV, OFF_RG = 0, RET_W, 2 * RET_W, 3 * RET_W
OFF_AQ = 4 * RET_W
OFF_AK = OFF_AQ + ATT_W
OFF_AV = OFF_AK + ATT_KV * ATT_DH
OFF_AG = OFF_AV + ATT_KV * ATT_DH

LANES = 128
VMEM_LIMIT = 48 * 1024 * 1024

PROJ_ROWS = 1024
PROJ_SUB = 256
RET_CHUNK = 256
RET_PAIR = 2
ATT_ROWS = 1024


def _silu(t):
    return t / (1.0 + jnp.exp(-t))


def _log_sigmoid(t):
    return -(jnp.maximum(-t, 0.0) + jnp.log1p(jnp.exp(-jnp.abs(t))))


def _mod_kernel(c_ref, w_ref, b_ref, o_ref):
    s = _silu(c_ref[...])
    o_ref[...] = jnp.dot(s, w_ref[...], preferred_element_type=F32,
                         precision=lax.Precision.HIGHEST) + b_ref[...]


def _modulation(cc, w_ada, b_ada):
    n = w_ada.shape[1]
    bn = 512
    return pl.pallas_call(
        _mod_kernel,
        out_shape=jax.ShapeDtypeStruct((cc.shape[0], n), F32),
        grid=(n // bn,),
        in_specs=[pl.BlockSpec(cc.shape, lambda i: (0, 0)),
                  pl.BlockSpec((D_MODEL, bn), lambda i: (0, i)),
                  pl.BlockSpec((1, bn), lambda i: (0, i))],
        out_specs=pl.BlockSpec((cc.shape[0], bn), lambda i: (0, i)),
        compiler_params=pltpu.CompilerParams(dimension_semantics=("arbitrary",)),
        name="mod",
    )(cc, w_ada, b_ada)


def _norm_mod(x, m_ref):
    shift = m_ref[0, :, 0:D_MODEL]
    scale = m_ref[0, :, D_MODEL:2 * D_MODEL]
    h = x * lax.rsqrt(jnp.mean(x * x, axis=-1, keepdims=True) + EPS)
    return h * (1.0 + scale) + shift


def _rope_ret(t, cos, sin):
    return t * cos + pltpu.roll(t, 64, 1) * sin


def _rope_att(t, cos, sin, low_half):
    partner = jnp.where(low_half, pltpu.roll(t, 96, 1), pltpu.roll(t, 32, 1))
    return t * cos + partner * sin


def _store_kv_groups(k_ref, v_ref, rows, k, v):
    low = lax.broadcasted_iota(jnp.int32, k.shape, 1) < ATT_DH
    k_ref[0, 0, rows, :] = jnp.where(low, k, 0.0).astype(BF16)
    k_ref[0, 1, rows, :] = jnp.where(low, 0.0, k).astype(BF16)
    v_ref[0, 0, rows, :] = jnp.where(low, v, 1.0).astype(BF16)
    v_ref[0, 1, rows, :] = jnp.where(low, 1.0, v).astype(BF16)


def _pair_heads(nat):
    low = lax.broadcasted_iota(jnp.int32, nat[0].shape, 1) < ATT_DH
    out = []
    for j in range(len(nat)):
        a, b = nat[j // 2], nat[2 + j // 2]
        if j % 2 == 0:
            out.append(jnp.where(low, a, pltpu.roll(b, ATT_DH, 1)))
        else:
            out.append(jnp.where(low, pltpu.roll(a, ATT_DH, 1), b))
    return out


def _proj_kernel(x_ref, m_ref, w_ref, cr_ref, sr_ref, ca_ref, sa_ref,
                 r_ref, rvt_ref, aq_ref, ak_ref, av_ref, ag_ref, wq_scr, wg_scr):

    @pl.when((pl.program_id(0) == 0) & (pl.program_id(1) == 0))
    def _():
        for off, scr in ((OFF_AQ, wq_scr), (OFF_AG, wg_scr)):
            for r in range(0, D_MODEL, LANES):
                nat = [w_ref[r:r + LANES, off + j * LANES:off + (j + 1) * LANES].astype(F32)
                       for j in range(ATT_W // LANES)]
                for j, blk in enumerate(_pair_heads(nat)):
                    scr[r:r + LANES, j * LANES:(j + 1) * LANES] = blk.astype(BF16)

    for t in range(x_ref.shape[1] // PROJ_SUB):
        rows = slice(t * PROJ_SUB, (t + 1) * PROJ_SUB)
        hb = _norm_mod(x_ref[0, rows, :], m_ref).astype(BF16)
        cr, sr, ca, sa = cr_ref[rows, :], sr_ref[rows, :], ca_ref[rows, :], sa_ref[rows, :]
        low_half = (lax.broadcasted_iota(jnp.int32, ca.shape, 1) % ATT_DH) < (ATT_DH // 2)

        def mm(c0, c1):
            return jnp.dot(hb, w_ref[:, c0:c1], preferred_element_type=F32)

        y = mm(OFF_RQ, OFF_RK)
        for h in range(RET_HEADS):
            sl = slice(h * LANES, (h + 1) * LANES)
            r_ref[0, rows, OFF_RQ + h * LANES:OFF_RQ + (h + 1) * LANES] = _rope_ret(y[:, sl], cr, sr).astype(BF16)
        y = mm(OFF_RK, OFF_RV) * (RET_D ** -0.5)
        for h in range(RET_HEADS):
            sl = slice(h * LANES, (h + 1) * LANES)
            r_ref[0, rows, OFF_RK + h * LANES:OFF_RK + (h + 1) * LANES] = _rope_ret(y[:, sl], cr, sr).astype(BF16)
        y = mm(OFF_RV, OFF_RG)
        for h in range(RET_HEADS):
            sl = slice(h * LANES, (h + 1) * LANES)
            rvt_ref[0, sl, rows] = y[:, sl].T.astype(BF16)
        r_ref[0, rows, 2 * RET_W:3 * RET_W] = _silu(mm(OFF_RG, OFF_AQ)).astype(BF16)
        y = jnp.dot(hb, wq_scr[...], preferred_element_type=F32)
        for j in range(ATT_W // LANES):
            sl = slice(j * LANES, (j + 1) * LANES)
            aq_ref[0, rows, sl] = (_rope_att(y[:, sl], ca, sa, low_half) * (ATT_DH ** -0.5 * LOG2E)).astype(BF16)
        y = mm(OFF_AK, OFF_AG)
        _store_kv_groups(ak_ref, av_ref, rows, _rope_att(y[:, 0:LANES], ca, sa, low_half), y[:, LANES:2 * LANES])
        ag_ref[0, rows, :] = _silu(jnp.dot(hb, wg_scr[...], preferred_element_type=F32)).astype(BF16)


def _project(x, mod3, w_in_b, tabs):
    B, L, _ = x.shape
    tm = PROJ_ROWS
    tab_spec = pl.BlockSpec((tm, LANES), lambda b, i: (i, 0))
    row = lambda n: pl.BlockSpec((1, tm, n), lambda b, i: (b, i, 0))
    kv_spec = pl.BlockSpec((1, ATT_KV, tm, LANES), lambda b, i: (b, 0, i, 0))
    kv_shape = jax.ShapeDtypeStruct((B, ATT_KV, L, LANES), BF16)
    return pl.pallas_call(
        _proj_kernel,
        out_shape=(jax.ShapeDtypeStruct((B, L, 3 * RET_W), BF16),
                   jax.ShapeDtypeStruct((B, RET_W, L), BF16),
                   jax.ShapeDtypeStruct((B, L, ATT_W), BF16),
                   kv_shape, kv_shape,
                   jax.ShapeDtypeStruct((B, L, ATT_W), BF16)),
        grid=(B, L // tm),
        in_specs=[row(D_MODEL),
                  pl.BlockSpec((1, 1, 3 * D_MODEL), lambda b, i: (b, 0, 0)),
                  pl.BlockSpec((D_MODEL, IN_COLS), lambda b, i: (0, 0), pipeline_mode=pl.Buffered(1)),
                  tab_spec, tab_spec, tab_spec, tab_spec],
        out_specs=(row(3 * RET_W), pl.BlockSpec((1, RET_W, tm), lambda b, i: (b, 0, i)),
                   row(ATT_W), kv_spec, kv_spec, row(ATT_W)),
        scratch_shapes=[pltpu.VMEM((D_MODEL, ATT_W), BF16), pltpu.VMEM((D_MODEL, ATT_W), BF16)],
        compiler_params=pltpu.CompilerParams(
            dimension_semantics=("arbitrary", "arbitrary"), vmem_limit_bytes=VMEM_LIMIT),
        name="proj",
    )(x, mod3, w_in_b, *tabs)


def _ctx_kernel(x_ref, m_ref, wk_ref, wv_ref, wa_ref, dl_ref, sf_ref, sb_ref, ck_ref, cv_ref):
    lc = x_ref.shape[1]
    hb = _norm_mod(x_ref[0], m_ref).astype(BF16)
    yk = jnp.dot(hb, wk_ref[...], preferred_element_type=F32)
    yv = jnp.dot(hb, wv_ref[...], preferred_element_type=F32)
    ya = jnp.dot(hb, wa_ref[...], preferred_element_type=F32)
    _store_kv_groups(ck_ref, cv_ref, slice(0, lc), ya[:, 0:LANES], ya[:, LANES:2 * LANES])
    lg = _log_sigmoid(dl_ref[...])
    pos = lax.broadcasted_iota(jnp.int32, (lc, 1), 0).astype(F32)
    for h in range(RET_HEADS):
        k = yk[:, h * RET_D:(h + 1) * RET_D] * (RET_D ** -0.5)
        v = yv[:, h * RET_D:(h + 1) * RET_D].astype(BF16)
        wf = jnp.exp(lg[0, h, 0:1, :] * (lc - 1.0 - pos))
        wb = jnp.exp(lg[1, h, 0:1, :] * pos)
        dn = (((0,), (0,)), ((), ()))
        sf_ref[0, h] = lax.dot_general(v, (k * wf).astype(BF16), dn, preferred_element_type=F32)
        sb_ref[0, h] = lax.dot_general(v, (k * wb).astype(BF16), dn, preferred_element_type=F32)


def _context(ctx, mod3, w_in_b, dl):
    B, lc, _ = ctx.shape
    wcol = lambda off, n: pl.BlockSpec((D_MODEL, n), lambda b: (0, off // n))
    st = jax.ShapeDtypeStruct((B, RET_HEADS, RET_D, RET_D), F32)
    kv = jax.ShapeDtypeStruct((B, ATT_KV, lc, LANES), BF16)
    st_spec = pl.BlockSpec((1, RET_HEADS, RET_D, RET_D), lambda b: (b, 0, 0, 0))
    kv_spec = pl.BlockSpec((1, ATT_KV, lc, LANES), lambda b: (b, 0, 0, 0))
    ctx_row = mod3.shape[0] // 2
    return pl.pallas_call(
        _ctx_kernel,
        out_shape=(st, st, kv, kv),
        grid=(B,),
        in_specs=[pl.BlockSpec((1, lc, D_MODEL), lambda b: (b, 0, 0)),
                  pl.BlockSpec((1, 1, 3 * D_MODEL), lambda b: (ctx_row, 0, 0)),
                  wcol(OFF_RK, RET_W), wcol(OFF_RV, RET_W), wcol(OFF_AK, 2 * LANES),
                  pl.BlockSpec(dl.shape, lambda b: (0, 0, 0, 0))],
        out_specs=(st_spec, st_spec, kv_spec, kv_spec),
        compiler_params=pltpu.CompilerParams(
            dimension_semantics=("arbitrary",), vmem_limit_bytes=VMEM_LIMIT),
        name="ctx",
    )(ctx, mod3, w_in_b, w_in_b, w_in_b, dl)


def _ret_kernel(q_ref, k_ref, vt_ref, g_ref, sf_ref, sb_ref, dl_ref, gn_ref, o_ref,
                uf_scr, rb_scr, kdec_scr, qdec_scr, cdec_scr, decay_scr):
    L = q_ref.shape[1]
    C = RET_CHUNK
    nch = L // C
    heads = range(RET_PAIR)
    lanes = lambda h: slice(h * RET_D, (h + 1) * RET_D)

    @pl.when(pl.program_id(1) == 0)
    def _():
        lg = _log_sigmoid(dl_ref[...])
        pos = lax.broadcasted_iota(jnp.int32, (C, 1), 0).astype(F32)
        posl = lax.broadcasted_iota(jnp.int32, (8, C), 1).astype(F32)
        diff = (lax.broadcasted_iota(jnp.int32, (C, C), 1)
                - lax.broadcasted_iota(jnp.int32, (C, C), 0)).astype(F32)
        for h in heads:
            lgf, lgb = lg[0, h, 0:1, :], lg[1, h, 0:1, :]
            lgf1, lgb1 = lgf[:, 0:1], lgb[:, 0:1]
            kdec_scr[h, 0] = jnp.exp(lgf * (C - 1.0 - pos))
            kdec_scr[h, 1] = jnp.exp(lgb * pos)
            qdec_scr[h, 0] = jnp.exp(lgf1 * (posl + 1.0))
            qdec_scr[h, 1] = jnp.exp(lgb1 * (C - posl))
            cdec_scr[h, 0] = jnp.broadcast_to(jnp.exp(lgf * C), (8, LANES))
            cdec_scr[h, 1] = jnp.broadcast_to(jnp.exp(lgb * C), (8, LANES))
            decay_scr[h] = jnp.where(diff >= 0.0, jnp.exp(lgf1 * jnp.maximum(diff, 0.0)),
                                     jnp.exp(lgb1 * jnp.maximum(-diff, 0.0)))

    nt = (((1,), (1,)), ((), ()))
    chunk = lambda n: slice(n * C, (n + 1) * C)

    rb = [sb_ref[0, h] for h in heads]
    for n in reversed(range(nch)):
        for h in heads:
            kc = k_ref[0, chunk(n), lanes(h)].astype(F32)
            kd = jnp.concatenate([(kc * kdec_scr[h, 0]).astype(BF16), (kc * kdec_scr[h, 1]).astype(BF16)],
                                 axis=1)
            u = jnp.dot(vt_ref[0, lanes(h), chunk(n)], kd, preferred_element_type=F32)
            uf_scr[h, n] = u[:, 0:RET_D]
            rb_scr[h, n] = rb[h].astype(BF16)
            rb[h] = rb[h] * cdec_scr[h, 1, 0:1, :] + u[:, RET_D:2 * RET_D]

    def kq(h, n):
        return lax.dot_general(k_ref[0, chunk(n), lanes(h)], q_ref[0, chunk(n), lanes(h)], nt,
                               preferred_element_type=F32)

    rf = [sf_ref[0, h] for h in heads]
    s_next = [kq(h, 0) for h in heads]
    for n in range(nch):
        for h in heads:
            s = s_next[h]
            q = q_ref[0, chunk(n), lanes(h)]
            states = jnp.concatenate([rf[h].astype(BF16), rb_scr[h, n]], axis=0)
            cross = lax.dot_general(states, q, nt, preferred_element_type=F32)
            if n + 1 < nch:
                s_next[h] = kq(h, n + 1)
                rf[h] = rf[h] * cdec_scr[h, 0, 0:1, :] + uf_scr[h, n]
            inner = jnp.dot(vt_ref[0, lanes(h), chunk(n)], (s * decay_scr[h]).astype(BF16),
                            preferred_element_type=F32)
            y = (inner + cross[0:RET_D] * qdec_scr[h, 0, 0:1, :]
                 + cross[RET_D:2 * RET_D] * qdec_scr[h, 1, 0:1, :])
            mu = jnp.mean(y, axis=0, keepdims=True)
            yc = y - mu
            var = jnp.mean(yc * yc, axis=0, keepdims=True)
            yn = (yc * lax.rsqrt(var + EPS)).T * gn_ref[h]
            o_ref[0, chunk(n), lanes(h)] = (yn * g_ref[0, chunk(n), lanes(h)].astype(F32)).astype(BF16)


def _retention(r, rvt, sf, sb, dl, gn):
    B, L, _ = r.shape
    P, C = RET_PAIR, RET_CHUNK
    col = lambda off: pl.BlockSpec((1, L, P * RET_D), lambda hp, b: (b, 0, off + hp))
    st_spec = pl.BlockSpec((1, P, RET_D, RET_D), lambda hp, b: (b, hp, 0, 0))
    npair = RET_HEADS // P
    return pl.pallas_call(
        _ret_kernel,
        out_shape=jax.ShapeDtypeStruct((B, L, RET_W), BF16),
        grid=(npair, B),
        in_specs=[col(0), col(npair),
                  pl.BlockSpec((1, P * RET_D, L), lambda hp, b: (b, hp, 0)),
                  col(2 * npair),
                  st_spec, st_spec,
                  pl.BlockSpec((2, P, 8, LANES), lambda hp, b: (0, hp, 0, 0)),
                  pl.BlockSpec((P, 1, RET_D), lambda hp, b: (hp, 0, 0))],
        out_specs=pl.BlockSpec((1, L, P * RET_D), lambda hp, b: (b, 0, hp)),
        scratch_shapes=[pltpu.VMEM((P, L // C, RET_D, RET_D), F32),
                        pltpu.VMEM((P, L // C, RET_D, RET_D), BF16),
                        pltpu.VMEM((P, 2, C, LANES), F32),
                        pltpu.VMEM((P, 2, 8, C), F32),
                        pltpu.VMEM((P, 2, 8, LANES), F32),
                        pltpu.VMEM((P, C, C), F32)],
        compiler_params=pltpu.CompilerParams(
            dimension_semantics=("arbitrary", "arbitrary"), vmem_limit_bytes=VMEM_LIMIT),
        name="ret",
    )(r, r, rvt, r, sf, sb, dl, gn)


def _att_kernel(q_ref, k_ref, v_ref, ck_ref, cv_ref, g_ref, sink_ref, r_ref, x_ref, m_ref, w_ref, fn_ref,
                o_ref, w_scr):
    @pl.when((pl.program_id(0) == 0) & (pl.program_id(1) == 0))
    def _():
        w_scr[0:RET_W, :] = w_ref[0:RET_W, :]
        for j in range(ATT_HEADS // ATT_KV):
            for half in range(ATT_KV):
                dst = RET_W + j * LANES + half * ATT_DH
                src = RET_W + (j + half * (ATT_HEADS // ATT_KV)) * ATT_DH
                w_scr[dst:dst + ATT_DH, :] = w_ref[src:src + ATT_DH, :]

    L = k_ref.shape[2]
    nb_total = L // ATT_BLOCK
    nb_step = q_ref.shape[1] // ATT_BLOCK
    nhb = ATT_W // LANES
    i = pl.program_id(1)
    r_i = lax.broadcasted_iota(jnp.int32, (ATT_BLOCK, ATT_BLOCK), 0)
    q_i = lax.broadcasted_iota(jnp.int32, (ATT_BLOCK, ATT_BLOCK), 1)
    band_prev = jnp.where(q_i <= r_i, 0.0, NEG).astype(F32)
    band_next = jnp.where(r_i <= q_i, 0.0, NEG).astype(F32)
    nt = (((1,), (1,)), ((), ()))
    tn = (((0,), (0,)), ((), ()))

    def key_rows(jb):
        n = i * nb_step + jb
        blk = lambda t: pl.ds(pl.multiple_of(t * ATT_BLOCK, ATT_BLOCK), ATT_BLOCK)
        return n, (blk(jnp.maximum(n - 1, 0)), blk(n), blk(jnp.minimum(n + 1, nb_total - 1)))

    def scores(jb, kv):
        n, (p_rows, o_rows, n_rows) = key_rows(jb)
        q = q_ref[0, jb * ATT_BLOCK:(jb + 1) * ATT_BLOCK, :]
        qall = jnp.concatenate([q[:, j * LANES:(j + 1) * LANES] for j in range(nhb)], axis=0)
        kall = jnp.concatenate([ck_ref[0, kv], k_ref[0, kv, p_rows, :], k_ref[0, kv, o_rows, :],
                                k_ref[0, kv, n_rows, :]], axis=0)
        return lax.dot_general(kall, qall, nt, preferred_element_type=F32)

    def attend(jb, kv, s):
        n, (p_rows, o_rows, n_rows) = key_rows(jb)
        lc = ck_ref.shape[2]
        bias_prev = jnp.concatenate([band_prev + jnp.where(n == 0, NEG, 0.0)] * nhb, axis=1)
        bias_next = jnp.concatenate([band_next + jnp.where(n == nb_total - 1, NEG, 0.0)] * nhb, axis=1)
        parts = [s[0:lc], s[lc:lc + ATT_BLOCK] + bias_prev, s[lc + ATT_BLOCK:lc + 2 * ATT_BLOCK],
                 s[lc + 2 * ATT_BLOCK:] + bias_next]
        sink = sink_ref[kv:kv + 1, :] * LOG2E
        m = sink
        for t in parts:
            m = jnp.maximum(m, jnp.max(t, axis=0, keepdims=True))
        p = jnp.concatenate([jnp.exp2(t - m).astype(BF16) for t in parts], axis=0)
        vaug = jnp.concatenate([cv_ref[0, kv], v_ref[0, kv, p_rows, :], v_ref[0, kv, o_rows, :],
                                v_ref[0, kv, n_rows, :]], axis=0)
        o = lax.dot_general(vaug, p, tn, preferred_element_type=F32)
        if kv == 0:
            val, ones = slice(0, ATT_DH), slice(ATT_DH, ATT_DH + 1)
        else:
            val, ones = slice(ATT_DH, 2 * ATT_DH), slice(0, 1)
        den = o[ones] + jnp.exp2(sink - m)
        return o[val] * (1.0 / den)

    def gated_block(jb, halves):
        rows = slice(jb * ATT_BLOCK, (jb + 1) * ATT_BLOCK)
        comb = jnp.concatenate(halves, axis=0)
        return jnp.concatenate(
            [(comb[:, j * LANES:(j + 1) * LANES].T
              * g_ref[0, rows, j * LANES:(j + 1) * LANES].astype(F32)).astype(BF16) for j in range(nhb)], axis=1)

    def finish(jb, att_blk):
        rows = slice(jb * ATT_BLOCK, (jb + 1) * ATT_BLOCK)
        mix_in = jnp.concatenate([r_ref[0, rows, :], att_blk], axis=1)
        mixed = jnp.dot(mix_in, w_scr[...], preferred_element_type=F32)
        xn = x_ref[0, rows, :] + m_ref[0, :, 2 * D_MODEL:3 * D_MODEL] * mixed
        o_ref[0, rows, :] = xn * lax.rsqrt(jnp.mean(xn * xn, axis=-1, keepdims=True) + EPS) * fn_ref[...]

    chains = [(jb, kv) for jb in range(nb_step) for kv in range(ATT_KV)]
    s_next = scores(*chains[0])
    halves, pending = [], []
    for c, (jb, kv) in enumerate(chains):
        s = s_next
        if c + 1 < len(chains):
            s_next = scores(*chains[c + 1])
        halves.append(attend(jb, kv, s))
        if pending:
            finish(*pending.pop())
        if kv == ATT_KV - 1:
            pending.append((jb, gated_block(jb, halves)))
            halves = []
    finish(*pending.pop())


def _attention_output(aq, ak, av, ck, cv, ag, sink2, ret, x, mod3, w_out_b, fnw):
    B, L, _ = aq.shape
    lc = ck.shape[2]
    tq = ATT_ROWS
    row = lambda n: pl.BlockSpec((1, tq, n), lambda b, i: (b, i, 0))
    full = lambda n: pl.BlockSpec((1, ATT_KV, n, LANES), lambda b, i: (b, 0, 0, 0))
    return pl.pallas_call(
        _att_kernel,
        out_shape=jax.ShapeDtypeStruct((B, L, D_MODEL), F32),
        grid=(B, L // tq),
        in_specs=[row(ATT_W), full(L), full(L), full(lc), full(lc), row(ATT_W),
                  pl.BlockSpec(sink2.shape, lambda b, i: (0, 0)),
                  row(RET_W), row(D_MODEL),
                  pl.BlockSpec((1, 1, 3 * D_MODEL), lambda b, i: (b, 0, 0)),
                  pl.BlockSpec((RET_W + ATT_W, D_MODEL), lambda b, i: (0, 0)),
                  pl.BlockSpec((1, D_MODEL), lambda b, i: (0, 0))],
        out_specs=row(D_MODEL),
        scratch_shapes=[pltpu.VMEM((RET_W + ATT_W, D_MODEL), BF16)],
        compiler_params=pltpu.CompilerParams(
            dimension_semantics=("arbitrary", "arbitrary"), vmem_limit_bytes=VMEM_LIMIT),
        name="att",
    )(aq, ak, av, ck, cv, ag, sink2, ret, x, mod3, w_out_b, fnw)


def _rope_tables(L):
    pos = np.arange(L)
    rows, cols = (pos // GRID_W).astype(np.float64), (pos % GRID_W).astype(np.float64)

    def tables(dh):
        nf = dh // 4
        inv = ROPE_BASE ** (-np.arange(nf, dtype=np.float64) / nf)
        ang = np.concatenate([rows[:, None] * inv, cols[:, None] * inv], axis=-1)
        cos, sin = np.cos(ang), np.sin(ang)
        reps = LANES // dh
        return (jnp.asarray(np.tile(np.concatenate([cos, cos], axis=-1), (1, reps)), F32),
                jnp.asarray(np.tile(np.concatenate([-sin, sin], axis=-1), (1, reps)), F32))

    cr, sr = tables(RET_D)
    ca, sa = tables(ATT_DH)
    return cr, sr, ca, sa


def kernel(x, c, ctx, c_ctx, w_ada, b_ada, w_in, ret_decay_logit, ret_gn_w, att_sink, w_out, final_norm_w):
    B, L, _ = x.shape
    assert w_ada.shape[0] == 1, "single-layer trunk"

    cc = jnp.concatenate([c, c_ctx[None, :], jnp.zeros((B - 1, D_MODEL), F32)], axis=0)
    mod3 = _modulation(cc, w_ada[0], b_ada[0][None, :]).reshape(2 * B, 1, 3 * D_MODEL)

    w_in_b, w_out_b = w_in[0].astype(BF16), w_out[0].astype(BF16)
    dl =jnp.broadcast_to(ret_decay_logit[0].astype(F32)[:, :, None, None], (2, RET_HEADS, 8, LANES))

    r, rvt, aq, ak, av, ag = _project(x, mod3, w_in_b, _rope_tables(L))
    sf, sb, ck, cv = _context(ctx, mod3, w_in_b, dl)
    ret = _retention(r, rvt, sf, sb, dl, ret_gn_w[0].reshape(RET_HEADS, 1, RET_D))

    sink2 = jnp.repeat(att_sink[0].astype(F32).reshape(ATT_KV, ATT_HEADS // ATT_KV), LANES, axis=1)
    return _attention_output(aq, ak, av, ck, cv, ag, sink2, ret, x, mod3, w_out_b, final_norm_w[None, :])
```

```python
import numpy as np
import jax
import jax.numpy as jnp
from jax import lax
from jax.experimental import pallas as pl
from jax.experimental.pallas import tpu as pltpu

F32 = jnp.float32
BF16 = jnp.bfloat16

D_MODEL = 1024
GRID_W = 64
RET_HEADS = 4
RET_D = 128
RET_W = RET_HEADS * RET_D
ATT_HEADS = 8
ATT_KV = 2
ATT_DH = 64
ATT_W = ATT_HEADS * ATT_DH
ATT_BLOCK = 128
WINDOW = 128
ROPE_BASE = 10000.0
EPS = 1e-6
NEG = -1e30
LOG2E = 1.4426950408889634
IN_COLS = 4 * RET_W + 2 * ATT_W + 2 * ATT_KV * ATT_DH
OFF_RQ, OFF_RK, OFF_RV, OFF_RG = 0, RET_W, 2 * RET_W, 3 * RET_W
OFF_AQ = 4 * RET_W
OFF_AK = OFF_AQ + ATT_W
OFF_AV = OFF_AK + ATT_KV * ATT_DH
OFF_AG = OFF_AV + ATT_KV * ATT_DH

LANES = 128
VMEM_LIMIT = 48 * 1024 * 1024

PROJ_ROWS = 1024
PROJ_SUB = 256
RET_CHUNK = 256
RET_PAIR = 2
ATT_ROWS = 1024
OUT_BLOCKS = 2


def _silu(t):
    return t / (1.0 + jnp.exp(-t))


def _log_sigmoid(t):
    return -(jnp.maximum(-t, 0.0) + jnp.log1p(jnp.exp(-jnp.abs(t))))


def _mod_kernel(c_ref, w_ref, b_ref, o_ref):
    s = _silu(c_ref[...])
    o_ref[...] = jnp.dot(s, w_ref[...], preferred_element_type=F32,
                         precision=lax.Precision.HIGHEST) + b_ref[...]


def _modulation(cc, w_ada, b_ada):
    n = w_ada.shape[1]
    bn = 512
    return pl.pallas_call(
        _mod_kernel,
        out_shape=jax.ShapeDtypeStruct((cc.shape[0], n), F32),
        grid=(n // bn,),
        in_specs=[pl.BlockSpec(cc.shape, lambda i: (0, 0)),
                  pl.BlockSpec((D_MODEL, bn), lambda i: (0, i)),
                  pl.BlockSpec((1, bn), lambda i: (0, i))],
        out_specs=pl.BlockSpec((cc.shape[0], bn), lambda i: (0, i)),
        compiler_params=pltpu.CompilerParams(dimension_semantics=("arbitrary",)),
        name="mod",
    )(cc, w_ada, b_ada)


def _norm_mod(x, m_ref):
    shift = m_ref[0, :, 0:D_MODEL]
    scale = m_ref[0, :, D_MODEL:2 * D_MODEL]
    h = x * lax.rsqrt(jnp.mean(x * x, axis=-1, keepdims=True) + EPS)
    return h * (1.0 + scale) + shift


def _rope_ret(t, cos, sin):
    return t * cos + pltpu.roll(t, 64, 1) * sin


def _rope_att(t, cos, sin, low_half):
    partner = jnp.where(low_half, pltpu.roll(t, 96, 1), pltpu.roll(t, 32, 1))
    return t * cos + partner * sin


def _store_kv_groups(k_ref, v_ref, rows, k, v):
    low = lax.broadcasted_iota(jnp.int32, k.shape, 1) < ATT_DH
    k_ref[0, 0, rows, :] = jnp.where(low, k, 0.0).astype(BF16)
    k_ref[0, 1, rows, :] = jnp.where(low, 0.0, k).astype(BF16)
    v_ref[0, 0, rows, :] = jnp.where(low, v, 1.0).astype(BF16)
    v_ref[0, 1, rows, :] = jnp.where(low, 1.0, v).astype(BF16)


def _pair_heads(nat):
    low = lax.broadcasted_iota(jnp.int32, nat[0].shape, 1) < ATT_DH
    out = []
    for j in range(len(nat)):
        a, b = nat[j // 2], nat[2 + j // 2]
        if j % 2 == 0:
            out.append(jnp.where(low, a, pltpu.roll(b, ATT_DH, 1)))
        else:
            out.append(jnp.where(low, pltpu.roll(a, ATT_DH, 1), b))
    return out


def _proj_kernel(x_ref, m_ref, w_ref, cr_ref, sr_ref, ca_ref, sa_ref,
                 r_ref, rvt_ref, aq_ref, ak_ref, av_ref, ag_ref, wq_scr, wg_scr):

    @pl.when((pl.program_id(0) == 0) & (pl.program_id(1) == 0))
    def _():
        for off, scr in ((OFF_AQ, wq_scr), (OFF_AG, wg_scr)):
            for r in range(0, D_MODEL, LANES):
                nat = [w_ref[r:r + LANES, off + j * LANES:off + (j + 1) * LANES].astype(F32)
                       for j in range(ATT_W // LANES)]
                for j, blk in enumerate(_pair_heads(nat)):
                    scr[r:r + LANES, j * LANES:(j + 1) * LANES] = blk.astype(BF16)

    for t in range(x_ref.shape[1] // PROJ_SUB):
        rows = slice(t * PROJ_SUB, (t + 1) * PROJ_SUB)
        hb = _norm_mod(x_ref[0, rows, :], m_ref).astype(BF16)
        cr, sr, ca, sa = cr_ref[rows, :], sr_ref[rows, :], ca_ref[rows, :], sa_ref[rows, :]
        low_half = (lax.broadcasted_iota(jnp.int32, ca.shape, 1) % ATT_DH) < (ATT_DH // 2)

        def mm(c0, c1):
            return jnp.dot(hb, w_ref[:, c0:c1], preferred_element_type=F32)

        y = mm(OFF_RQ, OFF_RK)
        for h in range(RET_HEADS):
            sl = slice(h * LANES, (h + 1) * LANES)
            r_ref[0, rows, OFF_RQ + h * LANES:OFF_RQ + (h + 1) * LANES] = _rope_ret(y[:, sl], cr, sr).astype(BF16)
        y = mm(OFF_RK, OFF_RV) * (RET_D ** -0.5)
        for h in range(RET_HEADS):
            sl = slice(h * LANES, (h + 1) * LANES)
            r_ref[0, rows, OFF_RK + h * LANES:OFF_RK + (h + 1) * LANES] = _rope_ret(y[:, sl], cr, sr).astype(BF16)
        y = mm(OFF_RV, OFF_RG)
        for h in range(RET_HEADS):
            sl = slice(h * LANES, (h + 1) * LANES)
            rvt_ref[0, sl, rows] = y[:, sl].T.astype(BF16)
        r_ref[0, rows, 2 * RET_W:3 * RET_W] = _silu(mm(OFF_RG, OFF_AQ)).astype(BF16)
        y = jnp.dot(hb, wq_scr[...], preferred_element_type=F32)
        for j in range(ATT_W // LANES):
            sl = slice(j * LANES, (j + 1) * LANES)
            aq_ref[0, rows, sl] = (_rope_att(y[:, sl], ca, sa, low_half) * (ATT_DH ** -0.5 * LOG2E)).astype(BF16)
        y = mm(OFF_AK, OFF_AG)
        _store_kv_groups(ak_ref, av_ref, rows, _rope_att(y[:, 0:LANES], ca, sa, low_half), y[:, LANES:2 * LANES])
        ag_ref[0, rows, :] = _silu(jnp.dot(hb, wg_scr[...], preferred_element_type=F32)).astype(BF16)


def _project(x, mod3, w_in_b, tabs):
    B, L, _ = x.shape
    tm = PROJ_ROWS
    tab_spec = pl.BlockSpec((tm, LANES), lambda b, i: (i, 0))
    row = lambda n: pl.BlockSpec((1, tm, n), lambda b, i: (b, i, 0))
    kv_spec = pl.BlockSpec((1, ATT_KV, tm, LANES), lambda b, i: (b, 0, i, 0))
    kv_shape = jax.ShapeDtypeStruct((B, ATT_KV, L, LANES), BF16)
    return pl.pallas_call(
        _proj_kernel,
        out_shape=(jax.ShapeDtypeStruct((B, L, 3 * RET_W), BF16),
                   jax.ShapeDtypeStruct((B, RET_W, L), BF16),
                   jax.ShapeDtypeStruct((B, L, ATT_W), BF16),
                   kv_shape, kv_shape,
                   jax.ShapeDtypeStruct((B, L, ATT_W), BF16)),
        grid=(B, L // tm),
        in_specs=[row(D_MODEL),
                  pl.BlockSpec((1, 1, 3 * D_MODEL), lambda b, i: (b, 0, 0)),
                  pl.BlockSpec((D_MODEL, IN_COLS), lambda b, i: (0, 0), pipeline_mode=pl.Buffered(1)),
                  tab_spec, tab_spec, tab_spec, tab_spec],
        out_specs=(row(3 * RET_W), pl.BlockSpec((1, RET_W, tm), lambda b, i: (b, 0, i)),
                   row(ATT_W), kv_spec, kv_spec, row(ATT_W)),
        scratch_shapes=[pltpu.VMEM((D_MODEL, ATT_W), BF16), pltpu.VMEM((D_MODEL, ATT_W), BF16)],
        compiler_params=pltpu.CompilerParams(
            dimension_semantics=("arbitrary", "arbitrary"), vmem_limit_bytes=VMEM_LIMIT),
        name="proj",
    )(x, mod3, w_in_b, *tabs)


def _ctx_kernel(x_ref, m_ref, wk_ref, wv_ref, wa_ref, dl_ref, sf_ref, sb_ref, ck_ref, cv_ref):
    lc = x_ref.shape[1]
    hb = _norm_mod(x_ref[0], m_ref).astype(BF16)
    yk = jnp.dot(hb, wk_ref[...], preferred_element_type=F32)
    yv = jnp.dot(hb, wv_ref[...], preferred_element_type=F32)
    ya = jnp.dot(hb, wa_ref[...], preferred_element_type=F32)
    _store_kv_groups(ck_ref, cv_ref, slice(0, lc), ya[:, 0:LANES], ya[:, LANES:2 * LANES])
    lg = _log_sigmoid(dl_ref[...])
    pos = lax.broadcasted_iota(jnp.int32, (lc, 1), 0).astype(F32)
    for h in range(RET_HEADS):
        k = yk[:, h * RET_D:(h + 1) * RET_D] * (RET_D ** -0.5)
        v = yv[:, h * RET_D:(h + 1) * RET_D].astype(BF16)
        wf = jnp.exp(lg[0, h, 0:1, :] * (lc - 1.0 - pos))
        wb = jnp.exp(lg[1, h, 0:1, :] * pos)
        dn = (((0,), (0,)), ((), ()))
        sf_ref[0, h] = lax.dot_general(v, (k * wf).astype(BF16), dn, preferred_element_type=F32)
        sb_ref[0, h] = lax.dot_general(v, (k * wb).astype(BF16), dn, preferred_element_type=F32)


def _context(ctx, mod3, w_in_b, dl):
    B, lc, _ = ctx.shape
    wcol = lambda off, n: pl.BlockSpec((D_MODEL, n), lambda b: (0, off // n))
    st = jax.ShapeDtypeStruct((B, RET_HEADS, RET_D, RET_D), F32)
    kv = jax.ShapeDtypeStruct((B, ATT_KV, lc, LANES), BF16)
    st_spec = pl.BlockSpec((1, RET_HEADS, RET_D, RET_D), lambda b: (b, 0, 0, 0))
    kv_spec = pl.BlockSpec((1, ATT_KV, lc, LANES), lambda b: (b, 0, 0, 0))
    ctx_row = mod3.shape[0] // 2
    return pl.pallas_call(
        _ctx_kernel,
        out_shape=(st, st, kv, kv),
        grid=(B,),
        in_specs=[pl.BlockSpec((1, lc, D_MODEL), lambda b: (b, 0, 0)),
                  pl.BlockSpec((1, 1, 3 * D_MODEL), lambda b: (ctx_row, 0, 0)),
                  wcol(OFF_RK, RET_W), wcol(OFF_RV, RET_W), wcol(OFF_AK, 2 * LANES),
                  pl.BlockSpec(dl.shape, lambda b: (0, 0, 0, 0))],
        out_specs=(st_spec, st_spec, kv_spec, kv_spec),
        compiler_params=pltpu.CompilerParams(
            dimension_semantics=("arbitrary",), vmem_limit_bytes=VMEM_LIMIT),
        name="ctx",
    )(ctx, mod3, w_in_b, w_in_b, w_in_b, dl)


def _ret_kernel(q_ref, k_ref, vt_ref, g_ref, sf_ref, sb_ref, dl_ref, gn_ref, o_ref,
                uf_scr, rb_scr, kdec_scr, qdec_scr, cdec_scr, decay_scr):
    L = q_ref.shape[1]
    C = RET_CHUNK
    nch = L // C
    heads = range(RET_PAIR)
    lanes = lambda h: slice(h * RET_D, (h + 1) * RET_D)

    @pl.when(pl.program_id(1) == 0)
    def _():
        lg = _log_sigmoid(dl_ref[...])
        pos = lax.broadcasted_iota(jnp.int32, (C, 1), 0).astype(F32)
        posl = lax.broadcasted_iota(jnp.int32, (8, C), 1).astype(F32)
        diff = (lax.broadcasted_iota(jnp.int32, (C, C), 1)
                - lax.broadcasted_iota(jnp.int32, (C, C), 0)).astype(F32)
        for h in heads:
            lgf, lgb = lg[0, h, 0:1, :], lg[1, h, 0:1, :]
            lgf1, lgb1 = lgf[:, 0:1], lgb[:, 0:1]
            kdec_scr[h, 0] = jnp.exp(lgf * (C - 1.0 - pos))
            kdec_scr[h, 1] = jnp.exp(lgb * pos)
            qdec_scr[h, 0] = jnp.exp(lgf1 * (posl + 1.0))
            qdec_scr[h, 1] = jnp.exp(lgb1 * (C - posl))
            cdec_scr[h, 0] = jnp.broadcast_to(jnp.exp(lgf * C), (8, LANES))
            cdec_scr[h, 1] = jnp.broadcast_to(jnp.exp(lgb * C), (8, LANES))
            decay_scr[h] = jnp.where(diff >= 0.0, jnp.exp(lgf1 * jnp.maximum(diff, 0.0)),
                                     jnp.exp(lgb1 * jnp.maximum(-diff, 0.0)))

    nt = (((1,), (1,)), ((), ()))
    chunk = lambda n: slice(n * C, (n + 1) * C)

    rb = [sb_ref[0, h] for h in heads]
    for n in reversed(range(nch)):
        for h in heads:
            kc = k_ref[0, chunk(n), lanes(h)].astype(F32)
            kd = jnp.concatenate([(kc * kdec_scr[h, 0]).astype(BF16), (kc * kdec_scr[h, 1]).astype(BF16)],
                                 axis=1)
            u = jnp.dot(vt_ref[0, lanes(h), chunk(n)], kd, preferred_element_type=F32)
            uf_scr[h, n] = u[:, 0:RET_D]
            rb_scr[h, n] = rb[h].astype(BF16)
            rb[h] = rb[h] * cdec_scr[h, 1, 0:1, :] + u[:, RET_D:2 * RET_D]

    def kq(h, n):
        return lax.dot_general(k_ref[0, chunk(n), lanes(h)], q_ref[0, chunk(n), lanes(h)], nt,
                               preferred_element_type=F32)

    rf = [sf_ref[0, h] for h in heads]
    s_next = [kq(h, 0) for h in heads]
    for n in range(nch):
        for h in heads:
            s = s_next[h]
            q = q_ref[0, chunk(n), lanes(h)]
            states = jnp.concatenate([rf[h].astype(BF16), rb_scr[h, n]], axis=0)
            cross = lax.dot_general(states, q, nt, preferred_element_type=F32)
            if n + 1 < nch:
                s_next[h] = kq(h, n + 1)
                rf[h] = rf[h] * cdec_scr[h, 0, 0:1, :] + uf_scr[h, n]
            inner = jnp.dot(vt_ref[0, lanes(h), chunk(n)], (s * decay_scr[h]).astype(BF16),
                            preferred_element_type=F32)
            y = (inner + cross[0:RET_D] * qdec_scr[h, 0, 0:1, :]
                 + cross[RET_D:2 * RET_D] * qdec_scr[h, 1, 0:1, :])
            mu = jnp.mean(y, axis=0, keepdims=True)
            yc = y - mu
            var = jnp.mean(yc * yc, axis=0, keepdims=True)
            yn = (yc * lax.rsqrt(var + EPS)).T * gn_ref[h]
            o_ref[0, chunk(n), lanes(h)] = (yn * g_ref[0, chunk(n), lanes(h)].astype(F32)).astype(BF16)


def _retention(r, rvt, sf, sb, dl, gn):
    B, L, _ = r.shape
    P, C = RET_PAIR, RET_CHUNK
    col = lambda off: pl.BlockSpec((1, L, P * RET_D), lambda hp, b: (b, 0, off + hp))
    st_spec = pl.BlockSpec((1, P, RET_D, RET_D), lambda hp, b: (b, hp, 0, 0))
    npair = RET_HEADS // P
    return pl.pallas_call(
        _ret_kernel,
        out_shape=jax.ShapeDtypeStruct((B, L, RET_W), BF16),
        grid=(npair, B),
        in_specs=[col(0), col(npair),
                  pl.BlockSpec((1, P * RET_D, L), lambda hp, b: (b, hp, 0)),
                  col(2 * npair),
                  st_spec, st_spec,
                  pl.BlockSpec((2, P, 8, LANES), lambda hp, b: (0, hp, 0, 0)),
                  pl.BlockSpec((P, 1, RET_D), lambda hp, b: (hp, 0, 0))],
        out_specs=pl.BlockSpec((1, L, P * RET_D), lambda hp, b: (b, 0, hp)),
        scratch_shapes=[pltpu.VMEM((P, L // C, RET_D, RET_D), F32),
                        pltpu.VMEM((P, L // C, RET_D, RET_D), BF16),
                        pltpu.VMEM((P, 2, C, LANES), F32),
                        pltpu.VMEM((P, 2, 8, C), F32),
                        pltpu.VMEM((P, 2, 8, LANES), F32),
                        pltpu.VMEM((P, C, C), F32)],
        compiler_params=pltpu.CompilerParams(
            dimension_semantics=("arbitrary", "arbitrary"), vmem_limit_bytes=VMEM_LIMIT),
        name="ret",
    )(r, r, rvt, r, sf, sb, dl, gn)


def _att_kernel(q_ref, k_ref, v_ref, ck_ref, cv_ref, g_ref, sink_ref, r_ref, x_ref, m_ref, w_ref, fn_ref,
                o_ref, w_scr):
    @pl.when((pl.program_id(0) == 0) & (pl.program_id(1) == 0))
    def _():
        w_scr[0:RET_W, :] = w_ref[0:RET_W, :]
        for j in range(ATT_HEADS // ATT_KV):
            for half in range(ATT_KV):
                dst = RET_W + j * LANES + half * ATT_DH
                src = RET_W + (j + half * (ATT_HEADS // ATT_KV)) * ATT_DH
                w_scr[dst:dst + ATT_DH, :] = w_ref[src:src + ATT_DH, :]

    L = k_ref.shape[2]
    nb_total = L // ATT_BLOCK
    nb_step = q_ref.shape[1] // ATT_BLOCK
    nhb = ATT_W // LANES
    i = pl.program_id(1)
    r_i = lax.broadcasted_iota(jnp.int32, (ATT_BLOCK, ATT_BLOCK), 0)
    q_i = lax.broadcasted_iota(jnp.int32, (ATT_BLOCK, ATT_BLOCK), 1)
    band_prev = jnp.where(q_i <= r_i, 0.0, NEG).astype(F32)
    band_next = jnp.where(r_i <= q_i, 0.0, NEG).astype(F32)
    nt = (((1,), (1,)), ((), ()))
    tn = (((0,), (0,)), ((), ()))

    def key_rows(jb):
        n = i * nb_step + jb
        blk = lambda t: pl.ds(pl.multiple_of(t * ATT_BLOCK, ATT_BLOCK), ATT_BLOCK)
        return n, (blk(jnp.maximum(n - 1, 0)), blk(n), blk(jnp.minimum(n + 1, nb_total - 1)))

    def scores(jb, kv):
        n, (p_rows, o_rows, n_rows) = key_rows(jb)
        q = q_ref[0, jb * ATT_BLOCK:(jb + 1) * ATT_BLOCK, :]
        qall = jnp.concatenate([q[:, j * LANES:(j + 1) * LANES] for j in range(nhb)], axis=0)
        kall = jnp.concatenate([ck_ref[0, kv], k_ref[0, kv, p_rows, :], k_ref[0, kv, o_rows, :],
                                k_ref[0, kv, n_rows, :]], axis=0)
        return lax.dot_general(kall, qall, nt, preferred_element_type=F32)

    def attend(jb, kv, s):
        n, (p_rows, o_rows, n_rows) = key_rows(jb)
        lc = ck_ref.shape[2]
        bias_prev = jnp.concatenate([band_prev + jnp.where(n == 0, NEG, 0.0)] * nhb, axis=1)
        bias_next = jnp.concatenate([band_next + jnp.where(n == nb_total - 1, NEG, 0.0)] * nhb, axis=1)
        parts = [s[0:lc], s[lc:lc + ATT_BLOCK] + bias_prev, s[lc + ATT_BLOCK:lc + 2 * ATT_BLOCK],
                 s[lc + 2 * ATT_BLOCK:] + bias_next]
        sink = sink_ref[kv:kv + 1, :] * LOG2E
        m = sink
        for t in parts:
            m = jnp.maximum(m, jnp.max(t, axis=0, keepdims=True))
        p = jnp.concatenate([jnp.exp2(t - m).astype(BF16) for t in parts], axis=0)
        vaug = jnp.concatenate([cv_ref[0, kv], v_ref[0, kv, p_rows, :], v_ref[0, kv, o_rows, :],
                                v_ref[0, kv, n_rows, :]], axis=0)
        o = lax.dot_general(vaug, p, tn, preferred_element_type=F32)
        if kv == 0:
            val, ones = slice(0, ATT_DH), slice(ATT_DH, ATT_DH + 1)
        else:
            val, ones = slice(ATT_DH, 2 * ATT_DH), slice(0, 1)
        den = o[ones] + jnp.exp2(sink - m)
        return o[val] * (1.0 / den)

    def gated_block(jb, halves):
        rows = slice(jb * ATT_BLOCK, (jb + 1) * ATT_BLOCK)
        comb = jnp.concatenate(halves, axis=0)
        return jnp.concatenate(
            [(comb[:, j * LANES:(j + 1) * LANES].T
              * g_ref[0, rows, j * LANES:(j + 1) * LANES].astype(F32)).astype(BF16) for j in range(nhb)], axis=1)

    def finish(jb, att_blks):
        rows = slice(jb * ATT_BLOCK, (jb + len(att_blks)) * ATT_BLOCK)
        mix_in = jnp.concatenate([r_ref[0, rows, :], jnp.concatenate(att_blks, axis=0)], axis=1)
        mixed = jnp.dot(mix_in, w_scr[...], preferred_element_type=F32)
        xn = x_ref[0, rows, :] + m_ref[0, :, 2 * D_MODEL:3 * D_MODEL] * mixed
        o_ref[0, rows, :] = xn * lax.rsqrt(jnp.mean(xn * xn, axis=-1, keepdims=True) + EPS) * fn_ref[...]

    chains = [(jb, kv) for jb in range(nb_step) for kv in range(ATT_KV)]
    s_next = scores(*chains[0])
    halves, done, pending = [], [], []
    for c, (jb, kv) in enumerate(chains):
        s = s_next
        if c + 1 < len(chains):
            s_next = scores(*chains[c + 1])
        halves.append(attend(jb, kv, s))
        if pending:
            finish(*pending.pop())
        if kv == ATT_KV - 1:
            done.append(gated_block(jb, halves))
            halves = []
            if len(done) == OUT_BLOCKS:
                pending.append((jb + 1 - OUT_BLOCKS, done))
                done = []
    finish(*pending.pop())


def _attention_output(aq, ak, av, ck, cv, ag, sink2, ret, x, mod3, w_out_b, fnw):
    B, L, _ = aq.shape
    lc = ck.shape[2]
    tq = ATT_ROWS
    row = lambda n: pl.BlockSpec((1, tq, n), lambda b, i: (b, i, 0))
    full = lambda n: pl.BlockSpec((1, ATT_KV, n, LANES), lambda b, i: (b, 0, 0, 0))
    return pl.pallas_call(
        _att_kernel,
        out_shape=jax.ShapeDtypeStruct((B, L, D_MODEL), F32),
        grid=(B, L // tq),
        in_specs=[row(ATT_W), full(L), full(L), full(lc), full(lc), row(ATT_W),
                  pl.BlockSpec(sink2.shape, lambda b, i: (0, 0)),
                  row(RET_W), row(D_MODEL),
                  pl.BlockSpec((1, 1, 3 * D_MODEL), lambda b, i: (b, 0, 0)),
                  pl.BlockSpec((RET_W + ATT_W, D_MODEL), lambda b, i: (0, 0)),
                  pl.BlockSpec((1, D_MODEL), lambda b, i: (0, 0))],
        out_specs=row(D_MODEL),
        scratch_shapes=[pltpu.VMEM((RET_W + ATT_W, D_MODEL), BF16)],
        compiler_params=pltpu.CompilerParams(
            dimension_semantics=("arbitrary", "arbitrary"), vmem_limit_bytes=VMEM_LIMIT),
        name="att",
    )(aq, ak, av, ck, cv, ag, sink2, ret, x, mod3, w_out_b, fnw)


def _rope_tables(L):
    pos = np.arange(L)
    rows, cols = (pos // GRID_W).astype(np.float64), (pos % GRID_W).astype(np.float64)

    def tables(dh):
        nf = dh // 4
        inv = ROPE_BASE ** (-np.arange(nf, dtype=np.float64) / nf)
        ang = np.concatenate([rows[:, None] * inv, cols[:, None] * inv], axis=-1)
        cos, sin = np.cos(ang), np.sin(ang)
        reps = LANES // dh
        return (jnp.asarray(np.tile(np.concatenate([cos, cos], axis=-1), (1, reps)), F32),
                jnp.asarray(np.tile(np.concatenate([-sin, sin], axis=-1), (1, reps)), F32))

    cr, sr = tables(RET_D)
    ca, sa = tables(ATT_DH)
    return cr, sr, ca, sa


def kernel(x, c, ctx, c_ctx, w_ada, b_ada, w_in, ret_decay_logit, ret_gn_w, att_sink, w_out, final_norm_w):
    B, L, _ = x.shape
    assert w_ada.shape[0] == 1, "single-layer trunk"

    cc = jnp.concatenate([c, c_ctx[None, :], jnp.zeros((B - 1, D_MODEL), F32)], axis=0)
    mod3 = _modulation(cc, w_ada[0], b_ada[0][None, :]).reshape(2 * B, 1, 3 * D_MODEL)

    w_in_b, w_out_b = w_in[0].astype(BF16), w_out[0].astype(BF16)
    dl =jnp.broadcast_to(ret_decay_logit[0].astype(F32)[:, :, None, None], (2, RET_HEADS, 8, LANES))

    r, rvt, aq, ak, av, ag = _project(x, mod3, w_in_b, _rope_tables(L))
    sf, sb, ck, cv = _context(ctx, mod3, w_in_b, dl)
    ret = _retention(r, rvt, sf, sb, dl, ret_gn_w[0].reshape(RET_HEADS, 1, RET_D))

    sink2 = jnp.repeat(att_sink[0].astype(F32).reshape(ATT_KV, ATT_HEADS // ATT_KV), LANES, axis=1)
    return _attention_output(aq, ak, av, ck, cv, ag, sink2, ret, x, mod3, w_out_b, final_norm_w[None, :])
```

```python
import numpy as np
import jax
import jax.numpy as jnp
from jax import lax
from jax.experimental import pallas as pl
from jax.experimental.pallas import tpu as pltpu

F32 = jnp.float32
BF16 = jnp.bfloat16

D_MODEL = 1024
GRID_W = 64
RET_HEADS = 4
RET_D = 128
RET_W = RET_HEADS * RET_D
ATT_HEADS = 8
ATT_KV = 2
ATT_DH = 64
ATT_W = ATT_HEADS * ATT_DH
ATT_BLOCK = 128
WINDOW = 128
ROPE_BASE = 10000.0
EPS = 1e-6
NEG = -1e30
LOG2E = 1.4426950408889634
IN_COLS = 4 * RET_W + 2 * ATT_W + 2 * ATT_KV * ATT_DH
OFF_RQ, OFF_RK, OFF_RV, OFF_RG = 0, RET_W, 2 * RET_W, 3 * RET_W
OFF_AQ = 4 * RET_W
OFF_AK = OFF_AQ + ATT_W
OFF_AV = OFF_AK + ATT_KV * ATT_DH
OFF_AG = OFF_AV + ATT_KV * ATT_DH

LANES = 128
SUBLANES = 8
VMEM_LIMIT = 48 * 1024 * 1024

PROJ_ROWS = 1024
PROJ_SUB = 256
RET_CHUNK = 256
RET_PAIR = 2
ATT_ROWS = 1024
ATT_AHEAD = 2


def _silu(t):
    return t / (1.0 + jnp.exp(-t))


def _log_sigmoid(t):
    return -(jnp.maximum(-t, 0.0) + jnp.log1p(jnp.exp(-jnp.abs(t))))


def _mod_kernel(c_ref, w_ref, b_ref, o_ref):
    s = _silu(c_ref[...])
    o_ref[...] = jnp.dot(s, w_ref[...], preferred_element_type=F32,
                         precision=lax.Precision.HIGHEST) + b_ref[...]


def _modulation(cc, w_ada, b_ada):
    n = w_ada.shape[1]
    bn = 512
    return pl.pallas_call(
        _mod_kernel,
        out_shape=jax.ShapeDtypeStruct((cc.shape[0], n), F32),
        grid=(n // bn,),
        in_specs=[pl.BlockSpec(cc.shape, lambda i: (0, 0)),
                  pl.BlockSpec((D_MODEL, bn), lambda i: (0, i)),
                  pl.BlockSpec((1, bn), lambda i: (0, i))],
        out_specs=pl.BlockSpec((cc.shape[0], bn), lambda i: (0, i)),
        compiler_params=pltpu.CompilerParams(dimension_semantics=("arbitrary",)),
        name="mod",
    )(cc, w_ada, b_ada)


def _norm_mod(x, m_ref):
    shift = m_ref[0, :, 0:D_MODEL]
    scale = m_ref[0, :, D_MODEL:2 * D_MODEL]
    h = x * lax.rsqrt(jnp.mean(x * x, axis=-1, keepdims=True) + EPS)
    return h * (1.0 + scale) + shift


def _rope_ret(t, cos, sin):
    return t * cos + pltpu.roll(t, 64, 1) * sin


def _rope_att(t, cos, sin, low_half):
    partner = jnp.where(low_half, pltpu.roll(t, 96, 1), pltpu.roll(t, 32, 1))
    return t * cos + partner * sin


def _store_kv_groups(k_ref, v_ref, rows, k, v):
    low = lax.broadcasted_iota(jnp.int32, k.shape, 1) < ATT_DH
    k_ref[0, 0, rows, :] = jnp.where(low, k, 0.0).astype(BF16)
    k_ref[0, 1, rows, :] = jnp.where(low, 0.0, k).astype(BF16)
    v_ref[0, 0, rows, :] = jnp.where(low, v, 1.0).astype(BF16)
    v_ref[0, 1, rows, :] = jnp.where(low, 1.0, v).astype(BF16)


def _pair_heads(nat):
    low = lax.broadcasted_iota(jnp.int32, nat[0].shape, 1) < ATT_DH
    out = []
    for j in range(len(nat)):
        a, b = nat[j // 2], nat[2 + j // 2]
        if j % 2 == 0:
            out.append(jnp.where(low, a, pltpu.roll(b, ATT_DH, 1)))
        else:
            out.append(jnp.where(low, pltpu.roll(a, ATT_DH, 1), b))
    return out


def _proj_kernel(x_ref, m_ref, w_ref, cr_ref, sr_ref, ca_ref, sa_ref,
                 r_ref, rvt_ref, aq_ref, ak_ref, av_ref, ag_ref, wq_scr, wg_scr):

    @pl.when((pl.program_id(0) == 0) & (pl.program_id(1) == 0))
    def _():
        for off, scr in ((OFF_AQ, wq_scr), (OFF_AG, wg_scr)):
            for r in range(0, D_MODEL, LANES):
                nat = [w_ref[r:r + LANES, off + j * LANES:off + (j + 1) * LANES].astype(F32)
                       for j in range(ATT_W // LANES)]
                for j, blk in enumerate(_pair_heads(nat)):
                    scr[r:r + LANES, j * LANES:(j + 1) * LANES] = blk.astype(BF16)

    for t in range(x_ref.shape[1] // PROJ_SUB):
        rows = slice(t * PROJ_SUB, (t + 1) * PROJ_SUB)
        hb = _norm_mod(x_ref[0, rows, :], m_ref).astype(BF16)
        cr, sr, ca, sa = cr_ref[rows, :], sr_ref[rows, :], ca_ref[rows, :], sa_ref[rows, :]
        low_half = (lax.broadcasted_iota(jnp.int32, ca.shape, 1) % ATT_DH) < (ATT_DH // 2)

        def mm(c0, c1):
            return jnp.dot(hb, w_ref[:, c0:c1], preferred_element_type=F32)

        y = mm(OFF_RQ, OFF_RK)
        for h in range(RET_HEADS):
            sl = slice(h * LANES, (h + 1) * LANES)
            r_ref[0, rows, OFF_RQ + h * LANES:OFF_RQ + (h + 1) * LANES] = _rope_ret(y[:, sl], cr, sr).astype(BF16)
        y = mm(OFF_RK, OFF_RV) * (RET_D ** -0.5)
        for h in range(RET_HEADS):
            sl = slice(h * LANES, (h + 1) * LANES)
            r_ref[0, rows, OFF_RK + h * LANES:OFF_RK + (h + 1) * LANES] = _rope_ret(y[:, sl], cr, sr).astype(BF16)
        y = mm(OFF_RV, OFF_RG)
        for h in range(RET_HEADS):
            sl = slice(h * LANES, (h + 1) * LANES)
            rvt_ref[0, sl, rows] = y[:, sl].T.astype(BF16)
        r_ref[0, rows, 2 * RET_W:3 * RET_W] = _silu(mm(OFF_RG, OFF_AQ)).astype(BF16)
        y = jnp.dot(hb, wq_scr[...], preferred_element_type=F32)
        for j in range(ATT_W // LANES):
            sl = slice(j * LANES, (j + 1) * LANES)
            aq_ref[0, rows, sl] = (_rope_att(y[:, sl], ca, sa, low_half) * (ATT_DH ** -0.5 * LOG2E)).astype(BF16)
        y = mm(OFF_AK, OFF_AG)
        _store_kv_groups(ak_ref, av_ref, rows, _rope_att(y[:, 0:LANES], ca, sa, low_half), y[:, LANES:2 * LANES])
        ag_ref[0, rows, :] = _silu(jnp.dot(hb, wg_scr[...], preferred_element_type=F32)).astype(BF16)


def _project(x, mod3, w_in_b, tabs):
    B, L, _ = x.shape
    tm = PROJ_ROWS
    tab_spec = pl.BlockSpec((tm, LANES), lambda b, i: (i, 0))
    row = lambda n: pl.BlockSpec((1, tm, n), lambda b, i: (b, i, 0))
    kv_spec = pl.BlockSpec((1, ATT_KV, tm, LANES), lambda b, i: (b, 0, i, 0))
    kv_shape = jax.ShapeDtypeStruct((B, ATT_KV, L, LANES), BF16)
    return pl.pallas_call(
        _proj_kernel,
        out_shape=(jax.ShapeDtypeStruct((B, L, 3 * RET_W), BF16),
                   jax.ShapeDtypeStruct((B, RET_W, L), BF16),
                   jax.ShapeDtypeStruct((B, L, ATT_W), BF16),
                   kv_shape, kv_shape,
                   jax.ShapeDtypeStruct((B, L, ATT_W), BF16)),
        grid=(B, L // tm),
        in_specs=[row(D_MODEL),
                  pl.BlockSpec((1, 1, 3 * D_MODEL), lambda b, i: (b, 0, 0)),
                  pl.BlockSpec((D_MODEL, IN_COLS), lambda b, i: (0, 0), pipeline_mode=pl.Buffered(1)),
                  tab_spec, tab_spec, tab_spec, tab_spec],
        out_specs=(row(3 * RET_W), pl.BlockSpec((1, RET_W, tm), lambda b, i: (b, 0, i)),
                   row(ATT_W), kv_spec, kv_spec, row(ATT_W)),
        scratch_shapes=[pltpu.VMEM((D_MODEL, ATT_W), BF16), pltpu.VMEM((D_MODEL, ATT_W), BF16)],
        compiler_params=pltpu.CompilerParams(
            dimension_semantics=("arbitrary", "arbitrary"), vmem_limit_bytes=VMEM_LIMIT),
        name="proj",
    )(x, mod3, w_in_b, *tabs)


def _ctx_kernel(x_ref, m_ref, wk_ref, wv_ref, wa_ref, dl_ref, sf_ref, sb_ref, ck_ref, cv_ref):
    lc = x_ref.shape[1]
    hb = _norm_mod(x_ref[0], m_ref).astype(BF16)
    yk = jnp.dot(hb, wk_ref[...], preferred_element_type=F32)
    yv = jnp.dot(hb, wv_ref[...], preferred_element_type=F32)
    ya = jnp.dot(hb, wa_ref[...], preferred_element_type=F32)
    _store_kv_groups(ck_ref, cv_ref, slice(0, lc), ya[:, 0:LANES], ya[:, LANES:2 * LANES])
    lg = _log_sigmoid(dl_ref[...])
    pos = lax.broadcasted_iota(jnp.int32, (lc, 1), 0).astype(F32)
    for h in range(RET_HEADS):
        k = yk[:, h * RET_D:(h + 1) * RET_D] * (RET_D ** -0.5)
        v = yv[:, h * RET_D:(h + 1) * RET_D].astype(BF16)
        wf = jnp.exp(lg[0, h, 0:1, :] * (lc - 1.0 - pos))
        wb = jnp.exp(lg[1, h, 0:1, :] * pos)
        dn = (((0,), (0,)), ((), ()))
        sf_ref[0, h] = lax.dot_general(v, (k * wf).astype(BF16), dn, preferred_element_type=F32)
        sb_ref[0, h] = lax.dot_general(v, (k * wb).astype(BF16), dn, preferred_element_type=F32)


def _context(ctx, mod3, w_in_b, dl):
    B, lc, _ = ctx.shape
    wcol = lambda off, n: pl.BlockSpec((D_MODEL, n), lambda b: (0, off // n))
    st = jax.ShapeDtypeStruct((B, RET_HEADS, RET_D, RET_D), F32)
    kv = jax.ShapeDtypeStruct((B, ATT_KV, lc, LANES), BF16)
    st_spec = pl.BlockSpec((1, RET_HEADS, RET_D, RET_D), lambda b: (b, 0, 0, 0))
    kv_spec = pl.BlockSpec((1, ATT_KV, lc, LANES), lambda b: (b, 0, 0, 0))
    ctx_row = mod3.shape[0] // 2
    return pl.pallas_call(
        _ctx_kernel,
        out_shape=(st, st, kv, kv),
        grid=(B,),
        in_specs=[pl.BlockSpec((1, lc, D_MODEL), lambda b: (b, 0, 0)),
                  pl.BlockSpec((1, 1, 3 * D_MODEL), lambda b: (ctx_row, 0, 0)),
                  wcol(OFF_RK, RET_W), wcol(OFF_RV, RET_W), wcol(OFF_AK, 2 * LANES),
                  pl.BlockSpec(dl.shape, lambda b: (0, 0, 0, 0))],
        out_specs=(st_spec, st_spec, kv_spec, kv_spec),
        compiler_params=pltpu.CompilerParams(
            dimension_semantics=("arbitrary",), vmem_limit_bytes=VMEM_LIMIT),
        name="ctx",
    )(ctx, mod3, w_in_b, w_in_b, w_in_b, dl)


def _ret_kernel(q_ref, k_ref, vt_ref, g_ref, sf_ref, sb_ref, dl_ref, gn_ref, o_ref,
                uf_scr, rb_scr, kdec_scr, qdec_scr, cdec_scr, decay_scr):
    L = q_ref.shape[1]
    C = RET_CHUNK
    nch = L // C
    heads = range(RET_PAIR)
    lanes = lambda h: slice(h * RET_D, (h + 1) * RET_D)

    @pl.when(pl.program_id(1) == 0)
    def _():
        lg = _log_sigmoid(dl_ref[...])
        pos = lax.broadcasted_iota(jnp.int32, (C, 1), 0).astype(F32)
        posl = lax.broadcasted_iota(jnp.int32, (SUBLANES, C), 1).astype(F32)
        diff = (lax.broadcasted_iota(jnp.int32, (C, C), 1)
                - lax.broadcasted_iota(jnp.int32, (C, C), 0)).astype(F32)
        for h in heads:
            lgf, lgb = lg[0, h, 0:1, :], lg[1, h, 0:1, :]
            lgf1, lgb1 = lgf[:, 0:1], lgb[:, 0:1]
            kdec_scr[h, 0] = jnp.exp(lgf * (C - 1.0 - pos))
            kdec_scr[h, 1] = jnp.exp(lgb * pos)
            qdec_scr[h, 0] = jnp.exp(lgf1 * (posl + 1.0))
            qdec_scr[h, 1] = jnp.exp(lgb1 * (C - posl))
            cdec_scr[h, 0] = jnp.broadcast_to(jnp.exp(lgf * C), (SUBLANES, LANES))
            cdec_scr[h, 1] = jnp.broadcast_to(jnp.exp(lgb * C), (SUBLANES, LANES))
            decay_scr[h] = jnp.where(diff >= 0.0, jnp.exp(lgf1 * jnp.maximum(diff, 0.0)),
                                     jnp.exp(lgb1 * jnp.maximum(-diff, 0.0)))

    nt = (((1,), (1,)), ((), ()))
    chunk = lambda n: slice(n * C, (n + 1) * C)

    rb = [sb_ref[0, h] for h in heads]
    for n in reversed(range(nch)):
        for h in heads:
            kc = k_ref[0, chunk(n), lanes(h)].astype(F32)
            kd = jnp.concatenate([(kc * kdec_scr[h, 0]).astype(BF16), (kc * kdec_scr[h, 1]).astype(BF16)],
                                 axis=1)
            u = jnp.dot(vt_ref[0, lanes(h), chunk(n)], kd, preferred_element_type=F32)
            uf_scr[h, n] = u[:, 0:RET_D]
            rb_scr[h, n] = rb[h].astype(BF16)
            rb[h] = rb[h] * cdec_scr[h, 1, 0:1, :] + u[:, RET_D:2 * RET_D]

    def kq(h, n):
        return lax.dot_general(k_ref[0, chunk(n), lanes(h)], q_ref[0, chunk(n), lanes(h)], nt,
                               preferred_element_type=F32)

    rf = [sf_ref[0, h] for h in heads]
    s_next = [kq(h, 0) for h in heads]
    for n in range(nch):
        for h in heads:
            s = s_next[h]
            q = q_ref[0, chunk(n), lanes(h)]
            states = jnp.concatenate([rf[h].astype(BF16), rb_scr[h, n]], axis=0)
            cross = lax.dot_general(states, q, nt, preferred_element_type=F32)
            if n + 1 < nch:
                s_next[h] = kq(h, n + 1)
                rf[h] = rf[h] * cdec_scr[h, 0, 0:1, :] + uf_scr[h, n]
            inner = jnp.dot(vt_ref[0, lanes(h), chunk(n)], (s * decay_scr[h]).astype(BF16),
                            preferred_element_type=F32)
            y = (inner + cross[0:RET_D] * qdec_scr[h, 0, 0:1, :]
                 + cross[RET_D:2 * RET_D] * qdec_scr[h, 1, 0:1, :])
            mu = jnp.mean(y, axis=0, keepdims=True)
            yc = y - mu
            var = jnp.mean(yc * yc, axis=0, keepdims=True)
            yn = (yc * lax.rsqrt(var + EPS)).T * gn_ref[h]
            o_ref[0, chunk(n), lanes(h)] = (yn * g_ref[0, chunk(n), lanes(h)].astype(F32)).astype(BF16)


def _retention(r, rvt, sf, sb, dl, gn):
    B, L, _ = r.shape
    P, C = RET_PAIR, RET_CHUNK
    col = lambda off: pl.BlockSpec((1, L, P * RET_D), lambda hp, b: (b, 0, off + hp))
    st_spec = pl.BlockSpec((1, P, RET_D, RET_D), lambda hp, b: (b, hp, 0, 0))
    npair = RET_HEADS // P
    return pl.pallas_call(
        _ret_kernel,
        out_shape=jax.ShapeDtypeStruct((B, L, RET_W), BF16),
        grid=(npair, B),
        in_specs=[col(0), col(npair),
                  pl.BlockSpec((1, P * RET_D, L), lambda hp, b: (b, hp, 0)),
                  col(2 * npair),
                  st_spec, st_spec,
                  pl.BlockSpec((2, P, SUBLANES, LANES), lambda hp, b: (0, hp, 0, 0)),
                  pl.BlockSpec((P, 1, RET_D), lambda hp, b: (hp, 0, 0))],
        out_specs=pl.BlockSpec((1, L, P * RET_D), lambda hp, b: (b, 0, hp)),
        scratch_shapes=[pltpu.VMEM((P, L // C, RET_D, RET_D), F32),
                        pltpu.VMEM((P, L // C, RET_D, RET_D), BF16),
                        pltpu.VMEM((P, 2, C, LANES), F32),
                        pltpu.VMEM((P, 2, SUBLANES, C), F32),
                        pltpu.VMEM((P, 2, SUBLANES, LANES), F32),
                        pltpu.VMEM((P, C, C), F32)],
        compiler_params=pltpu.CompilerParams(
            dimension_semantics=("arbitrary", "arbitrary"), vmem_limit_bytes=VMEM_LIMIT),
        name="ret",
    )(r, r, rvt, r, sf, sb, dl, gn)


def _att_kernel(q_ref, k_ref, v_ref, ck_ref, cv_ref, g_ref, sink_ref, r_ref, x_ref, m_ref, w_ref, fn_ref,
                o_ref, w_scr):
    @pl.when((pl.program_id(0) == 0) & (pl.program_id(1) == 0))
    def _():
        w_scr[0:RET_W, :] = w_ref[0:RET_W, :]
        for j in range(ATT_HEADS // ATT_KV):
            for half in range(ATT_KV):
                dst = RET_W + j * LANES + half * ATT_DH
                src = RET_W + (j + half * (ATT_HEADS // ATT_KV)) * ATT_DH
                w_scr[dst:dst + ATT_DH, :] = w_ref[src:src + ATT_DH, :]

    L = k_ref.shape[2]
    nb_total = L // ATT_BLOCK
    nb_step = q_ref.shape[1] // ATT_BLOCK
    nhb = ATT_W // LANES
    i = pl.program_id(1)
    r_i = lax.broadcasted_iota(jnp.int32, (ATT_BLOCK, ATT_BLOCK), 0)
    q_i = lax.broadcasted_iota(jnp.int32, (ATT_BLOCK, ATT_BLOCK), 1)
    band_prev = jnp.where(q_i <= r_i, 0.0, NEG).astype(F32)
    band_next = jnp.where(r_i <= q_i, 0.0, NEG).astype(F32)
    nt = (((1,), (1,)), ((), ()))
    tn = (((0,), (0,)), ((), ()))

    def key_rows(jb):
        n = i * nb_step + jb
        blk = lambda t: pl.ds(pl.multiple_of(t * ATT_BLOCK, ATT_BLOCK), ATT_BLOCK)
        return n, (blk(jnp.maximum(n - 1, 0)), blk(n), blk(jnp.minimum(n + 1, nb_total - 1)))

    def scores(jb, kv):
        n, (p_rows, o_rows, n_rows) = key_rows(jb)
        q = q_ref[0, jb * ATT_BLOCK:(jb + 1) * ATT_BLOCK, :]
        qall = jnp.concatenate([q[:, j * LANES:(j + 1) * LANES] for j in range(nhb)], axis=0)
        kall = jnp.concatenate([ck_ref[0, kv], k_ref[0, kv, p_rows, :], k_ref[0, kv, o_rows, :],
                                k_ref[0, kv, n_rows, :]], axis=0)
        return lax.dot_general(kall, qall, nt, preferred_element_type=F32)

    def attend(jb, kv, s):
        n, (p_rows, o_rows, n_rows) = key_rows(jb)
        lc = ck_ref.shape[2]
        bias_prev = jnp.concatenate([band_prev + jnp.where(n == 0, NEG, 0.0)] * nhb, axis=1)
        bias_next = jnp.concatenate([band_next + jnp.where(n == nb_total - 1, NEG, 0.0)] * nhb, axis=1)
        parts = [s[0:lc], s[lc:lc + ATT_BLOCK] + bias_prev, s[lc + ATT_BLOCK:lc + 2 * ATT_BLOCK],
                 s[lc + 2 * ATT_BLOCK:] + bias_next]
        sink = sink_ref[kv:kv + 1, :] * LOG2E
        m = sink
        for t in parts:
            m = jnp.maximum(m, jnp.max(t, axis=0, keepdims=True))
        p = jnp.concatenate([jnp.exp2(t - m).astype(BF16) for t in parts], axis=0)
        vaug = jnp.concatenate([cv_ref[0, kv], v_ref[0, kv, p_rows, :], v_ref[0, kv, o_rows, :],
                                v_ref[0, kv, n_rows, :]], axis=0)
        o = lax.dot_general(vaug, p, tn, preferred_element_type=F32)
        if kv == 0:
            val, ones = slice(0, ATT_DH), slice(ATT_DH, ATT_DH + 1)
        else:
            val, ones = slice(ATT_DH, 2 * ATT_DH), slice(0, 1)
        den = o[ones] + jnp.exp2(sink - m)
        return o[val] * (1.0 / den)

    def gated_block(jb, halves):
        rows = slice(jb * ATT_BLOCK, (jb + 1) * ATT_BLOCK)
        comb = jnp.concatenate(halves, axis=0)
        return jnp.concatenate(
            [(comb[:, j * LANES:(j + 1) * LANES].T
              * g_ref[0, rows, j * LANES:(j + 1) * LANES].astype(F32)).astype(BF16) for j in range(nhb)], axis=1)

    def finish(jb, att_blk):
        rows = slice(jb * ATT_BLOCK, (jb + 1) * ATT_BLOCK)
        mix_in = jnp.concatenate([r_ref[0, rows, :], att_blk], axis=1)
        mixed = jnp.dot(mix_in, w_scr[...], preferred_element_type=F32)
        xn = x_ref[0, rows, :] + m_ref[0, :, 2 * D_MODEL:3 * D_MODEL] * mixed
        o_ref[0, rows, :] = xn * lax.rsqrt(jnp.mean(xn * xn, axis=-1, keepdims=True) + EPS) * fn_ref[...]

    chains = [(jb, kv) for jb in range(nb_step) for kv in range(ATT_KV)]
    ahead = [scores(*chains[t]) for t in range(ATT_AHEAD)]
    halves, pending = [], []
    for c, (jb, kv) in enumerate(chains):
        s = ahead.pop(0)
        if c + ATT_AHEAD < len(chains):
            ahead.append(scores(*chains[c + ATT_AHEAD]))
        halves.append(attend(jb, kv, s))
        if pending:
            finish(*pending.pop())
        if kv == ATT_KV - 1:
            pending.append((jb, gated_block(jb, halves)))
            halves = []
    finish(*pending.pop())


def _attention_output(aq, ak, av, ck, cv, ag, sink2, ret, x, mod3, w_out_b, fnw):
    B, L, _ = aq.shape
    lc = ck.shape[2]
    tq = ATT_ROWS
    row = lambda n: pl.BlockSpec((1, tq, n), lambda b, i: (b, i, 0))
    full = lambda n: pl.BlockSpec((1, ATT_KV, n, LANES), lambda b, i: (b, 0, 0, 0))
    return pl.pallas_call(
        _att_kernel,
        out_shape=jax.ShapeDtypeStruct((B, L, D_MODEL), F32),
        grid=(B, L // tq),
        in_specs=[row(ATT_W), full(L), full(L), full(lc), full(lc), row(ATT_W),
                  pl.BlockSpec(sink2.shape, lambda b, i: (0, 0)),
                  row(RET_W), row(D_MODEL),
                  pl.BlockSpec((1, 1, 3 * D_MODEL), lambda b, i: (b, 0, 0)),
                  pl.BlockSpec((RET_W + ATT_W, D_MODEL), lambda b, i: (0, 0)),
                  pl.BlockSpec((1, D_MODEL), lambda b, i: (0, 0))],
        out_specs=row(D_MODEL),
        scratch_shapes=[pltpu.VMEM((RET_W + ATT_W, D_MODEL), BF16)],
        compiler_params=pltpu.CompilerParams(
            dimension_semantics=("arbitrary", "arbitrary"), vmem_limit_bytes=VMEM_LIMIT),
        name="att",
    )(aq, ak, av, ck, cv, ag, sink2, ret, x, mod3, w_out_b, fnw)


def _rope_tables(L):
    pos = np.arange(L)
    rows, cols = (pos // GRID_W).astype(np.float64), (pos % GRID_W).astype(np.float64)

    def tables(dh):
        nf = dh // 4
        inv = ROPE_BASE ** (-np.arange(nf, dtype=np.float64) / nf)
        ang = np.concatenate([rows[:, None] * inv, cols[:, None] * inv], axis=-1)
        cos, sin = np.cos(ang), np.sin(ang)
        reps = LANES // dh
        return (jnp.asarray(np.tile(np.concatenate([cos, cos], axis=-1), (1, reps)), F32),
                jnp.asarray(np.tile(np.concatenate([-sin, sin], axis=-1), (1, reps)), F32))

    cr, sr = tables(RET_D)
    ca, sa = tables(ATT_DH)
    return cr, sr, ca, sa


def kernel(x, c, ctx, c_ctx, w_ada, b_ada, w_in, ret_decay_logit, ret_gn_w, att_sink, w_out, final_norm_w):
    B, L, _ = x.shape
    assert w_ada.shape[0] == 1, "single-layer trunk"

    cc = jnp.concatenate([c, c_ctx[None, :], jnp.zeros((B - 1, D_MODEL), F32)], axis=0)
    mod3 = _modulation(cc, w_ada[0], b_ada[0][None, :]).reshape(2 * B, 1, 3 * D_MODEL)

    w_in_b, w_out_b = w_in[0].astype(BF16), w_out[0].astype(BF16)
    dl = jnp.broadcast_to(ret_decay_logit[0][:, :, None, None], (2, RET_HEADS, SUBLANES, LANES))

    r, rvt, aq, ak, av, ag = _project(x, mod3, w_in_b, _rope_tables(L))
    sf, sb, ck, cv = _context(ctx, mod3, w_in_b, dl)
    ret = _retention(r, rvt, sf, sb, dl, ret_gn_w[0].reshape(RET_HEADS, 1, RET_D))

    sink2 = jnp.repeat(att_sink[0].reshape(ATT_KV, ATT_HEADS // ATT_KV), LANES, axis=1)
    return _attention_output(aq, ak, av, ck, cv, ag, sink2, ret, x, mod3, w_out_b, final_norm_w[None, :])
```

```python
import numpy as np
import jax
import jax.numpy as jnp
from jax import lax
from jax.experimental import pallas as pl
from jax.experimental.pallas import tpu as pltpu

F32 = jnp.float32
BF16 = jnp.bfloat16

D_MODEL = 1024
GRID_W = 64
RET_HEADS = 4
RET_D = 128
RET_W = RET_HEADS * RET_D
ATT_HEADS = 8
ATT_KV = 2
ATT_DH = 64
ATT_W = ATT_HEADS * ATT_DH
ATT_BLOCK = 128
WINDOW = 128
ROPE_BASE = 10000.0
EPS = 1e-6
NEG = -1e30
LOG2E = 1.4426950408889634
IN_COLS = 4 * RET_W + 2 * ATT_W + 2 * ATT_KV * ATT_DH
OFF_RQ, OFF_RK, OFF_RV, OFF_RG = 0, RET_W, 2 * RET_W, 3 * RET_W
OFF_AQ = 4 * RET_W
OFF_AK = OFF_AQ + ATT_W
OFF_AV = OFF_AK + ATT_KV * ATT_DH
OFF_AG = OFF_AV + ATT_KV * ATT_DH

LANES = 128
SUBLANES = 8
VMEM_LIMIT = 48 * 1024 * 1024

PROJ_ROWS = 1024
PROJ_SUB = 256
RET_CHUNK = 256
RET_PAIR = 2
ATT_ROWS = 1024
ATT_AHEAD = 2


def _silu(t):
    return t / (1.0 + jnp.exp(-t))


def _log_sigmoid(t):
    return -(jnp.maximum(-t, 0.0) + jnp.log1p(jnp.exp(-jnp.abs(t))))


def _mod_kernel(c_ref, w_ref, b_ref, o_ref):
    n = c_ref.shape[0]
    s, w = _silu(c_ref[...]), w_ref[...]
    s_hi, w_hi = s.astype(BF16), w.astype(BF16)
    s_lo = (s - s_hi.astype(F32)).astype(BF16)
    w_lo = (w - w_hi.astype(F32)).astype(BF16)
    a = jnp.dot(jnp.concatenate([s_hi, s_lo], axis=0), w_hi, preferred_element_type=F32)
    o_ref[...] = a[0:n] + a[n:2 * n] + jnp.dot(s_hi, w_lo, preferred_element_type=F32) + b_ref[...]


def _modulation(cc, w_ada, b_ada):
    n = w_ada.shape[1]
    bn = 512
    return pl.pallas_call(
        _mod_kernel,
        out_shape=jax.ShapeDtypeStruct((cc.shape[0], n), F32),
        grid=(n // bn,),
        in_specs=[pl.BlockSpec(cc.shape, lambda i: (0, 0)),
                  pl.BlockSpec((D_MODEL, bn), lambda i: (0, i)),
                  pl.BlockSpec((1, bn), lambda i: (0, i))],
        out_specs=pl.BlockSpec((cc.shape[0], bn), lambda i: (0, i)),
        compiler_params=pltpu.CompilerParams(dimension_semantics=("arbitrary",)),
        name="mod",
    )(cc, w_ada, b_ada)


def _norm_mod(x, m_ref):
    shift = m_ref[0, :, 0:D_MODEL]
    scale = m_ref[0, :, D_MODEL:2 * D_MODEL]
    h = x * lax.rsqrt(jnp.mean(x * x, axis=-1, keepdims=True) + EPS)
    return h * (1.0 + scale) + shift


def _rope_ret(t, cos, sin):
    return t * cos + pltpu.roll(t, 64, 1) * sin


def _rope_att(t, cos, sin, low_half):
    partner = jnp.where(low_half, pltpu.roll(t, 96, 1), pltpu.roll(t, 32, 1))
    return t * cos + partner * sin


def _store_kv_groups(k_ref, v_ref, rows, k, v):
    low = lax.broadcasted_iota(jnp.int32, k.shape, 1) < ATT_DH
    k_ref[0, 0, rows, :] = jnp.where(low, k, 0.0).astype(BF16)
    k_ref[0, 1, rows, :] = jnp.where(low, 0.0, k).astype(BF16)
    v_ref[0, 0, rows, :] = jnp.where(low, v, 1.0).astype(BF16)
    v_ref[0, 1, rows, :] = jnp.where(low, 1.0, v).astype(BF16)


def _pair_heads(nat):
    low = lax.broadcasted_iota(jnp.int32, nat[0].shape, 1) < ATT_DH
    out = []
    for j in range(len(nat)):
        a, b = nat[j // 2], nat[2 + j // 2]
        if j % 2 == 0:
            out.append(jnp.where(low, a, pltpu.roll(b, ATT_DH, 1)))
        else:
            out.append(jnp.where(low, pltpu.roll(a, ATT_DH, 1), b))
    return out


def _proj_kernel(x_ref, m_ref, w_ref, cr_ref, sr_ref, ca_ref, sa_ref,
                 r_ref, rvt_ref, aq_ref, ak_ref, av_ref, ag_ref, wq_scr, wg_scr):

    @pl.when((pl.program_id(0) == 0) & (pl.program_id(1) == 0))
    def _():
        for off, scr in ((OFF_AQ, wq_scr), (OFF_AG, wg_scr)):
            for r in range(0, D_MODEL, LANES):
                nat = [w_ref[r:r + LANES, off + j * LANES:off + (j + 1) * LANES].astype(F32)
                       for j in range(ATT_W // LANES)]
                for j, blk in enumerate(_pair_heads(nat)):
                    scr[r:r + LANES, j * LANES:(j + 1) * LANES] = blk.astype(BF16)

    for t in range(x_ref.shape[1] // PROJ_SUB):
        rows = slice(t * PROJ_SUB, (t + 1) * PROJ_SUB)
        hb = _norm_mod(x_ref[0, rows, :], m_ref).astype(BF16)
        cr, sr, ca, sa = cr_ref[rows, :], sr_ref[rows, :], ca_ref[rows, :], sa_ref[rows, :]
        low_half = (lax.broadcasted_iota(jnp.int32, ca.shape, 1) % ATT_DH) < (ATT_DH // 2)

        def mm(c0, c1):
            return jnp.dot(hb, w_ref[:, c0:c1], preferred_element_type=F32)

        y = mm(OFF_RQ, OFF_RK)
        for h in range(RET_HEADS):
            sl = slice(h * LANES, (h + 1) * LANES)
            r_ref[0, rows, OFF_RQ + h * LANES:OFF_RQ + (h + 1) * LANES] = _rope_ret(y[:, sl], cr, sr).astype(BF16)
        y = mm(OFF_RK, OFF_RV) * (RET_D ** -0.5)
        for h in range(RET_HEADS):
            sl = slice(h * LANES, (h + 1) * LANES)
            r_ref[0, rows, OFF_RK + h * LANES:OFF_RK + (h + 1) * LANES] = _rope_ret(y[:, sl], cr, sr).astype(BF16)
        y = mm(OFF_RV, OFF_RG)
        for h in range(RET_HEADS):
            sl = slice(h * LANES, (h + 1) * LANES)
            rvt_ref[0, sl, rows] = y[:, sl].T.astype(BF16)
        r_ref[0, rows, 2 * RET_W:3 * RET_W] = _silu(mm(OFF_RG, OFF_AQ)).astype(BF16)
        y = jnp.dot(hb, wq_scr[...], preferred_element_type=F32)
        for j in range(ATT_W // LANES):
            sl = slice(j * LANES, (j + 1) * LANES)
            aq_ref[0, rows, sl] = (_rope_att(y[:, sl], ca, sa, low_half) * (ATT_DH ** -0.5 * LOG2E)).astype(BF16)
        y = mm(OFF_AK, OFF_AG)
        _store_kv_groups(ak_ref, av_ref, rows, _rope_att(y[:, 0:LANES], ca, sa, low_half), y[:, LANES:2 * LANES])
        ag_ref[0, rows, :] = _silu(jnp.dot(hb, wg_scr[...], preferred_element_type=F32)).astype(BF16)


def _project(x, mod3, w_in_b, tabs):
    B, L, _ = x.shape
    tm = PROJ_ROWS
    tab_spec = pl.BlockSpec((tm, LANES), lambda b, i: (i, 0))
    row = lambda n: pl.BlockSpec((1, tm, n), lambda b, i: (b, i, 0))
    kv_spec = pl.BlockSpec((1, ATT_KV, tm, LANES), lambda b, i: (b, 0, i, 0))
    kv_shape = jax.ShapeDtypeStruct((B, ATT_KV, L, LANES), BF16)
    return pl.pallas_call(
        _proj_kernel,
        out_shape=(jax.ShapeDtypeStruct((B, L, 3 * RET_W), BF16),
                   jax.ShapeDtypeStruct((B, RET_W, L), BF16),
                   jax.ShapeDtypeStruct((B, L, ATT_W), BF16),
                   kv_shape, kv_shape,
                   jax.ShapeDtypeStruct((B, L, ATT_W), BF16)),
        grid=(B, L // tm),
        in_specs=[row(D_MODEL),
                  pl.BlockSpec((1, 1, 3 * D_MODEL), lambda b, i: (b, 0, 0)),
                  pl.BlockSpec((D_MODEL, IN_COLS), lambda b, i: (0, 0), pipeline_mode=pl.Buffered(1)),
                  tab_spec, tab_spec, tab_spec, tab_spec],
        out_specs=(row(3 * RET_W), pl.BlockSpec((1, RET_W, tm), lambda b, i: (b, 0, i)),
                   row(ATT_W), kv_spec, kv_spec, row(ATT_W)),
        scratch_shapes=[pltpu.VMEM((D_MODEL, ATT_W), BF16), pltpu.VMEM((D_MODEL, ATT_W), BF16)],
        compiler_params=pltpu.CompilerParams(
            dimension_semantics=("arbitrary", "arbitrary"), vmem_limit_bytes=VMEM_LIMIT),
        name="proj",
    )(x, mod3, w_in_b, *tabs)


def _ctx_kernel(x_ref, m_ref, wk_ref, wv_ref, wa_ref, dl_ref, sf_ref, sb_ref, ck_ref, cv_ref):
    lc = x_ref.shape[1]
    hb = _norm_mod(x_ref[0], m_ref).astype(BF16)
    yk = jnp.dot(hb, wk_ref[...], preferred_element_type=F32)
    yv = jnp.dot(hb, wv_ref[...], preferred_element_type=F32)
    ya = jnp.dot(hb, wa_ref[...], preferred_element_type=F32)
    _store_kv_groups(ck_ref, cv_ref, slice(0, lc), ya[:, 0:LANES], ya[:, LANES:2 * LANES])
    lg = _log_sigmoid(dl_ref[...])
    pos = lax.broadcasted_iota(jnp.int32, (lc, 1), 0).astype(F32)
    for h in range(RET_HEADS):
        k = yk[:, h * RET_D:(h + 1) * RET_D] * (RET_D ** -0.5)
        v = yv[:, h * RET_D:(h + 1) * RET_D].astype(BF16)
        wf = jnp.exp(lg[0, h, 0:1, :] * (lc - 1.0 - pos))
        wb = jnp.exp(lg[1, h, 0:1, :] * pos)
        dn = (((0,), (0,)), ((), ()))
        sf_ref[0, h] = lax.dot_general(v, (k * wf).astype(BF16), dn, preferred_element_type=F32)
        sb_ref[0, h] = lax.dot_general(v, (k * wb).astype(BF16), dn, preferred_element_type=F32)


def _context(ctx, mod3, w_in_b, dl):
    B, lc, _ = ctx.shape
    wcol = lambda off, n: pl.BlockSpec((D_MODEL, n), lambda b: (0, off // n))
    st = jax.ShapeDtypeStruct((B, RET_HEADS, RET_D, RET_D), F32)
    kv = jax.ShapeDtypeStruct((B, ATT_KV, lc, LANES), BF16)
    st_spec = pl.BlockSpec((1, RET_HEADS, RET_D, RET_D), lambda b: (b, 0, 0, 0))
    kv_spec = pl.BlockSpec((1, ATT_KV, lc, LANES), lambda b: (b, 0, 0, 0))
    ctx_row = mod3.shape[0] // 2
    return pl.pallas_call(
        _ctx_kernel,
        out_shape=(st, st, kv, kv),
        grid=(B,),
        in_specs=[pl.BlockSpec((1, lc, D_MODEL), lambda b: (b, 0, 0)),
                  pl.BlockSpec((1, 1, 3 * D_MODEL), lambda b: (ctx_row, 0, 0)),
                  wcol(OFF_RK, RET_W), wcol(OFF_RV, RET_W), wcol(OFF_AK, 2 * LANES),
                  pl.BlockSpec(dl.shape, lambda b: (0, 0, 0, 0))],
        out_specs=(st_spec, st_spec, kv_spec, kv_spec),
        compiler_params=pltpu.CompilerParams(
            dimension_semantics=("arbitrary",), vmem_limit_bytes=VMEM_LIMIT),
        name="ctx",
    )(ctx, mod3, w_in_b, w_in_b, w_in_b, dl)


def _ret_kernel(q_ref, k_ref, vt_ref, g_ref, sf_ref, sb_ref, dl_ref, gn_ref, o_ref,
                uf_scr, rb_scr, kdec_scr, qdec_scr, cdec_scr, decay_scr):
    L = q_ref.shape[1]
    C = RET_CHUNK
    nch = L // C
    heads = range(RET_PAIR)
    lanes = lambda h: slice(h * RET_D, (h + 1) * RET_D)

    @pl.when(pl.program_id(1) == 0)
    def _():
        lg = _log_sigmoid(dl_ref[...])
        pos = lax.broadcasted_iota(jnp.int32, (C, 1), 0).astype(F32)
        posl = lax.broadcasted_iota(jnp.int32, (SUBLANES, C), 1).astype(F32)
        diff = (lax.broadcasted_iota(jnp.int32, (C, C), 1)
                - lax.broadcasted_iota(jnp.int32, (C, C), 0)).astype(F32)
        for h in heads:
            lgf, lgb = lg[0, h, 0:1, :], lg[1, h, 0:1, :]
            lgf1, lgb1 = lgf[:, 0:1], lgb[:, 0:1]
            kdec_scr[h, 0] = jnp.exp(lgf * (C - 1.0 - pos))
            kdec_scr[h, 1] = jnp.exp(lgb * pos)
            qdec_scr[h, 0] = jnp.exp(lgf1 * (posl + 1.0))
            qdec_scr[h, 1] = jnp.exp(lgb1 * (C - posl))
            cdec_scr[h, 0] = jnp.broadcast_to(jnp.exp(lgf * C), (SUBLANES, LANES))
            cdec_scr[h, 1] = jnp.broadcast_to(jnp.exp(lgb * C), (SUBLANES, LANES))
            decay_scr[h] = jnp.where(diff >= 0.0, jnp.exp(lgf1 * jnp.maximum(diff, 0.0)),
                                     jnp.exp(lgb1 * jnp.maximum(-diff, 0.0)))

    nt = (((1,), (1,)), ((), ()))
    chunk = lambda n: slice(n * C, (n + 1) * C)

    rb = [sb_ref[0, h] for h in heads]
    for n in reversed(range(nch)):
        for h in heads:
            kc = k_ref[0, chunk(n), lanes(h)].astype(F32)
            kd = jnp.concatenate([(kc * kdec_scr[h, 0]).astype(BF16), (kc * kdec_scr[h, 1]).astype(BF16)],
                                 axis=1)
            u = jnp.dot(vt_ref[0, lanes(h), chunk(n)], kd, preferred_element_type=F32)
            uf_scr[h, n] = u[:, 0:RET_D]
            rb_scr[h, n] = rb[h].astype(BF16)
            rb[h] = rb[h] * cdec_scr[h, 1, 0:1, :] + u[:, RET_D:2 * RET_D]

    def kq(h, n):
        return lax.dot_general(k_ref[0, chunk(n), lanes(h)], q_ref[0, chunk(n), lanes(h)], nt,
                               preferred_element_type=F32)

    rf = [sf_ref[0, h] for h in heads]
    s_next = [kq(h, 0) for h in heads]
    for n in range(nch):
        for h in heads:
            s = s_next[h]
            q = q_ref[0, chunk(n), lanes(h)]
            states = jnp.concatenate([rf[h].astype(BF16), rb_scr[h, n]], axis=0)
            cross = lax.dot_general(states, q, nt, preferred_element_type=F32)
            if n + 1 < nch:
                s_next[h] = kq(h, n + 1)
                rf[h] = rf[h] * cdec_scr[h, 0, 0:1, :] + uf_scr[h, n]
            inner = jnp.dot(vt_ref[0, lanes(h), chunk(n)], (s * decay_scr[h]).astype(BF16),
                            preferred_element_type=F32)
            y = (inner + cross[0:RET_D] * qdec_scr[h, 0, 0:1, :]
                 + cross[RET_D:2 * RET_D] * qdec_scr[h, 1, 0:1, :])
            mu = jnp.mean(y, axis=0, keepdims=True)
            yc = y - mu
            var = jnp.mean(yc * yc, axis=0, keepdims=True)
            yn = (yc * lax.rsqrt(var + EPS)).T * gn_ref[h]
            o_ref[0, chunk(n), lanes(h)] = (yn * g_ref[0, chunk(n), lanes(h)].astype(F32)).astype(BF16)


def _retention(r, rvt, sf, sb, dl, gn):
    B, L, _ = r.shape
    P, C = RET_PAIR, RET_CHUNK
    col = lambda off: pl.BlockSpec((1, L, P * RET_D), lambda hp, b: (b, 0, off + hp))
    st_spec = pl.BlockSpec((1, P, RET_D, RET_D), lambda hp, b: (b, hp, 0, 0))
    npair = RET_HEADS // P
    return pl.pallas_call(
        _ret_kernel,
        out_shape=jax.ShapeDtypeStruct((B, L, RET_W), BF16),
        grid=(npair, B),
        in_specs=[col(0), col(npair),
                  pl.BlockSpec((1, P * RET_D, L), lambda hp, b: (b, hp, 0)),
                  col(2 * npair),
                  st_spec, st_spec,
                  pl.BlockSpec((2, P, SUBLANES, LANES), lambda hp, b: (0, hp, 0, 0)),
                  pl.BlockSpec((P, 1, RET_D), lambda hp, b: (hp, 0, 0))],
        out_specs=pl.BlockSpec((1, L, P * RET_D), lambda hp, b: (b, 0, hp)),
        scratch_shapes=[pltpu.VMEM((P, L // C, RET_D, RET_D), F32),
                        pltpu.VMEM((P, L // C, RET_D, RET_D), BF16),
                        pltpu.VMEM((P, 2, C, LANES), F32),
                        pltpu.VMEM((P, 2, SUBLANES, C), F32),
                        pltpu.VMEM((P, 2, SUBLANES, LANES), F32),
                        pltpu.VMEM((P, C, C), F32)],
        compiler_params=pltpu.CompilerParams(
            dimension_semantics=("arbitrary", "arbitrary"), vmem_limit_bytes=VMEM_LIMIT),
        name="ret",
    )(r, r, rvt, r, sf, sb, dl, gn)


def _att_kernel(q_ref, k_ref, v_ref, ck_ref, cv_ref, g_ref, sink_ref, r_ref, x_ref, m_ref, w_ref, fn_ref,
                o_ref, w_scr):
    @pl.when((pl.program_id(0) == 0) & (pl.program_id(1) == 0))
    def _():
        w_scr[0:RET_W, :] = w_ref[0:RET_W, :]
        for j in range(ATT_HEADS // ATT_KV):
            for half in range(ATT_KV):
                dst = RET_W + j * LANES + half * ATT_DH
                src = RET_W + (j + half * (ATT_HEADS // ATT_KV)) * ATT_DH
                w_scr[dst:dst + ATT_DH, :] = w_ref[src:src + ATT_DH, :]

    L = k_ref.shape[2]
    nb_total = L // ATT_BLOCK
    nb_step = q_ref.shape[1] // ATT_BLOCK
    nhb = ATT_W // LANES
    i = pl.program_id(1)
    r_i = lax.broadcasted_iota(jnp.int32, (ATT_BLOCK, ATT_BLOCK), 0)
    q_i = lax.broadcasted_iota(jnp.int32, (ATT_BLOCK, ATT_BLOCK), 1)
    band_prev = jnp.where(q_i <= r_i, 0.0, NEG).astype(F32)
    band_next = jnp.where(r_i <= q_i, 0.0, NEG).astype(F32)
    nt = (((1,), (1,)), ((), ()))
    tn = (((0,), (0,)), ((), ()))

    def key_rows(jb):
        n = i * nb_step + jb
        blk = lambda t: pl.ds(pl.multiple_of(t * ATT_BLOCK, ATT_BLOCK), ATT_BLOCK)
        return n, (blk(jnp.maximum(n - 1, 0)), blk(n), blk(jnp.minimum(n + 1, nb_total - 1)))

    def scores(jb, kv):
        n, (p_rows, o_rows, n_rows) = key_rows(jb)
        q = q_ref[0, jb * ATT_BLOCK:(jb + 1) * ATT_BLOCK, :]
        qall = jnp.concatenate([q[:, j * LANES:(j + 1) * LANES] for j in range(nhb)], axis=0)
        kall = jnp.concatenate([ck_ref[0, kv], k_ref[0, kv, p_rows, :], k_ref[0, kv, o_rows, :],
                                k_ref[0, kv, n_rows, :]], axis=0)
        return lax.dot_general(kall, qall, nt, preferred_element_type=F32)

    def attend(jb, kv, s):
        n, (p_rows, o_rows, n_rows) = key_rows(jb)
        lc = ck_ref.shape[2]
        bias_prev = jnp.concatenate([band_prev + jnp.where(n == 0, NEG, 0.0)] * nhb, axis=1)
        bias_next = jnp.concatenate([band_next + jnp.where(n == nb_total - 1, NEG, 0.0)] * nhb, axis=1)
        parts = [s[0:lc], s[lc:lc + ATT_BLOCK] + bias_prev, s[lc + ATT_BLOCK:lc + 2 * ATT_BLOCK],
                 s[lc + 2 * ATT_BLOCK:] + bias_next]
        sink = sink_ref[kv:kv + 1, :] * LOG2E
        m = sink
        for t in parts:
            m = jnp.maximum(m, jnp.max(t, axis=0, keepdims=True))
        p = jnp.concatenate([jnp.exp2(t - m).astype(BF16) for t in parts], axis=0)
        vaug = jnp.concatenate([cv_ref[0, kv], v_ref[0, kv, p_rows, :], v_ref[0, kv, o_rows, :],
                                v_ref[0, kv, n_rows, :]], axis=0)
        o = lax.dot_general(vaug, p, tn, preferred_element_type=F32)
        if kv == 0:
            val, ones = slice(0, ATT_DH), slice(ATT_DH, ATT_DH + 1)
        else:
            val, ones = slice(ATT_DH, 2 * ATT_DH), slice(0, 1)
        den = o[ones] + jnp.exp2(sink - m)
        return o[val] * (1.0 / den)

    def gated_block(jb, halves):
        rows = slice(jb * ATT_BLOCK, (jb + 1) * ATT_BLOCK)
        comb = jnp.concatenate(halves, axis=0)
        return jnp.concatenate(
            [(comb[:, j * LANES:(j + 1) * LANES].T
              * g_ref[0, rows, j * LANES:(j + 1) * LANES].astype(F32)).astype(BF16) for j in range(nhb)], axis=1)

    def finish(jb, att_blk):
        rows = slice(jb * ATT_BLOCK, (jb + 1) * ATT_BLOCK)
        mix_in = jnp.concatenate([r_ref[0, rows, :], att_blk], axis=1)
        mixed = jnp.dot(mix_in, w_scr[...], preferred_element_type=F32)
        xn = x_ref[0, rows, :] + m_ref[0, :, 2 * D_MODEL:3 * D_MODEL] * mixed
        o_ref[0, rows, :] = xn * lax.rsqrt(jnp.mean(xn * xn, axis=-1, keepdims=True) + EPS) * fn_ref[...]

    chains = [(jb, kv) for jb in range(nb_step) for kv in range(ATT_KV)]
    ahead = [scores(*chains[t]) for t in range(ATT_AHEAD)]
    halves, pending = [], []
    for c, (jb, kv) in enumerate(chains):
        s = ahead.pop(0)
        if c + ATT_AHEAD < len(chains):
            ahead.append(scores(*chains[c + ATT_AHEAD]))
        halves.append(attend(jb, kv, s))
        if pending:
            finish(*pending.pop())
        if kv == ATT_KV - 1:
            pending.append((jb, gated_block(jb, halves)))
            halves = []
    finish(*pending.pop())


def _attention_output(aq, ak, av, ck, cv, ag, sink2, ret, x, mod3, w_out_b, fnw):
    B, L, _ = aq.shape
    lc = ck.shape[2]
    tq = ATT_ROWS
    row = lambda n: pl.BlockSpec((1, tq, n), lambda b, i: (b, i, 0))
    full = lambda n: pl.BlockSpec((1, ATT_KV, n, LANES), lambda b, i: (b, 0, 0, 0))
    return pl.pallas_call(
        _att_kernel,
        out_shape=jax.ShapeDtypeStruct((B, L, D_MODEL), F32),
        grid=(B, L // tq),
        in_specs=[row(ATT_W), full(L), full(L), full(lc), full(lc), row(ATT_W),
                  pl.BlockSpec(sink2.shape, lambda b, i: (0, 0)),
                  row(RET_W), row(D_MODEL),
                  pl.BlockSpec((1, 1, 3 * D_MODEL), lambda b, i: (b, 0, 0)),
                  pl.BlockSpec((RET_W + ATT_W, D_MODEL), lambda b, i: (0, 0)),
                  pl.BlockSpec((1, D_MODEL), lambda b, i: (0, 0))],
        out_specs=row(D_MODEL),
        scratch_shapes=[pltpu.VMEM((RET_W + ATT_W, D_MODEL), BF16)],
        compiler_params=pltpu.CompilerParams(
            dimension_semantics=("arbitrary", "arbitrary"), vmem_limit_bytes=VMEM_LIMIT),
        name="att",
    )(aq, ak, av, ck, cv, ag, sink2, ret, x, mod3, w_out_b, fnw)


def _rope_tables(L):
    pos = np.arange(L)
    rows, cols = (pos // GRID_W).astype(np.float64), (pos % GRID_W).astype(np.float64)

    def tables(dh):
        nf = dh // 4
        inv = ROPE_BASE ** (-np.arange(nf, dtype=np.float64) / nf)
        ang = np.concatenate([rows[:, None] * inv, cols[:, None] * inv], axis=-1)
        cos, sin = np.cos(ang), np.sin(ang)
        reps = LANES // dh
        return (jnp.asarray(np.tile(np.concatenate([cos, cos], axis=-1), (1, reps)), F32),
                jnp.asarray(np.tile(np.concatenate([-sin, sin], axis=-1), (1, reps)), F32))

    cr, sr = tables(RET_D)
    ca, sa = tables(ATT_DH)
    return cr, sr, ca, sa


def kernel(x, c, ctx, c_ctx, w_ada, b_ada, w_in, ret_decay_logit, ret_gn_w, att_sink, w_out, final_norm_w):
    B, L, _ = x.shape
    assert w_ada.shape[0] == 1, "single-layer trunk"

    cc = jnp.concatenate([c, c_ctx[None, :], jnp.zeros((B - 1, D_MODEL), F32)], axis=0)
    mod3 = _modulation(cc, w_ada[0], b_ada[0][None, :]).reshape(2 * B, 1, 3 * D_MODEL)

    w_in_b, w_out_b = w_in[0].astype(BF16), w_out[0].astype(BF16)
    dl = jnp.broadcast_to(ret_decay_logit[0][:, :, None, None], (2, RET_HEADS, SUBLANES, LANES))

    r, rvt, aq, ak, av, ag = _project(x, mod3, w_in_b, _rope_tables(L))
    sf, sb, ck, cv = _context(ctx, mod3, w_in_b, dl)
    ret = _retention(r, rvt, sf, sb, dl, ret_gn_w[0].reshape(RET_HEADS, 1, RET_D))

    sink2 = jnp.repeat(att_sink[0].reshape(ATT_KV, ATT_HEADS // ATT_KV), LANES, axis=1)
    return _attention_output(aq, ak, av, ck, cv, ag, sink2, ret, x, mod3, w_out_b, final_norm_w[None, :])
```

```python
import numpy as np
import jax
import jax.numpy as jnp
from jax import lax
from jax.experimental import pallas as pl
from jax.experimental.pallas import tpu as pltpu

F32 = jnp.float32
BF16 = jnp.bfloat16

D_MODEL = 1024
GRID_W = 64
RET_HEADS = 4
RET_D = 128
RET_W = RET_HEADS * RET_D
ATT_HEADS = 8
ATT_KV = 2
ATT_DH = 64
ATT_W = ATT_HEADS * ATT_DH
ATT_BLOCK = 128
WINDOW = 128
ROPE_BASE = 10000.0
EPS = 1e-6
NEG = -1e30
LOG2E = 1.4426950408889634
IN_COLS = 4 * RET_W + 2 * ATT_W + 2 * ATT_KV * ATT_DH
OFF_RQ, OFF_RK, OFF_RV, OFF_RG = 0, RET_W, 2 * RET_W, 3 * RET_W
OFF_AQ = 4 * RET_W
OFF_AK = OFF_AQ + ATT_W
OFF_AV = OFF_AK + ATT_KV * ATT_DH
OFF_AG = OFF_AV + ATT_KV * ATT_DH

LANES = 128
SUBLANES = 8
VMEM_LIMIT = 48 * 1024 * 1024

PROJ_ROWS = 1024
PROJ_SUB = 256
RET_CHUNK = 256
RET_PAIR = 2
ATT_ROWS = 1024
ATT_AHEAD = 2


def _silu(t):
    return t / (1.0 + jnp.exp(-t))


def _log_sigmoid(t):
    return -(jnp.maximum(-t, 0.0) + jnp.log1p(jnp.exp(-jnp.abs(t))))


def _mod_kernel(c_ref, w_ref, b_ref, o_ref):
    n = c_ref.shape[0]
    s, w = _silu(c_ref[...]), w_ref[...]
    s_hi, w_hi = s.astype(BF16), w.astype(BF16)
    s_lo = (s - s_hi.astype(F32)).astype(BF16)
    w_lo = (w - w_hi.astype(F32)).astype(BF16)
    a = jnp.dot(jnp.concatenate([s_hi, s_lo], axis=0), w_hi, preferred_element_type=F32)
    o_ref[...] = a[0:n] + a[n:2 * n] + jnp.dot(s_hi, w_lo, preferred_element_type=F32) + b_ref[...]


def _modulation(cc, w_ada, b_ada):
    n = w_ada.shape[1]
    bn = 512
    return pl.pallas_call(
        _mod_kernel,
        out_shape=jax.ShapeDtypeStruct((cc.shape[0], n), F32),
        grid=(n // bn,),
        in_specs=[pl.BlockSpec(cc.shape, lambda i: (0, 0)),
                  pl.BlockSpec((D_MODEL, bn), lambda i: (0, i)),
                  pl.BlockSpec((1, bn), lambda i: (0, i))],
        out_specs=pl.BlockSpec((cc.shape[0], bn), lambda i: (0, i)),
        compiler_params=pltpu.CompilerParams(dimension_semantics=("arbitrary",)),
        name="mod",
    )(cc, w_ada, b_ada)


def _norm_mod(x, m_ref):
    shift = m_ref[0, :, 0:D_MODEL]
    scale = m_ref[0, :, D_MODEL:2 * D_MODEL]
    h = x * lax.rsqrt(jnp.mean(x * x, axis=-1, keepdims=True) + EPS)
    return h * (1.0 + scale) + shift


def _rope_ret(t, cos, sin):
    return t * cos + pltpu.roll(t, 64, 1) * sin


def _rope_att(t, cos, sin, low_half):
    partner = jnp.where(low_half, pltpu.roll(t, 96, 1), pltpu.roll(t, 32, 1))
    return t * cos + partner * sin


def _store_kv_groups(k_ref, v_ref, rows, k, v):
    low = lax.broadcasted_iota(jnp.int32, k.shape, 1) < ATT_DH
    k_ref[0, 0, rows, :] = jnp.where(low, k, 0.0).astype(BF16)
    k_ref[0, 1, rows, :] = jnp.where(low, 0.0, k).astype(BF16)
    v_ref[0, 0, rows, :] = jnp.where(low, v, 1.0).astype(BF16)
    v_ref[0, 1, rows, :] = jnp.where(low, 1.0, v).astype(BF16)


def _pair_heads(nat):
    low = lax.broadcasted_iota(jnp.int32, nat[0].shape, 1) < ATT_DH
    out = []
    for j in range(len(nat)):
        a, b = nat[j // 2], nat[2 + j // 2]
        if j % 2 == 0:
            out.append(jnp.where(low, a, pltpu.roll(b, ATT_DH, 1)))
        else:
            out.append(jnp.where(low, pltpu.roll(a, ATT_DH, 1), b))
    return out


def _context_block(x_ref, m_ref, w_ref, dl_ref, sf_ref, sb_ref, ck_ref, cv_ref):
    lc = x_ref.shape[1]
    hb = _norm_mod(x_ref[0], m_ref).astype(BF16)
    yk = jnp.dot(hb, w_ref[:, OFF_RK:OFF_RV], preferred_element_type=F32)
    yv = jnp.dot(hb, w_ref[:, OFF_RV:OFF_RG], preferred_element_type=F32)
    ya = jnp.dot(hb, w_ref[:, OFF_AK:OFF_AG], preferred_element_type=F32)
    _store_kv_groups(ck_ref, cv_ref, slice(0, lc), ya[:, 0:LANES], ya[:, LANES:2 * LANES])
    lg = _log_sigmoid(dl_ref[...])
    pos = lax.broadcasted_iota(jnp.int32, (lc, 1), 0).astype(F32)
    dn = (((0,), (0,)), ((), ()))
    for h in range(RET_HEADS):
        k = yk[:, h * RET_D:(h + 1) * RET_D] * (RET_D ** -0.5)
        v = yv[:, h * RET_D:(h + 1) * RET_D].astype(BF16)
        wf = jnp.exp(lg[0, h, 0:1, :] * (lc - 1.0 - pos))
        wb = jnp.exp(lg[1, h, 0:1, :] * pos)
        sf_ref[0, h] = lax.dot_general(v, (k * wf).astype(BF16), dn, preferred_element_type=F32)
        sb_ref[0, h] = lax.dot_general(v, (k * wb).astype(BF16), dn, preferred_element_type=F32)


def _proj_kernel(x_ref, m_ref, w_ref, cr_ref, sr_ref, ca_ref, sa_ref, cx_ref, mc_ref, dl_ref,
                 r_ref, rvt_ref, aq_ref, ak_ref, av_ref, ag_ref, sf_ref, sb_ref, ck_ref, cv_ref,
                 wq_scr, wg_scr):

    @pl.when(pl.program_id(1) == 0)
    def _():
        _context_block(cx_ref, mc_ref, w_ref, dl_ref, sf_ref, sb_ref, ck_ref, cv_ref)

    @pl.when((pl.program_id(0) == 0) & (pl.program_id(1) == 0))
    def _():
        for off, scr in ((OFF_AQ, wq_scr), (OFF_AG, wg_scr)):
            for r in range(0, D_MODEL, LANES):
                nat = [w_ref[r:r + LANES, off + j * LANES:off + (j + 1) * LANES].astype(F32)
                       for j in range(ATT_W // LANES)]
                for j, blk in enumerate(_pair_heads(nat)):
                    scr[r:r + LANES, j * LANES:(j + 1) * LANES] = blk.astype(BF16)

    for t in range(x_ref.shape[1] // PROJ_SUB):
        rows = slice(t * PROJ_SUB, (t + 1) * PROJ_SUB)
        hb = _norm_mod(x_ref[0, rows, :], m_ref).astype(BF16)
        cr, sr, ca, sa = cr_ref[rows, :], sr_ref[rows, :], ca_ref[rows, :], sa_ref[rows, :]
        low_half = (lax.broadcasted_iota(jnp.int32, ca.shape, 1) % ATT_DH) < (ATT_DH // 2)

        def mm(c0, c1):
            return jnp.dot(hb, w_ref[:, c0:c1], preferred_element_type=F32)

        y = mm(OFF_RQ, OFF_RK)
        for h in range(RET_HEADS):
            sl = slice(h * LANES, (h + 1) * LANES)
            r_ref[0, rows, OFF_RQ + h * LANES:OFF_RQ + (h + 1) * LANES] = _rope_ret(y[:, sl], cr, sr).astype(BF16)
        y = mm(OFF_RK, OFF_RV) * (RET_D ** -0.5)
        for h in range(RET_HEADS):
            sl = slice(h * LANES, (h + 1) * LANES)
            r_ref[0, rows, OFF_RK + h * LANES:OFF_RK + (h + 1) * LANES] = _rope_ret(y[:, sl], cr, sr).astype(BF16)
        y = mm(OFF_RV, OFF_RG)
        for h in range(RET_HEADS):
            sl = slice(h * LANES, (h + 1) * LANES)
            rvt_ref[0, sl, rows] = y[:, sl].T.astype(BF16)
        r_ref[0, rows, 2 * RET_W:3 * RET_W] = _silu(mm(OFF_RG, OFF_AQ)).astype(BF16)
        y = jnp.dot(hb, wq_scr[...], preferred_element_type=F32)
        for j in range(ATT_W // LANES):
            sl = slice(j * LANES, (j + 1) * LANES)
            aq_ref[0, rows, sl] = (_rope_att(y[:, sl], ca, sa, low_half) * (ATT_DH ** -0.5 * LOG2E)).astype(BF16)
        y = mm(OFF_AK, OFF_AG)
        _store_kv_groups(ak_ref, av_ref, rows, _rope_att(y[:, 0:LANES], ca, sa, low_half), y[:, LANES:2 * LANES])
        ag_ref[0, rows, :] = _silu(jnp.dot(hb, wg_scr[...], preferred_element_type=F32)).astype(BF16)


def _project(x, ctx, mod3, w_in_b, tabs, dl):
    B, L, _ = x.shape
    lc = ctx.shape[1]
    tm = PROJ_ROWS
    ctx_row = mod3.shape[0] // 2
    st_shape = jax.ShapeDtypeStruct((B, RET_HEADS, RET_D, RET_D), F32)
    st_spec = pl.BlockSpec((1, RET_HEADS, RET_D, RET_D), lambda b, i: (b, 0, 0, 0))
    ckv_shape = jax.ShapeDtypeStruct((B, ATT_KV, lc, LANES), BF16)
    ckv_spec = pl.BlockSpec((1, ATT_KV, lc, LANES), lambda b, i: (b, 0, 0, 0))
    tab_spec = pl.BlockSpec((tm, LANES), lambda b, i: (i, 0))
    row = lambda n: pl.BlockSpec((1, tm, n), lambda b, i: (b, i, 0))
    kv_spec = pl.BlockSpec((1, ATT_KV, tm, LANES), lambda b, i: (b, 0, i, 0))
    kv_shape = jax.ShapeDtypeStruct((B, ATT_KV, L, LANES), BF16)
    return pl.pallas_call(
        _proj_kernel,
        out_shape=(jax.ShapeDtypeStruct((B, L, 3 * RET_W), BF16),
                   jax.ShapeDtypeStruct((B, RET_W, L), BF16),
                   jax.ShapeDtypeStruct((B, L, ATT_W), BF16),
                   kv_shape, kv_shape,
                   jax.ShapeDtypeStruct((B, L, ATT_W), BF16),
                   st_shape, st_shape, ckv_shape, ckv_shape),
        grid=(B, L // tm),
        in_specs=[row(D_MODEL),
                  pl.BlockSpec((1, 1, 3 * D_MODEL), lambda b, i: (b, 0, 0)),
                  pl.BlockSpec((D_MODEL, IN_COLS), lambda b, i: (0, 0), pipeline_mode=pl.Buffered(1)),
                  tab_spec, tab_spec, tab_spec, tab_spec,
                  pl.BlockSpec((1, lc, D_MODEL), lambda b, i: (b, 0, 0)),
                  pl.BlockSpec((1, 1, 3 * D_MODEL), lambda b, i: (ctx_row, 0, 0)),
                  pl.BlockSpec(dl.shape, lambda b, i: (0, 0, 0, 0))],
        out_specs=(row(3 * RET_W), pl.BlockSpec((1, RET_W, tm), lambda b, i: (b, 0, i)),
                   row(ATT_W), kv_spec, kv_spec, row(ATT_W),
                   st_spec, st_spec, ckv_spec, ckv_spec),
        scratch_shapes=[pltpu.VMEM((D_MODEL, ATT_W), BF16), pltpu.VMEM((D_MODEL, ATT_W), BF16)],
        compiler_params=pltpu.CompilerParams(
            dimension_semantics=("arbitrary", "arbitrary"), vmem_limit_bytes=VMEM_LIMIT),
        name="proj",
    )(x, mod3, w_in_b, *tabs, ctx, mod3, dl)


def _ret_kernel(q_ref, k_ref, vt_ref, g_ref, sf_ref, sb_ref, dl_ref, gn_ref, o_ref,
                uf_scr, rb_scr, kdec_scr, qdec_scr, cdec_scr, decay_scr):
    L = q_ref.shape[1]
    C = RET_CHUNK
    nch = L // C
    heads = range(RET_PAIR)
    lanes = lambda h: slice(h * RET_D, (h + 1) * RET_D)

    @pl.when(pl.program_id(1) == 0)
    def _():
        lg = _log_sigmoid(dl_ref[...])
        pos = lax.broadcasted_iota(jnp.int32, (C, 1), 0).astype(F32)
        posl = lax.broadcasted_iota(jnp.int32, (SUBLANES, C), 1).astype(F32)
        diff = (lax.broadcasted_iota(jnp.int32, (C, C), 1)
                - lax.broadcasted_iota(jnp.int32, (C, C), 0)).astype(F32)
        for h in heads:
            lgf, lgb = lg[0, h, 0:1, :], lg[1, h, 0:1, :]
            lgf1, lgb1 = lgf[:, 0:1], lgb[:, 0:1]
            kdec_scr[h, 0] = jnp.exp(lgf * (C - 1.0 - pos))
            kdec_scr[h, 1] = jnp.exp(lgb * pos)
            qdec_scr[h, 0] = jnp.exp(lgf1 * (posl + 1.0))
            qdec_scr[h, 1] = jnp.exp(lgb1 * (C - posl))
            cdec_scr[h, 0] = jnp.broadcast_to(jnp.exp(lgf * C), (SUBLANES, LANES))
            cdec_scr[h, 1] = jnp.broadcast_to(jnp.exp(lgb * C), (SUBLANES, LANES))
            decay_scr[h] = jnp.where(diff >= 0.0, jnp.exp(lgf1 * jnp.maximum(diff, 0.0)),
                                     jnp.exp(lgb1 * jnp.maximum(-diff, 0.0)))

    nt = (((1,), (1,)), ((), ()))
    chunk = lambda n: slice(n * C, (n + 1) * C)

    rb = [sb_ref[0, h] for h in heads]
    for n in reversed(range(nch)):
        for h in heads:
            kc = k_ref[0, chunk(n), lanes(h)].astype(F32)
            kd = jnp.concatenate([(kc * kdec_scr[h, 0]).astype(BF16), (kc * kdec_scr[h, 1]).astype(BF16)],
                                 axis=1)
            u = jnp.dot(vt_ref[0, lanes(h), chunk(n)], kd, preferred_element_type=F32)
            uf_scr[h, n] = u[:, 0:RET_D]
            rb_scr[h, n] = rb[h].astype(BF16)
            rb[h] = rb[h] * cdec_scr[h, 1, 0:1, :] + u[:, RET_D:2 * RET_D]

    def kq(h, n):
        return lax.dot_general(k_ref[0, chunk(n), lanes(h)], q_ref[0, chunk(n), lanes(h)], nt,
                               preferred_element_type=F32)

    rf = [sf_ref[0, h] for h in heads]
    s_next = [kq(h, 0) for h in heads]
    for n in range(nch):
        for h in heads:
            s = s_next[h]
            q = q_ref[0, chunk(n), lanes(h)]
            states = jnp.concatenate([rf[h].astype(BF16), rb_scr[h, n]], axis=0)
            cross = lax.dot_general(states, q, nt, preferred_element_type=F32)
            if n + 1 < nch:
                s_next[h] = kq(h, n + 1)
                rf[h] = rf[h] * cdec_scr[h, 0, 0:1, :] + uf_scr[h, n]
            inner = jnp.dot(vt_ref[0, lanes(h), chunk(n)], (s * decay_scr[h]).astype(BF16),
                            preferred_element_type=F32)
            y = (inner + cross[0:RET_D] * qdec_scr[h, 0, 0:1, :]
                 + cross[RET_D:2 * RET_D] * qdec_scr[h, 1, 0:1, :])
            mu = jnp.mean(y, axis=0, keepdims=True)
            yc = y - mu
            var = jnp.mean(yc * yc, axis=0, keepdims=True)
            yn = (yc * lax.rsqrt(var + EPS)).T * gn_ref[h]
            o_ref[0, chunk(n), lanes(h)] = (yn * g_ref[0, chunk(n), lanes(h)].astype(F32)).astype(BF16)


def _retention(r, rvt, sf, sb, dl, gn):
    B, L, _ = r.shape
    P, C = RET_PAIR, RET_CHUNK
    col = lambda off: pl.BlockSpec((1, L, P * RET_D), lambda hp, b: (b, 0, off + hp))
    st_spec = pl.BlockSpec((1, P, RET_D, RET_D), lambda hp, b: (b, hp, 0, 0))
    npair = RET_HEADS // P
    return pl.pallas_call(
        _ret_kernel,
        out_shape=jax.ShapeDtypeStruct((B, L, RET_W), BF16),
        grid=(npair, B),
        in_specs=[col(0), col(npair),
                  pl.BlockSpec((1, P * RET_D, L), lambda hp, b: (b, hp, 0)),
                  col(2 * npair),
                  st_spec, st_spec,
                  pl.BlockSpec((2, P, SUBLANES, LANES), lambda hp, b: (0, hp, 0, 0)),
                  pl.BlockSpec((P, 1, RET_D), lambda hp, b: (hp, 0, 0))],
        out_specs=pl.BlockSpec((1, L, P * RET_D), lambda hp, b: (b, 0, hp)),
        scratch_shapes=[pltpu.VMEM((P, L // C, RET_D, RET_D), F32),
                        pltpu.VMEM((P, L // C, RET_D, RET_D), BF16),
                        pltpu.VMEM((P, 2, C, LANES), F32),
                        pltpu.VMEM((P, 2, SUBLANES, C), F32),
                        pltpu.VMEM((P, 2, SUBLANES, LANES), F32),
                        pltpu.VMEM((P, C, C), F32)],
        compiler_params=pltpu.CompilerParams(
            dimension_semantics=("arbitrary", "arbitrary"), vmem_limit_bytes=VMEM_LIMIT),
        name="ret",
    )(r, r, rvt, r, sf, sb, dl, gn)


def _att_kernel(q_ref, k_ref, v_ref, ck_ref, cv_ref, g_ref, sink_ref, r_ref, x_ref, m_ref, w_ref, fn_ref,
                o_ref, w_scr):
    @pl.when((pl.program_id(0) == 0) & (pl.program_id(1) == 0))
    def _():
        w_scr[0:RET_W, :] = w_ref[0:RET_W, :]
        for j in range(ATT_HEADS // ATT_KV):
            for half in range(ATT_KV):
                dst = RET_W + j * LANES + half * ATT_DH
                src = RET_W + (j + half * (ATT_HEADS // ATT_KV)) * ATT_DH
                w_scr[dst:dst + ATT_DH, :] = w_ref[src:src + ATT_DH, :]

    L = k_ref.shape[2]
    nb_total = L // ATT_BLOCK
    nb_step = q_ref.shape[1] // ATT_BLOCK
    nhb = ATT_W // LANES
    i = pl.program_id(1)
    r_i = lax.broadcasted_iota(jnp.int32, (ATT_BLOCK, ATT_BLOCK), 0)
    q_i = lax.broadcasted_iota(jnp.int32, (ATT_BLOCK, ATT_BLOCK), 1)
    band_prev = jnp.where(q_i <= r_i, 0.0, NEG).astype(F32)
    band_next = jnp.where(r_i <= q_i, 0.0, NEG).astype(F32)
    nt = (((1,), (1,)), ((), ()))
    tn = (((0,), (0,)), ((), ()))

    def key_rows(jb):
        n = i * nb_step + jb
        blk = lambda t: pl.ds(pl.multiple_of(t * ATT_BLOCK, ATT_BLOCK), ATT_BLOCK)
        return n, (blk(jnp.maximum(n - 1, 0)), blk(n), blk(jnp.minimum(n + 1, nb_total - 1)))

    def scores(jb, kv):
        n, (p_rows, o_rows, n_rows) = key_rows(jb)
        q = q_ref[0, jb * ATT_BLOCK:(jb + 1) * ATT_BLOCK, :]
        qall = jnp.concatenate([q[:, j * LANES:(j + 1) * LANES] for j in range(nhb)], axis=0)
        kall = jnp.concatenate([ck_ref[0, kv], k_ref[0, kv, p_rows, :], k_ref[0, kv, o_rows, :],
                                k_ref[0, kv, n_rows, :]], axis=0)
        return lax.dot_general(kall, qall, nt, preferred_element_type=F32)

    def attend(jb, kv, s):
        n, (p_rows, o_rows, n_rows) = key_rows(jb)
        lc = ck_ref.shape[2]
        bias_prev = jnp.concatenate([band_prev + jnp.where(n == 0, NEG, 0.0)] * nhb, axis=1)
        bias_next = jnp.concatenate([band_next + jnp.where(n == nb_total - 1, NEG, 0.0)] * nhb, axis=1)
        parts = [s[0:lc], s[lc:lc + ATT_BLOCK] + bias_prev, s[lc + ATT_BLOCK:lc + 2 * ATT_BLOCK],
                 s[lc + 2 * ATT_BLOCK:] + bias_next]
        sink = sink_ref[kv:kv + 1, :] * LOG2E
        m = sink
        for t in parts:
            m = jnp.maximum(m, jnp.max(t, axis=0, keepdims=True))
        p = jnp.concatenate([jnp.exp2(t - m).astype(BF16) for t in parts], axis=0)
        vaug = jnp.concatenate([cv_ref[0, kv], v_ref[0, kv, p_rows, :], v_ref[0, kv, o_rows, :],
                                v_ref[0, kv, n_rows, :]], axis=0)
        o = lax.dot_general(vaug, p, tn, preferred_element_type=F32)
        if kv == 0:
            val, ones = slice(0, ATT_DH), slice(ATT_DH, ATT_DH + 1)
        else:
            val, ones = slice(ATT_DH, 2 * ATT_DH), slice(0, 1)
        den = o[ones] + jnp.exp2(sink - m)
        return o[val] * (1.0 / den)

    def gated_block(jb, halves):
        rows = slice(jb * ATT_BLOCK, (jb + 1) * ATT_BLOCK)
        comb = jnp.concatenate(halves, axis=0)
        return jnp.concatenate(
            [(comb[:, j * LANES:(j + 1) * LANES].T
              * g_ref[0, rows, j * LANES:(j + 1) * LANES].astype(F32)).astype(BF16) for j in range(nhb)], axis=1)

    def finish(jb, att_blk):
        rows = slice(jb * ATT_BLOCK, (jb + 1) * ATT_BLOCK)
        mix_in = jnp.concatenate([r_ref[0, rows, :], att_blk], axis=1)
        mixed = jnp.dot(mix_in, w_scr[...], preferred_element_type=F32)
        xn = x_ref[0, rows, :] + m_ref[0, :, 2 * D_MODEL:3 * D_MODEL] * mixed
        o_ref[0, rows, :] = xn * lax.rsqrt(jnp.mean(xn * xn, axis=-1, keepdims=True) + EPS) * fn_ref[...]

    chains = [(jb, kv) for jb in range(nb_step) for kv in range(ATT_KV)]
    ahead = [scores(*chains[t]) for t in range(ATT_AHEAD)]
    halves, pending = [], []
    for c, (jb, kv) in enumerate(chains):
        s = ahead.pop(0)
        if c + ATT_AHEAD < len(chains):
            ahead.append(scores(*chains[c + ATT_AHEAD]))
        halves.append(attend(jb, kv, s))
        if pending:
            finish(*pending.pop())
        if kv == ATT_KV - 1:
            pending.append((jb, gated_block(jb, halves)))
            halves = []
    finish(*pending.pop())


def _attention_output(aq, ak, av, ck, cv, ag, sink2, ret, x, mod3, w_out_b, fnw):
    B, L, _ = aq.shape
    lc = ck.shape[2]
    tq = ATT_ROWS
    row = lambda n: pl.BlockSpec((1, tq, n), lambda b, i: (b, i, 0))
    full = lambda n: pl.BlockSpec((1, ATT_KV, n, LANES), lambda b, i: (b, 0, 0, 0))
    return pl.pallas_call(
        _att_kernel,
        out_shape=jax.ShapeDtypeStruct((B, L, D_MODEL), F32),
        grid=(B, L // tq),
        in_specs=[row(ATT_W), full(L), full(L), full(lc), full(lc), row(ATT_W),
                  pl.BlockSpec(sink2.shape, lambda b, i: (0, 0)),
                  row(RET_W), row(D_MODEL),
                  pl.BlockSpec((1, 1, 3 * D_MODEL), lambda b, i: (b, 0, 0)),
                  pl.BlockSpec((RET_W + ATT_W, D_MODEL), lambda b, i: (0, 0)),
                  pl.BlockSpec((1, D_MODEL), lambda b, i: (0, 0))],
        out_specs=row(D_MODEL),
        scratch_shapes=[pltpu.VMEM((RET_W + ATT_W, D_MODEL), BF16)],
        compiler_params=pltpu.CompilerParams(
            dimension_semantics=("arbitrary", "arbitrary"), vmem_limit_bytes=VMEM_LIMIT),
        name="att",
    )(aq, ak, av, ck, cv, ag, sink2, ret, x, mod3, w_out_b, fnw)


def _rope_tables(L):
    pos = np.arange(L)
    rows, cols = (pos // GRID_W).astype(np.float64), (pos % GRID_W).astype(np.float64)

    def tables(dh):
        nf = dh // 4
        inv = ROPE_BASE ** (-np.arange(nf, dtype=np.float64) / nf)
        ang = np.concatenate([rows[:, None] * inv, cols[:, None] * inv], axis=-1)
        cos, sin = np.cos(ang), np.sin(ang)
        reps = LANES // dh
        return (jnp.asarray(np.tile(np.concatenate([cos, cos], axis=-1), (1, reps)), F32),
                jnp.asarray(np.tile(np.concatenate([-sin, sin], axis=-1), (1, reps)), F32))

    cr, sr = tables(RET_D)
    ca, sa = tables(ATT_DH)
    return cr, sr, ca, sa


def kernel(x, c, ctx, c_ctx, w_ada, b_ada, w_in, ret_decay_logit, ret_gn_w, att_sink, w_out, final_norm_w):
    B, L, _ = x.shape
    assert w_ada.shape[0] == 1, "single-layer trunk"

    cc = jnp.concatenate([c, c_ctx[None, :], jnp.zeros((B - 1, D_MODEL), F32)], axis=0)
    mod3 = _modulation(cc, w_ada[0], b_ada[0][None, :]).reshape(2 * B, 1, 3 * D_MODEL)

    w_in_b, w_out_b = w_in[0].astype(BF16), w_out[0].astype(BF16)
    dl = jnp.broadcast_to(ret_decay_logit[0][:, :, None, None], (2, RET_HEADS, SUBLANES, LANES))

    r, rvt, aq, ak, av, ag, sf, sb, ck, cv = _project(x, ctx, mod3, w_in_b, _rope_tables(L), dl)
    ret = _retention(r, rvt, sf, sb, dl, ret_gn_w[0].reshape(RET_HEADS, 1, RET_D))

    sink2 = jnp.repeat(att_sink[0].reshape(ATT_KV, ATT_HEADS // ATT_KV), LANES, axis=1)
    return _attention_output(aq, ak, av, ck, cv, ag, sink2, ret, x, mod3, w_out_b, final_norm_w[None, :])
```

```python
import numpy as np
import jax
import jax.numpy as jnp
from jax import lax
from jax.experimental import pallas as pl
from jax.experimental.pallas import tpu as pltpu

F32 = jnp.float32
BF16 = jnp.bfloat16

D_MODEL = 1024
GRID_W = 64
RET_HEADS = 4
RET_D = 128
RET_W = RET_HEADS * RET_D
ATT_HEADS = 8
ATT_KV = 2
ATT_DH = 64
ATT_W = ATT_HEADS * ATT_DH
ATT_BLOCK = 128
WINDOW = 128
ROPE_BASE = 10000.0
EPS = 1e-6
NEG = -1e30
LOG2E = 1.4426950408889634
IN_COLS = 4 * RET_W + 2 * ATT_W + 2 * ATT_KV * ATT_DH
OFF_RQ, OFF_RK, OFF_RV, OFF_RG = 0, RET_W, 2 * RET_W, 3 * RET_W
OFF_AQ = 4 * RET_W
OFF_AK = OFF_AQ + ATT_W
OFF_AV = OFF_AK + ATT_KV * ATT_DH
OFF_AG = OFF_AV + ATT_KV * ATT_DH

LANES = 128
SUBLANES = 8
VMEM_LIMIT = 56 * 1024 * 1024

PROJ_ROWS = 1024
PROJ_SUB = 256
RET_CHUNK = 256
RET_PAIR = 2
ATT_ROWS = 1024
ATT_AHEAD = 2


def _silu(t):
    return t / (1.0 + jnp.exp(-t))


def _log_sigmoid(t):
    return -(jnp.maximum(-t, 0.0) + jnp.log1p(jnp.exp(-jnp.abs(t))))


def _mod_kernel(c_ref, w_ref, b_ref, o_ref):
    n = c_ref.shape[0]
    s, w = _silu(c_ref[...]), w_ref[...]
    s_hi, w_hi = s.astype(BF16), w.astype(BF16)
    s_lo = (s - s_hi.astype(F32)).astype(BF16)
    w_lo = (w - w_hi.astype(F32)).astype(BF16)
    a = jnp.dot(jnp.concatenate([s_hi, s_lo], axis=0), w_hi, preferred_element_type=F32)
    o_ref[...] = a[0:n] + a[n:2 * n] + jnp.dot(s_hi, w_lo, preferred_element_type=F32) + b_ref[...]


def _modulation(cc, w_ada, b_ada):
    n = w_ada.shape[1]
    bn = 512
    return pl.pallas_call(
        _mod_kernel,
        out_shape=jax.ShapeDtypeStruct((cc.shape[0], n), F32),
        grid=(n // bn,),
        in_specs=[pl.BlockSpec(cc.shape, lambda i: (0, 0)),
                  pl.BlockSpec((D_MODEL, bn), lambda i: (0, i)),
                  pl.BlockSpec((1, bn), lambda i: (0, i))],
        out_specs=pl.BlockSpec((cc.shape[0], bn), lambda i: (0, i)),
        compiler_params=pltpu.CompilerParams(dimension_semantics=("arbitrary",)),
        name="mod",
    )(cc, w_ada, b_ada)


def _norm_mod(x, m_ref):
    shift = m_ref[0, :, 0:D_MODEL]
    scale = m_ref[0, :, D_MODEL:2 * D_MODEL]
    h = x * lax.rsqrt(jnp.mean(x * x, axis=-1, keepdims=True) + EPS)
    return h * (1.0 + scale) + shift


def _rope_ret(t, cos, sin):
    return t * cos + pltpu.roll(t, 64, 1) * sin


def _rope_att(t, cos, sin, low_half):
    partner = jnp.where(low_half, pltpu.roll(t, 96, 1), pltpu.roll(t, 32, 1))
    return t * cos + partner * sin


def _store_kv_groups(k_ref, v_ref, rows, k, v):
    low = lax.broadcasted_iota(jnp.int32, k.shape, 1) < ATT_DH
    k_ref[0, 0, rows, :] = jnp.where(low, k, 0.0).astype(BF16)
    k_ref[0, 1, rows, :] = jnp.where(low, 0.0, k).astype(BF16)
    v_ref[0, 0, rows, :] = jnp.where(low, v, 1.0).astype(BF16)
    v_ref[0, 1, rows, :] = jnp.where(low, 1.0, v).astype(BF16)


def _pair_heads(nat):
    low = lax.broadcasted_iota(jnp.int32, nat[0].shape, 1) < ATT_DH
    out = []
    for j in range(len(nat)):
        a, b = nat[j // 2], nat[2 + j // 2]
        if j % 2 == 0:
            out.append(jnp.where(low, a, pltpu.roll(b, ATT_DH, 1)))
        else:
            out.append(jnp.where(low, pltpu.roll(a, ATT_DH, 1), b))
    return out


def _context_block(x_ref, m_ref, w_ref, dl_ref, sf_ref, sb_ref, ck_ref, cv_ref):
    lc = x_ref.shape[1]
    hb = _norm_mod(x_ref[0], m_ref).astype(BF16)
    yk = jnp.dot(hb, w_ref[:, OFF_RK:OFF_RV], preferred_element_type=F32)
    yv = jnp.dot(hb, w_ref[:, OFF_RV:OFF_RG], preferred_element_type=F32)
    ya = jnp.dot(hb, w_ref[:, OFF_AK:OFF_AG], preferred_element_type=F32)
    _store_kv_groups(ck_ref, cv_ref, slice(0, lc), ya[:, 0:LANES], ya[:, LANES:2 * LANES])
    lg = _log_sigmoid(dl_ref[...])
    pos = lax.broadcasted_iota(jnp.int32, (lc, 1), 0).astype(F32)
    dn = (((0,), (0,)), ((), ()))
    for h in range(RET_HEADS):
        k = yk[:, h * RET_D:(h + 1) * RET_D] * (RET_D ** -0.5)
        v = yv[:, h * RET_D:(h + 1) * RET_D].astype(BF16)
        wf = jnp.exp(lg[0, h, 0:1, :] * (lc - 1.0 - pos))
        wb = jnp.exp(lg[1, h, 0:1, :] * pos)
        sf_ref[0, h] = lax.dot_general(v, (k * wf).astype(BF16), dn, preferred_element_type=F32)
        sb_ref[0, h] = lax.dot_general(v, (k * wb).astype(BF16), dn, preferred_element_type=F32)


def _proj_kernel(x_ref, m_ref, w_ref, cr_ref, sr_ref, ca_ref, sa_ref, cx_ref, mc_ref, dl_ref,
                 r_ref, rvt_ref, aq_ref, ak_ref, av_ref, ag_ref, sf_ref, sb_ref, ck_ref, cv_ref,
                 w_scr):

    @pl.when((pl.program_id(0) == 0) & (pl.program_id(1) == 0))
    def _():
        for r in range(0, D_MODEL, LANES):
            rr = slice(r, r + LANES)
            for c0, c1 in ((OFF_RQ, OFF_RK), (OFF_RK, OFF_RV), (OFF_RV, OFF_RG), (OFF_RG, OFF_AQ),
                           (OFF_AK, OFF_AG)):
                w_scr[rr, c0:c1] = w_ref[rr, c0:c1].astype(BF16)
            for off in (OFF_AQ, OFF_AG):
                nat = [w_ref[rr, off + j * LANES:off + (j + 1) * LANES] for j in range(ATT_W // LANES)]
                for j, blk in enumerate(_pair_heads(nat)):
                    w_scr[rr, off + j * LANES:off + (j + 1) * LANES] = blk.astype(BF16)

    @pl.when(pl.program_id(1) == 0)
    def _():
        _context_block(cx_ref, mc_ref, w_scr, dl_ref, sf_ref, sb_ref, ck_ref, cv_ref)

    for t in range(x_ref.shape[1] // PROJ_SUB):
        rows = slice(t * PROJ_SUB, (t + 1) * PROJ_SUB)
        hb = _norm_mod(x_ref[0, rows, :], m_ref).astype(BF16)
        cr, sr, ca, sa = cr_ref[rows, :], sr_ref[rows, :], ca_ref[rows, :], sa_ref[rows, :]
        low_half = (lax.broadcasted_iota(jnp.int32, ca.shape, 1) % ATT_DH) < (ATT_DH // 2)

        def mm(c0, c1):
            return jnp.dot(hb, w_scr[:, c0:c1], preferred_element_type=F32)

        y = mm(OFF_RQ, OFF_RK)
        for h in range(RET_HEADS):
            sl = slice(h * LANES, (h + 1) * LANES)
            r_ref[0, rows, OFF_RQ + h * LANES:OFF_RQ + (h + 1) * LANES] = _rope_ret(y[:, sl], cr, sr).astype(BF16)
        y = mm(OFF_RK, OFF_RV) * (RET_D ** -0.5)
        for h in range(RET_HEADS):
            sl = slice(h * LANES, (h + 1) * LANES)
            r_ref[0, rows, OFF_RK + h * LANES:OFF_RK + (h + 1) * LANES] = _rope_ret(y[:, sl], cr, sr).astype(BF16)
        y = mm(OFF_RV, OFF_RG)
        for h in range(RET_HEADS):
            sl = slice(h * LANES, (h + 1) * LANES)
            rvt_ref[0, sl, rows] = y[:, sl].T.astype(BF16)
        r_ref[0, rows, 2 * RET_W:3 * RET_W] = _silu(mm(OFF_RG, OFF_AQ)).astype(BF16)
        y = mm(OFF_AQ, OFF_AK)
        for j in range(ATT_W // LANES):
            sl = slice(j * LANES, (j + 1) * LANES)
            aq_ref[0, rows, sl] = (_rope_att(y[:, sl], ca, sa, low_half) * (ATT_DH ** -0.5 * LOG2E)).astype(BF16)
        y = mm(OFF_AK, OFF_AG)
        _store_kv_groups(ak_ref, av_ref, rows, _rope_att(y[:, 0:LANES], ca, sa, low_half), y[:, LANES:2 * LANES])
        ag_ref[0, rows, :] = _silu(mm(OFF_AG, IN_COLS)).astype(BF16)


def _project(x, ctx, mod3, w_in_l, tabs, dl):
    B, L, _ = x.shape
    lc = ctx.shape[1]
    tm = PROJ_ROWS
    ctx_row = mod3.shape[0] // 2
    st_shape = jax.ShapeDtypeStruct((B, RET_HEADS, RET_D, RET_D), F32)
    st_spec = pl.BlockSpec((1, RET_HEADS, RET_D, RET_D), lambda b, i: (b, 0, 0, 0))
    ckv_shape = jax.ShapeDtypeStruct((B, ATT_KV, lc, LANES), BF16)
    ckv_spec = pl.BlockSpec((1, ATT_KV, lc, LANES), lambda b, i: (b, 0, 0, 0))
    tab_spec = pl.BlockSpec((tm, LANES), lambda b, i: (i, 0))
    row = lambda n: pl.BlockSpec((1, tm, n), lambda b, i: (b, i, 0))
    kv_spec = pl.BlockSpec((1, ATT_KV, tm, LANES), lambda b, i: (b, 0, i, 0))
    kv_shape = jax.ShapeDtypeStruct((B, ATT_KV, L, LANES), BF16)
    return pl.pallas_call(
        _proj_kernel,
        out_shape=(jax.ShapeDtypeStruct((B, L, 3 * RET_W), BF16),
                   jax.ShapeDtypeStruct((B, RET_W, L), BF16),
                   jax.ShapeDtypeStruct((B, L, ATT_W), BF16),
                   kv_shape, kv_shape,
                   jax.ShapeDtypeStruct((B, L, ATT_W), BF16),
                   st_shape, st_shape, ckv_shape, ckv_shape),
        grid=(B, L // tm),
        in_specs=[row(D_MODEL),
                  pl.BlockSpec((1, 1, 3 * D_MODEL), lambda b, i: (b, 0, 0)),
                  pl.BlockSpec((D_MODEL, IN_COLS), lambda b, i: (0, 0), pipeline_mode=pl.Buffered(1)),
                  tab_spec, tab_spec, tab_spec, tab_spec,
                  pl.BlockSpec((1, lc, D_MODEL), lambda b, i: (b, 0, 0)),
                  pl.BlockSpec((1, 1, 3 * D_MODEL), lambda b, i: (ctx_row, 0, 0)),
                  pl.BlockSpec(dl.shape, lambda b, i: (0, 0, 0, 0))],
        out_specs=(row(3 * RET_W), pl.BlockSpec((1, RET_W, tm), lambda b, i: (b, 0, i)),
                   row(ATT_W), kv_spec, kv_spec, row(ATT_W),
                   st_spec, st_spec, ckv_spec, ckv_spec),
        scratch_shapes=[pltpu.VMEM((D_MODEL, IN_COLS), BF16)],
        compiler_params=pltpu.CompilerParams(
            dimension_semantics=("arbitrary", "arbitrary"), vmem_limit_bytes=VMEM_LIMIT),
        name="proj",
    )(x, mod3, w_in_l, *tabs, ctx, mod3, dl)


def _ret_kernel(q_ref, k_ref, vt_ref, g_ref, sf_ref, sb_ref, dl_ref, gn_ref, o_ref,
                uf_scr, rb_scr, kdec_scr, qdec_scr, cdec_scr, decay_scr):
    L = q_ref.shape[1]
    C = RET_CHUNK
    nch = L // C
    heads = range(RET_PAIR)
    lanes = lambda h: slice(h * RET_D, (h + 1) * RET_D)

    @pl.when(pl.program_id(1) == 0)
    def _():
        lg = _log_sigmoid(dl_ref[...])
        pos = lax.broadcasted_iota(jnp.int32, (C, 1), 0).astype(F32)
        posl = lax.broadcasted_iota(jnp.int32, (SUBLANES, C), 1).astype(F32)
        diff = (lax.broadcasted_iota(jnp.int32, (C, C), 1)
                - lax.broadcasted_iota(jnp.int32, (C, C), 0)).astype(F32)
        for h in heads:
            lgf, lgb = lg[0, h, 0:1, :], lg[1, h, 0:1, :]
            lgf1, lgb1 = lgf[:, 0:1], lgb[:, 0:1]
            kdec_scr[h, 0] = jnp.exp(lgf * (C - 1.0 - pos))
            kdec_scr[h, 1] = jnp.exp(lgb * pos)
            qdec_scr[h, 0] = jnp.exp(lgf1 * (posl + 1.0))
            qdec_scr[h, 1] = jnp.exp(lgb1 * (C - posl))
            cdec_scr[h, 0] = jnp.broadcast_to(jnp.exp(lgf * C), (SUBLANES, LANES))
            cdec_scr[h, 1] = jnp.broadcast_to(jnp.exp(lgb * C), (SUBLANES, LANES))
            decay_scr[h] = jnp.where(diff >= 0.0, jnp.exp(lgf1 * jnp.maximum(diff, 0.0)),
                                     jnp.exp(lgb1 * jnp.maximum(-diff, 0.0)))

    nt = (((1,), (1,)), ((), ()))
    chunk = lambda n: slice(n * C, (n + 1) * C)

    rb = [sb_ref[0, h] for h in heads]
    for n in reversed(range(nch)):
        for h in heads:
            kc = k_ref[0, chunk(n), lanes(h)].astype(F32)
            kd = jnp.concatenate([(kc * kdec_scr[h, 0]).astype(BF16), (kc * kdec_scr[h, 1]).astype(BF16)],
                                 axis=1)
            u = jnp.dot(vt_ref[0, lanes(h), chunk(n)], kd, preferred_element_type=F32)
            uf_scr[h, n] = u[:, 0:RET_D]
            rb_scr[h, n] = rb[h].astype(BF16)
            rb[h] = rb[h] * cdec_scr[h, 1, 0:1, :] + u[:, RET_D:2 * RET_D]

    def kq(h, n):
        return lax.dot_general(k_ref[0, chunk(n), lanes(h)], q_ref[0, chunk(n), lanes(h)], nt,
                               preferred_element_type=F32)

    rf = [sf_ref[0, h] for h in heads]
    s_next = [kq(h, 0) for h in heads]
    for n in range(nch):
        for h in heads:
            s = s_next[h]
            q = q_ref[0, chunk(n), lanes(h)]
            states = jnp.concatenate([rf[h].astype(BF16), rb_scr[h, n]], axis=0)
            cross = lax.dot_general(states, q, nt, preferred_element_type=F32)
            if n + 1 < nch:
                s_next[h] = kq(h, n + 1)
                rf[h] = rf[h] * cdec_scr[h, 0, 0:1, :] + uf_scr[h, n]
            inner = jnp.dot(vt_ref[0, lanes(h), chunk(n)], (s * decay_scr[h]).astype(BF16),
                            preferred_element_type=F32)
            y = (inner + cross[0:RET_D] * qdec_scr[h, 0, 0:1, :]
                 + cross[RET_D:2 * RET_D] * qdec_scr[h, 1, 0:1, :])
            mu = jnp.mean(y, axis=0, keepdims=True)
            yc = y - mu
            var = jnp.mean(yc * yc, axis=0, keepdims=True)
            yn = (yc * lax.rsqrt(var + EPS)).T * gn_ref[h]
            o_ref[0, chunk(n), lanes(h)] = (yn * g_ref[0, chunk(n), lanes(h)].astype(F32)).astype(BF16)


def _retention(r, rvt, sf, sb, dl, gn):
    B, L, _ = r.shape
    P, C = RET_PAIR, RET_CHUNK
    col = lambda off: pl.BlockSpec((1, L, P * RET_D), lambda hp, b: (b, 0, off + hp))
    st_spec = pl.BlockSpec((1, P, RET_D, RET_D), lambda hp, b: (b, hp, 0, 0))
    npair = RET_HEADS // P
    return pl.pallas_call(
        _ret_kernel,
        out_shape=jax.ShapeDtypeStruct((B, L, RET_W), BF16),
        grid=(npair, B),
        in_specs=[col(0), col(npair),
                  pl.BlockSpec((1, P * RET_D, L), lambda hp, b: (b, hp, 0)),
                  col(2 * npair),
                  st_spec, st_spec,
                  pl.BlockSpec((2, P, SUBLANES, LANES), lambda hp, b: (0, hp, 0, 0)),
                  pl.BlockSpec((P, 1, RET_D), lambda hp, b: (hp, 0, 0))],
        out_specs=pl.BlockSpec((1, L, P * RET_D), lambda hp, b: (b, 0, hp)),
        scratch_shapes=[pltpu.VMEM((P, L // C, RET_D, RET_D), F32),
                        pltpu.VMEM((P, L // C, RET_D, RET_D), BF16),
                        pltpu.VMEM((P, 2, C, LANES), F32),
                        pltpu.VMEM((P, 2, SUBLANES, C), F32),
                        pltpu.VMEM((P, 2, SUBLANES, LANES), F32),
                        pltpu.VMEM((P, C, C), F32)],
        compiler_params=pltpu.CompilerParams(
            dimension_semantics=("arbitrary", "arbitrary"), vmem_limit_bytes=VMEM_LIMIT),
        name="ret",
    )(r, r, rvt, r, sf, sb, dl, gn)


def _att_kernel(q_ref, k_ref, v_ref, ck_ref, cv_ref, g_ref, sink_ref, r_ref, x_ref, m_ref, w_ref, fn_ref,
                o_ref, w_scr):
    @pl.when((pl.program_id(0) == 0) & (pl.program_id(1) == 0))
    def _():
        w_scr[0:RET_W, :] = w_ref[0:RET_W, :].astype(BF16)
        for j in range(ATT_HEADS // ATT_KV):
            for half in range(ATT_KV):
                dst = RET_W + j * LANES + half * ATT_DH
                src = RET_W + (j + half * (ATT_HEADS // ATT_KV)) * ATT_DH
                w_scr[dst:dst + ATT_DH, :] = w_ref[src:src + ATT_DH, :].astype(BF16)

    L = k_ref.shape[2]
    nb_total = L // ATT_BLOCK
    nb_step = q_ref.shape[1] // ATT_BLOCK
    nhb = ATT_W // LANES
    i = pl.program_id(1)
    r_i = lax.broadcasted_iota(jnp.int32, (ATT_BLOCK, ATT_BLOCK), 0)
    q_i = lax.broadcasted_iota(jnp.int32, (ATT_BLOCK, ATT_BLOCK), 1)
    band_prev = jnp.where(q_i <= r_i, 0.0, NEG).astype(F32)
    band_next = jnp.where(r_i <= q_i, 0.0, NEG).astype(F32)
    nt = (((1,), (1,)), ((), ()))
    tn = (((0,), (0,)), ((), ()))

    def key_rows(jb):
        n = i * nb_step + jb
        blk = lambda t: pl.ds(pl.multiple_of(t * ATT_BLOCK, ATT_BLOCK), ATT_BLOCK)
        return n, (blk(jnp.maximum(n - 1, 0)), blk(n), blk(jnp.minimum(n + 1, nb_total - 1)))

    def scores(jb, kv):
        n, (p_rows, o_rows, n_rows) = key_rows(jb)
        q = q_ref[0, jb * ATT_BLOCK:(jb + 1) * ATT_BLOCK, :]
        qall = jnp.concatenate([q[:, j * LANES:(j + 1) * LANES] for j in range(nhb)], axis=0)
        kall = jnp.concatenate([ck_ref[0, kv], k_ref[0, kv, p_rows, :], k_ref[0, kv, o_rows, :],
                                k_ref[0, kv, n_rows, :]], axis=0)
        return lax.dot_general(kall, qall, nt, preferred_element_type=F32)

    def attend(jb, kv, s):
        n, (p_rows, o_rows, n_rows) = key_rows(jb)
        lc = ck_ref.shape[2]
        bias_prev = jnp.concatenate([band_prev + jnp.where(n == 0, NEG, 0.0)] * nhb, axis=1)
        bias_next = jnp.concatenate([band_next + jnp.where(n == nb_total - 1, NEG, 0.0)] * nhb, axis=1)
        parts = [s[0:lc], s[lc:lc + ATT_BLOCK] + bias_prev, s[lc + ATT_BLOCK:lc + 2 * ATT_BLOCK],
                 s[lc + 2 * ATT_BLOCK:] + bias_next]
        sink = sink_ref[kv:kv + 1, :] * LOG2E
        m = sink
        for t in parts:
            m = jnp.maximum(m, jnp.max(t, axis=0, keepdims=True))
        p = jnp.concatenate([jnp.exp2(t - m).astype(BF16) for t in parts], axis=0)
        vaug = jnp.concatenate([cv_ref[0, kv], v_ref[0, kv, p_rows, :], v_ref[0, kv, o_rows, :],
                                v_ref[0, kv, n_rows, :]], axis=0)
        o = lax.dot_general(vaug, p, tn, preferred_element_type=F32)
        if kv == 0:
            val, ones = slice(0, ATT_DH), slice(ATT_DH, ATT_DH + 1)
        else:
            val, ones = slice(ATT_DH, 2 * ATT_DH), slice(0, 1)
        den = o[ones] + jnp.exp2(sink - m)
        return o[val] * (1.0 / den)

    def gated_block(jb, halves):
        rows = slice(jb * ATT_BLOCK, (jb + 1) * ATT_BLOCK)
        comb = jnp.concatenate(halves, axis=0)
        return jnp.concatenate(
            [(comb[:, j * LANES:(j + 1) * LANES].T
              * g_ref[0, rows, j * LANES:(j + 1) * LANES].astype(F32)).astype(BF16) for j in range(nhb)], axis=1)

    def finish(jb, att_blk):
        rows = slice(jb * ATT_BLOCK, (jb + 1) * ATT_BLOCK)
        mix_in = jnp.concatenate([r_ref[0, rows, :], att_blk], axis=1)
        mixed = jnp.dot(mix_in, w_scr[...], preferred_element_type=F32)
        xn = x_ref[0, rows, :] + m_ref[0, :, 2 * D_MODEL:3 * D_MODEL] * mixed
        o_ref[0, rows, :] = xn * lax.rsqrt(jnp.mean(xn * xn, axis=-1, keepdims=True) + EPS) * fn_ref[...]

    chains = [(jb, kv) for jb in range(nb_step) for kv in range(ATT_KV)]
    ahead = [scores(*chains[t]) for t in range(ATT_AHEAD)]
    halves, pending = [], []
    for c, (jb, kv) in enumerate(chains):
        s = ahead.pop(0)
        if c + ATT_AHEAD < len(chains):
            ahead.append(scores(*chains[c + ATT_AHEAD]))
        halves.append(attend(jb, kv, s))
        if pending:
            finish(*pending.pop())
        if kv == ATT_KV - 1:
            pending.append((jb, gated_block(jb, halves)))
            halves = []
    finish(*pending.pop())


def _attention_output(aq, ak, av, ck, cv, ag, sink2, ret, x, mod3, w_out_l, fnw):
    B, L, _ = aq.shape
    lc = ck.shape[2]
    tq = ATT_ROWS
    row = lambda n: pl.BlockSpec((1, tq, n), lambda b, i: (b, i, 0))
    full = lambda n: pl.BlockSpec((1, ATT_KV, n, LANES), lambda b, i: (b, 0, 0, 0))
    return pl.pallas_call(
        _att_kernel,
        out_shape=jax.ShapeDtypeStruct((B, L, D_MODEL), F32),
        grid=(B, L // tq),
        in_specs=[row(ATT_W), full(L), full(L), full(lc), full(lc), row(ATT_W),
                  pl.BlockSpec(sink2.shape, lambda b, i: (0, 0)),
                  row(RET_W), row(D_MODEL),
                  pl.BlockSpec((1, 1, 3 * D_MODEL), lambda b, i: (b, 0, 0)),
                  pl.BlockSpec((RET_W + ATT_W, D_MODEL), lambda b, i: (0, 0), pipeline_mode=pl.Buffered(1)),
                  pl.BlockSpec((1, D_MODEL), lambda b, i: (0, 0))],
        out_specs=row(D_MODEL),
        scratch_shapes=[pltpu.VMEM((RET_W + ATT_W, D_MODEL), BF16)],
        compiler_params=pltpu.CompilerParams(
            dimension_semantics=("arbitrary", "arbitrary"), vmem_limit_bytes=VMEM_LIMIT),
        name="att",
    )(aq, ak, av, ck, cv, ag, sink2, ret, x, mod3, w_out_l, fnw)


def _rope_tables(L):
    pos = np.arange(L)
    rows, cols = (pos // GRID_W).astype(np.float64), (pos % GRID_W).astype(np.float64)

    def tables(dh):
        nf = dh // 4
        inv = ROPE_BASE ** (-np.arange(nf, dtype=np.float64) / nf)
        ang = np.concatenate([rows[:, None] * inv, cols[:, None] * inv], axis=-1)
        cos, sin = np.cos(ang), np.sin(ang)
        reps = LANES // dh
        return (jnp.asarray(np.tile(np.concatenate([cos, cos], axis=-1), (1, reps)), F32),
                jnp.asarray(np.tile(np.concatenate([-sin, sin], axis=-1), (1, reps)), F32))

    cr, sr = tables(RET_D)
    ca, sa = tables(ATT_DH)
    return cr, sr, ca, sa


def kernel(x, c, ctx, c_ctx, w_ada, b_ada, w_in, ret_decay_logit, ret_gn_w, att_sink, w_out, final_norm_w):
    B, L, _ = x.shape
    assert w_ada.shape[0] == 1, "single-layer trunk"

    cc = jnp.concatenate([c, c_ctx[None, :], jnp.zeros((B - 1, D_MODEL), F32)], axis=0)
    mod3 = _modulation(cc, w_ada[0], b_ada[0][None, :]).reshape(2 * B, 1, 3 * D_MODEL)

    dl = jnp.broadcast_to(ret_decay_logit[0][:, :, None, None], (2, RET_HEADS, SUBLANES, LANES))

    r, rvt, aq, ak, av, ag, sf, sb, ck, cv = _project(x, ctx, mod3, w_in[0], _rope_tables(L), dl)
    ret = _retention(r, rvt, sf, sb, dl, ret_gn_w[0].reshape(RET_HEADS, 1, RET_D))

    sink2 = jnp.repeat(att_sink[0].reshape(ATT_KV, ATT_HEADS // ATT_KV), LANES, axis=1)
    return _attention_output(aq, ak, av, ck, cv, ag, sink2, ret, x, mod3, w_out[0], final_norm_w[None, :])
```

```python
import numpy as np
import jax
import jax.numpy as jnp
from jax import lax
from jax.experimental import pallas as pl
from jax.experimental.pallas import tpu as pltpu

F32 = jnp.float32
BF16 = jnp.bfloat16

D_MODEL = 1024
GRID_W = 64
RET_HEADS = 4
RET_D = 128
RET_W = RET_HEADS * RET_D
ATT_HEADS = 8
ATT_KV = 2
ATT_DH = 64
ATT_W = ATT_HEADS * ATT_DH
ATT_BLOCK = 128
WINDOW = 128
ROPE_BASE = 10000.0
EPS = 1e-6
NEG = -1e30
LOG2E = 1.4426950408889634
IN_COLS = 4 * RET_W + 2 * ATT_W + 2 * ATT_KV * ATT_DH
OFF_RQ, OFF_RK, OFF_RV, OFF_RG = 0, RET_W, 2 * RET_W, 3 * RET_W
OFF_AQ = 4 * RET_W
OFF_AK = OFF_AQ + ATT_W
OFF_AV = OFF_AK + ATT_KV * ATT_DH
OFF_AG = OFF_AV + ATT_KV * ATT_DH

LANES = 128
SUBLANES = 8
VMEM_LIMIT = 56 * 1024 * 1024

MOD_COLS = 1536
PROJ_ROWS = 1024
PROJ_SUB = 256
RET_CHUNK = 256
RET_PAIR = 2
ATT_ROWS = 1024
ATT_AHEAD = 2


def _silu(t):
    return t / (1.0 + jnp.exp(-t))


def _log_sigmoid(t):
    return -(jnp.maximum(-t, 0.0) + jnp.log1p(jnp.exp(-jnp.abs(t))))


def _mod_kernel(c_ref, w_ref, b_ref, o_ref):
    n = c_ref.shape[0]
    s, w = _silu(c_ref[...]), w_ref[...]
    s_hi, w_hi = s.astype(BF16), w.astype(BF16)
    s_lo = (s - s_hi.astype(F32)).astype(BF16)
    w_lo = (w - w_hi.astype(F32)).astype(BF16)
    a = jnp.dot(jnp.concatenate([s_hi, s_lo], axis=0), w_hi, preferred_element_type=F32)
    o_ref[...] = a[0:n] + a[n:2 * n] + jnp.dot(s_hi, w_lo, preferred_element_type=F32) + b_ref[...]


def _modulation(cc, w_ada, b_ada):
    n = w_ada.shape[1]
    bn = MOD_COLS
    return pl.pallas_call(
        _mod_kernel,
        out_shape=jax.ShapeDtypeStruct((cc.shape[0], n), F32),
        grid=(n // bn,),
        in_specs=[pl.BlockSpec(cc.shape, lambda i: (0, 0)),
                  pl.BlockSpec((D_MODEL, bn), lambda i: (0, i)),
                  pl.BlockSpec((1, bn), lambda i: (0, i))],
        out_specs=pl.BlockSpec((cc.shape[0], bn), lambda i: (0, i)),
        compiler_params=pltpu.CompilerParams(dimension_semantics=("arbitrary",)),
        name="mod",
    )(cc, w_ada, b_ada)


def _norm_mod(x, m_ref):
    shift = m_ref[0, :, 0:D_MODEL]
    scale = m_ref[0, :, D_MODEL:2 * D_MODEL]
    h = x * lax.rsqrt(jnp.mean(x * x, axis=-1, keepdims=True) + EPS)
    return h * (1.0 + scale) + shift


def _rope_ret(t, cos, sin):
    return t * cos + pltpu.roll(t, 64, 1) * sin


def _rope_att(t, cos, sin, low_half):
    partner = jnp.where(low_half, pltpu.roll(t, 96, 1), pltpu.roll(t, 32, 1))
    return t * cos + partner * sin


def _store_kv_groups(k_ref, v_ref, rows, k, v):
    low = lax.broadcasted_iota(jnp.int32, k.shape, 1) < ATT_DH
    k_ref[0, 0, rows, :] = jnp.where(low, k, 0.0).astype(BF16)
    k_ref[0, 1, rows, :] = jnp.where(low, 0.0, k).astype(BF16)
    v_ref[0, 0, rows, :] = jnp.where(low, v, 1.0).astype(BF16)
    v_ref[0, 1, rows, :] = jnp.where(low, 1.0, v).astype(BF16)


def _pair_heads(nat):
    low = lax.broadcasted_iota(jnp.int32, nat[0].shape, 1) < ATT_DH
    out = []
    for j in range(len(nat)):
        a, b = nat[j // 2], nat[2 + j // 2]
        if j % 2 == 0:
            out.append(jnp.where(low, a, pltpu.roll(b, ATT_DH, 1)))
        else:
            out.append(jnp.where(low, pltpu.roll(a, ATT_DH, 1), b))
    return out


def _context_block(x_ref, m_ref, w_ref, dl_ref, sf_ref, sb_ref, ck_ref, cv_ref):
    lc = x_ref.shape[1]
    hb = _norm_mod(x_ref[0], m_ref).astype(BF16)
    yk = jnp.dot(hb, w_ref[:, OFF_RK:OFF_RV], preferred_element_type=F32)
    yv = jnp.dot(hb, w_ref[:, OFF_RV:OFF_RG], preferred_element_type=F32)
    ya = jnp.dot(hb, w_ref[:, OFF_AK:OFF_AG], preferred_element_type=F32)
    _store_kv_groups(ck_ref, cv_ref, slice(0, lc), ya[:, 0:LANES], ya[:, LANES:2 * LANES])
    lg = _log_sigmoid(dl_ref[...])
    pos = lax.broadcasted_iota(jnp.int32, (lc, 1), 0).astype(F32)
    dn = (((0,), (0,)), ((), ()))
    for h in range(RET_HEADS):
        k = yk[:, h * RET_D:(h + 1) * RET_D] * (RET_D ** -0.5)
        v = yv[:, h * RET_D:(h + 1) * RET_D].astype(BF16)
        wf = jnp.exp(lg[0, h, 0:1, :] * (lc - 1.0 - pos))
        wb = jnp.exp(lg[1, h, 0:1, :] * pos)
        sf_ref[0, h] = lax.dot_general(v, (k * wf).astype(BF16), dn, preferred_element_type=F32)
        sb_ref[0, h] = lax.dot_general(v, (k * wb).astype(BF16), dn, preferred_element_type=F32)


def _proj_kernel(x_ref, m_ref, w_ref, cr_ref, sr_ref, ca_ref, sa_ref, cx_ref, mc_ref, dl_ref,
                 r_ref, rvt_ref, aq_ref, ak_ref, av_ref, ag_ref, sf_ref, sb_ref, ck_ref, cv_ref,
                 w_scr):

    @pl.when((pl.program_id(0) == 0) & (pl.program_id(1) == 0))
    def _():
        for r in range(0, D_MODEL, LANES):
            rr = slice(r, r + LANES)
            for c0, c1 in ((OFF_RQ, OFF_RK), (OFF_RK, OFF_RV), (OFF_RV, OFF_RG), (OFF_RG, OFF_AQ),
                           (OFF_AK, OFF_AG)):
                w_scr[rr, c0:c1] = w_ref[rr, c0:c1].astype(BF16)
            for off in (OFF_AQ, OFF_AG):
                nat = [w_ref[rr, off + j * LANES:off + (j + 1) * LANES] for j in range(ATT_W // LANES)]
                for j, blk in enumerate(_pair_heads(nat)):
                    w_scr[rr, off + j * LANES:off + (j + 1) * LANES] = blk.astype(BF16)

    @pl.when(pl.program_id(1) == 0)
    def _():
        _context_block(cx_ref, mc_ref, w_scr, dl_ref, sf_ref, sb_ref, ck_ref, cv_ref)

    for t in range(x_ref.shape[1] // PROJ_SUB):
        rows = slice(t * PROJ_SUB, (t + 1) * PROJ_SUB)
        hb = _norm_mod(x_ref[0, rows, :], m_ref).astype(BF16)
        cr, sr, ca, sa = cr_ref[rows, :], sr_ref[rows, :], ca_ref[rows, :], sa_ref[rows, :]
        low_half = (lax.broadcasted_iota(jnp.int32, ca.shape, 1) % ATT_DH) < (ATT_DH // 2)

        def mm(c0, c1):
            return jnp.dot(hb, w_scr[:, c0:c1], preferred_element_type=F32)

        y = mm(OFF_RQ, OFF_RK)
        for h in range(RET_HEADS):
            sl = slice(h * LANES, (h + 1) * LANES)
            r_ref[0, rows, OFF_RQ + h * LANES:OFF_RQ + (h + 1) * LANES] = _rope_ret(y[:, sl], cr, sr).astype(BF16)
        y = mm(OFF_RK, OFF_RV) * (RET_D ** -0.5)
        for h in range(RET_HEADS):
            sl = slice(h * LANES, (h + 1) * LANES)
            r_ref[0, rows, OFF_RK + h * LANES:OFF_RK + (h + 1) * LANES] = _rope_ret(y[:, sl], cr, sr).astype(BF16)
        y = mm(OFF_RV, OFF_RG)
        for h in range(RET_HEADS):
            sl = slice(h * LANES, (h + 1) * LANES)
            rvt_ref[0, sl, rows] = y[:, sl].T.astype(BF16)
        r_ref[0, rows, 2 * RET_W:3 * RET_W] = _silu(mm(OFF_RG, OFF_AQ)).astype(BF16)
        y = mm(OFF_AQ, OFF_AK)
        for j in range(ATT_W // LANES):
            sl = slice(j * LANES, (j + 1) * LANES)
            aq_ref[0, rows, sl] = (_rope_att(y[:, sl], ca, sa, low_half) * (ATT_DH ** -0.5 * LOG2E)).astype(BF16)
        y = mm(OFF_AK, OFF_AG)
        _store_kv_groups(ak_ref, av_ref, rows, _rope_att(y[:, 0:LANES], ca, sa, low_half), y[:, LANES:2 * LANES])
        ag_ref[0, rows, :] = _silu(mm(OFF_AG, IN_COLS)).astype(BF16)


def _project(x, ctx, mod3, w_in_l, tabs, dl):
    B, L, _ = x.shape
    lc = ctx.shape[1]
    tm = PROJ_ROWS
    ctx_row = mod3.shape[0] // 2
    st_shape = jax.ShapeDtypeStruct((B, RET_HEADS, RET_D, RET_D), F32)
    st_spec = pl.BlockSpec((1, RET_HEADS, RET_D, RET_D), lambda b, i: (b, 0, 0, 0))
    ckv_shape = jax.ShapeDtypeStruct((B, ATT_KV, lc, LANES), BF16)
    ckv_spec = pl.BlockSpec((1, ATT_KV, lc, LANES), lambda b, i: (b, 0, 0, 0))
    tab_spec = pl.BlockSpec((tm, LANES), lambda b, i: (i, 0))
    row = lambda n: pl.BlockSpec((1, tm, n), lambda b, i: (b, i, 0))
    kv_spec = pl.BlockSpec((1, ATT_KV, tm, LANES), lambda b, i: (b, 0, i, 0))
    kv_shape = jax.ShapeDtypeStruct((B, ATT_KV, L, LANES), BF16)
    return pl.pallas_call(
        _proj_kernel,
        out_shape=(jax.ShapeDtypeStruct((B, L, 3 * RET_W), BF16),
                   jax.ShapeDtypeStruct((B, RET_W, L), BF16),
                   jax.ShapeDtypeStruct((B, L, ATT_W), BF16),
                   kv_shape, kv_shape,
                   jax.ShapeDtypeStruct((B, L, ATT_W), BF16),
                   st_shape, st_shape, ckv_shape, ckv_shape),
        grid=(B, L // tm),
        in_specs=[row(D_MODEL),
                  pl.BlockSpec((1, 1, 3 * D_MODEL), lambda b, i: (b, 0, 0)),
                  pl.BlockSpec((D_MODEL, IN_COLS), lambda b, i: (0, 0), pipeline_mode=pl.Buffered(1)),
                  tab_spec, tab_spec, tab_spec, tab_spec,
                  pl.BlockSpec((1, lc, D_MODEL), lambda b, i: (b, 0, 0)),
                  pl.BlockSpec((1, 1, 3 * D_MODEL), lambda b, i: (ctx_row, 0, 0)),
                  pl.BlockSpec(dl.shape, lambda b, i: (0, 0, 0, 0))],
        out_specs=(row(3 * RET_W), pl.BlockSpec((1, RET_W, tm), lambda b, i: (b, 0, i)),
                   row(ATT_W), kv_spec, kv_spec, row(ATT_W),
                   st_spec, st_spec, ckv_spec, ckv_spec),
        scratch_shapes=[pltpu.VMEM((D_MODEL, IN_COLS), BF16)],
        compiler_params=pltpu.CompilerParams(
            dimension_semantics=("arbitrary", "arbitrary"), vmem_limit_bytes=VMEM_LIMIT),
        name="proj",
    )(x, mod3, w_in_l, *tabs, ctx, mod3, dl)


def _ret_kernel(q_ref, k_ref, vt_ref, g_ref, sf_ref, sb_ref, dl_ref, gn_ref, o_ref,
                uf_scr, rb_scr, kdec_scr, qdec_scr, cdec_scr, decay_scr):
    L = q_ref.shape[1]
    C = RET_CHUNK
    nch = L // C
    heads = range(RET_PAIR)
    lanes = lambda h: slice(h * RET_D, (h + 1) * RET_D)

    @pl.when(pl.program_id(1) == 0)
    def _():
        lg = _log_sigmoid(dl_ref[...])
        pos = lax.broadcasted_iota(jnp.int32, (C, 1), 0).astype(F32)
        posl = lax.broadcasted_iota(jnp.int32, (SUBLANES, C), 1).astype(F32)
        diff = (lax.broadcasted_iota(jnp.int32, (C, C), 1)
                - lax.broadcasted_iota(jnp.int32, (C, C), 0)).astype(F32)
        for h in heads:
            lgf, lgb = lg[0, h, 0:1, :], lg[1, h, 0:1, :]
            lgf1, lgb1 = lgf[:, 0:1], lgb[:, 0:1]
            kdec_scr[h, 0] = jnp.exp(lgf * (C - 1.0 - pos))
            kdec_scr[h, 1] = jnp.exp(lgb * pos)
            qdec_scr[h, 0] = jnp.exp(lgf1 * (posl + 1.0))
            qdec_scr[h, 1] = jnp.exp(lgb1 * (C - posl))
            cdec_scr[h, 0] = jnp.broadcast_to(jnp.exp(lgf * C), (SUBLANES, LANES))
            cdec_scr[h, 1] = jnp.broadcast_to(jnp.exp(lgb * C), (SUBLANES, LANES))
            decay_scr[h] = jnp.where(diff >= 0.0, jnp.exp(lgf1 * jnp.maximum(diff, 0.0)),
                                     jnp.exp(lgb1 * jnp.maximum(-diff, 0.0)))

    nt = (((1,), (1,)), ((), ()))
    chunk = lambda n: slice(n * C, (n + 1) * C)

    rb = [sb_ref[0, h] for h in heads]
    for n in reversed(range(nch)):
        for h in heads:
            kc = k_ref[0, chunk(n), lanes(h)].astype(F32)
            kd = jnp.concatenate([(kc * kdec_scr[h, 0]).astype(BF16), (kc * kdec_scr[h, 1]).astype(BF16)],
                                 axis=1)
            u = jnp.dot(vt_ref[0, lanes(h), chunk(n)], kd, preferred_element_type=F32)
            uf_scr[h, n] = u[:, 0:RET_D]
            rb_scr[h, n] = rb[h].astype(BF16)
            rb[h] = rb[h] * cdec_scr[h, 1, 0:1, :] + u[:, RET_D:2 * RET_D]

    def kq(h, n):
        return lax.dot_general(k_ref[0, chunk(n), lanes(h)], q_ref[0, chunk(n), lanes(h)], nt,
                               preferred_element_type=F32)

    rf = [sf_ref[0, h] for h in heads]
    s_next = [kq(h, 0) for h in heads]
    for n in range(nch):
        for h in heads:
            s = s_next[h]
            q = q_ref[0, chunk(n), lanes(h)]
            states = jnp.concatenate([rf[h].astype(BF16), rb_scr[h, n]], axis=0)
            cross = lax.dot_general(states, q, nt, preferred_element_type=F32)
            if n + 1 < nch:
                s_next[h] = kq(h, n + 1)
                rf[h] = rf[h] * cdec_scr[h, 0, 0:1, :] + uf_scr[h, n]
            inner = jnp.dot(vt_ref[0, lanes(h), chunk(n)], (s * decay_scr[h]).astype(BF16),
                            preferred_element_type=F32)
            y = (inner + cross[0:RET_D] * qdec_scr[h, 0, 0:1, :]
                 + cross[RET_D:2 * RET_D] * qdec_scr[h, 1, 0:1, :])
            mu = jnp.mean(y, axis=0, keepdims=True)
            yc = y - mu
            var = jnp.mean(yc * yc, axis=0, keepdims=True)
            yn = (yc * lax.rsqrt(var + EPS)).T * gn_ref[h]
            o_ref[0, chunk(n), lanes(h)] = (yn * g_ref[0, chunk(n), lanes(h)].astype(F32)).astype(BF16)


def _retention(r, rvt, sf, sb, dl, gn):
    B, L, _ = r.shape
    P, C = RET_PAIR, RET_CHUNK
    col = lambda off: pl.BlockSpec((1, L, P * RET_D), lambda hp, b: (b, 0, off + hp))
    st_spec = pl.BlockSpec((1, P, RET_D, RET_D), lambda hp, b: (b, hp, 0, 0))
    npair = RET_HEADS // P
    return pl.pallas_call(
        _ret_kernel,
        out_shape=jax.ShapeDtypeStruct((B, L, RET_W), BF16),
        grid=(npair, B),
        in_specs=[col(0), col(npair),
                  pl.BlockSpec((1, P * RET_D, L), lambda hp, b: (b, hp, 0)),
                  col(2 * npair),
                  st_spec, st_spec,
                  pl.BlockSpec((2, P, SUBLANES, LANES), lambda hp, b: (0, hp, 0, 0)),
                  pl.BlockSpec((P, 1, RET_D), lambda hp, b: (hp, 0, 0))],
        out_specs=pl.BlockSpec((1, L, P * RET_D), lambda hp, b: (b, 0, hp)),
        scratch_shapes=[pltpu.VMEM((P, L // C, RET_D, RET_D), F32),
                        pltpu.VMEM((P, L // C, RET_D, RET_D), BF16),
                        pltpu.VMEM((P, 2, C, LANES), F32),
                        pltpu.VMEM((P, 2, SUBLANES, C), F32),
                        pltpu.VMEM((P, 2, SUBLANES, LANES), F32),
                        pltpu.VMEM((P, C, C), F32)],
        compiler_params=pltpu.CompilerParams(
            dimension_semantics=("arbitrary", "arbitrary"), vmem_limit_bytes=VMEM_LIMIT),
        name="ret",
    )(r, r, rvt, r, sf, sb, dl, gn)


def _att_kernel(q_ref, k_ref, v_ref, ck_ref, cv_ref, g_ref, sink_ref, r_ref, x_ref, m_ref, w_ref, fn_ref,
                o_ref, w_scr):
    @pl.when((pl.program_id(0) == 0) & (pl.program_id(1) == 0))
    def _():
        w_scr[0:RET_W, :] = w_ref[0:RET_W, :].astype(BF16)
        for j in range(ATT_HEADS // ATT_KV):
            for half in range(ATT_KV):
                dst = RET_W + j * LANES + half * ATT_DH
                src = RET_W + (j + half * (ATT_HEADS // ATT_KV)) * ATT_DH
                w_scr[dst:dst + ATT_DH, :] = w_ref[src:src + ATT_DH, :].astype(BF16)

    L = k_ref.shape[2]
    nb_total = L // ATT_BLOCK
    nb_step = q_ref.shape[1] // ATT_BLOCK
    nhb = ATT_W // LANES
    i = pl.program_id(1)
    r_i = lax.broadcasted_iota(jnp.int32, (ATT_BLOCK, ATT_BLOCK), 0)
    q_i = lax.broadcasted_iota(jnp.int32, (ATT_BLOCK, ATT_BLOCK), 1)
    band_prev = jnp.where(q_i <= r_i, 0.0, NEG).astype(F32)
    band_next = jnp.where(r_i <= q_i, 0.0, NEG).astype(F32)
    nt = (((1,), (1,)), ((), ()))
    tn = (((0,), (0,)), ((), ()))

    def key_rows(jb):
        n = i * nb_step + jb
        blk = lambda t: pl.ds(pl.multiple_of(t * ATT_BLOCK, ATT_BLOCK), ATT_BLOCK)
        return n, (blk(jnp.maximum(n - 1, 0)), blk(n), blk(jnp.minimum(n + 1, nb_total - 1)))

    def scores(jb, kv):
        n, (p_rows, o_rows, n_rows) = key_rows(jb)
        q = q_ref[0, jb * ATT_BLOCK:(jb + 1) * ATT_BLOCK, :]
        qall = jnp.concatenate([q[:, j * LANES:(j + 1) * LANES] for j in range(nhb)], axis=0)
        kall = jnp.concatenate([ck_ref[0, kv], k_ref[0, kv, p_rows, :], k_ref[0, kv, o_rows, :],
                                k_ref[0, kv, n_rows, :]], axis=0)
        return lax.dot_general(kall, qall, nt, preferred_element_type=F32)

    def attend(jb, kv, s):
        n, (p_rows, o_rows, n_rows) = key_rows(jb)
        lc = ck_ref.shape[2]
        bias_prev = jnp.concatenate([band_prev + jnp.where(n == 0, NEG, 0.0)] * nhb, axis=1)
        bias_next = jnp.concatenate([band_next + jnp.where(n == nb_total - 1, NEG, 0.0)] * nhb, axis=1)
        parts = [s[0:lc], s[lc:lc + ATT_BLOCK] + bias_prev, s[lc + ATT_BLOCK:lc + 2 * ATT_BLOCK],
                 s[lc + 2 * ATT_BLOCK:] + bias_next]
        sink = sink_ref[kv:kv + 1, :] * LOG2E
        m = sink
        for t in parts:
            m = jnp.maximum(m, jnp.max(t, axis=0, keepdims=True))
        p = jnp.concatenate([jnp.exp2(t - m).astype(BF16) for t in parts], axis=0)
        vaug = jnp.concatenate([cv_ref[0, kv], v_ref[0, kv, p_rows, :], v_ref[0, kv, o_rows, :],
                                v_ref[0, kv, n_rows, :]], axis=0)
        o = lax.dot_general(vaug, p, tn, preferred_element_type=F32)
        if kv == 0:
            val, ones = slice(0, ATT_DH), slice(ATT_DH, ATT_DH + 1)
        else:
            val, ones = slice(ATT_DH, 2 * ATT_DH), slice(0, 1)
        den = o[ones] + jnp.exp2(sink - m)
        return o[val] * (1.0 / den)

    def gated_block(jb, halves):
        rows = slice(jb * ATT_BLOCK, (jb + 1) * ATT_BLOCK)
        comb = jnp.concatenate(halves, axis=0)
        return jnp.concatenate(
            [(comb[:, j * LANES:(j + 1) * LANES].T
              * g_ref[0, rows, j * LANES:(j + 1) * LANES].astype(F32)).astype(BF16) for j in range(nhb)], axis=1)

    def finish(jb, att_blk):
        rows = slice(jb * ATT_BLOCK, (jb + 1) * ATT_BLOCK)
        mix_in = jnp.concatenate([r_ref[0, rows, :], att_blk], axis=1)
        mixed = jnp.dot(mix_in, w_scr[...], preferred_element_type=F32)
        xn = x_ref[0, rows, :] + m_ref[0, :, 2 * D_MODEL:3 * D_MODEL] * mixed
        o_ref[0, rows, :] = xn * lax.rsqrt(jnp.mean(xn * xn, axis=-1, keepdims=True) + EPS) * fn_ref[...]

    chains = [(jb, kv) for jb in range(nb_step) for kv in range(ATT_KV)]
    ahead = [scores(*chains[t]) for t in range(ATT_AHEAD)]
    halves, pending = [], []
    for c, (jb, kv) in enumerate(chains):
        s = ahead.pop(0)
        if c + ATT_AHEAD < len(chains):
            ahead.append(scores(*chains[c + ATT_AHEAD]))
        halves.append(attend(jb, kv, s))
        if pending:
            finish(*pending.pop())
        if kv == ATT_KV - 1:
            pending.append((jb, gated_block(jb, halves)))
            halves = []
    finish(*pending.pop())


def _attention_output(aq, ak, av, ck, cv, ag, sink2, ret, x, mod3, w_out_l, fnw):
    B, L, _ = aq.shape
    lc = ck.shape[2]
    tq = ATT_ROWS
    row = lambda n: pl.BlockSpec((1, tq, n), lambda b, i: (b, i, 0))
    full = lambda n: pl.BlockSpec((1, ATT_KV, n, LANES), lambda b, i: (b, 0, 0, 0))
    return pl.pallas_call(
        _att_kernel,
        out_shape=jax.ShapeDtypeStruct((B, L, D_MODEL), F32),
        grid=(B, L // tq),
        in_specs=[row(ATT_W), full(L), full(L), full(lc), full(lc), row(ATT_W),
                  pl.BlockSpec(sink2.shape, lambda b, i: (0, 0)),
                  row(RET_W), row(D_MODEL),
                  pl.BlockSpec((1, 1, 3 * D_MODEL), lambda b, i: (b, 0, 0)),
                  pl.BlockSpec((RET_W + ATT_W, D_MODEL), lambda b, i: (0, 0), pipeline_mode=pl.Buffered(1)),
                  pl.BlockSpec((1, D_MODEL), lambda b, i: (0, 0))],
        out_specs=row(D_MODEL),
        scratch_shapes=[pltpu.VMEM((RET_W + ATT_W, D_MODEL), BF16)],
        compiler_params=pltpu.CompilerParams(
            dimension_semantics=("arbitrary", "arbitrary"), vmem_limit_bytes=VMEM_LIMIT),
        name="att",
    )(aq, ak, av, ck, cv, ag, sink2, ret, x, mod3, w_out_l, fnw)


def _rope_tables(L):
    pos = np.arange(L)
    rows, cols = (pos // GRID_W).astype(np.float64), (pos % GRID_W).astype(np.float64)

    def tables(dh):
        nf = dh // 4
        inv = ROPE_BASE ** (-np.arange(nf, dtype=np.float64) / nf)
        ang = np.concatenate([rows[:, None] * inv, cols[:, None] * inv], axis=-1)
        cos, sin = np.cos(ang), np.sin(ang)
        reps = LANES // dh
        return (jnp.asarray(np.tile(np.concatenate([cos, cos], axis=-1), (1, reps)), F32),
                jnp.asarray(np.tile(np.concatenate([-sin, sin], axis=-1), (1, reps)), F32))

    cr, sr = tables(RET_D)
    ca, sa = tables(ATT_DH)
    return cr, sr, ca, sa


def kernel(x, c, ctx, c_ctx, w_ada, b_ada, w_in, ret_decay_logit, ret_gn_w, att_sink, w_out, final_norm_w):
    B, L, _ = x.shape
    assert w_ada.shape[0] == 1, "single-layer trunk"

    cc = jnp.concatenate([c, c_ctx[None, :], jnp.zeros((B - 1, D_MODEL), F32)], axis=0)
    mod3 = _modulation(cc, w_ada[0], b_ada[0][None, :]).reshape(2 * B, 1, 3 * D_MODEL)

    dl = jnp.broadcast_to(ret_decay_logit[0][:, :, None, None], (2, RET_HEADS, SUBLANES, LANES))

    r, rvt, aq, ak, av, ag, sf, sb, ck, cv = _project(x, ctx, mod3, w_in[0], _rope_tables(L), dl)
    ret = _retention(r, rvt, sf, sb, dl, ret_gn_w[0].reshape(RET_HEADS, 1, RET_D))

    sink2 = jnp.repeat(att_sink[0].reshape(ATT_KV, ATT_HEADS // ATT_KV), LANES, axis=1)
    return _attention_output(aq, ak, av, ck, cv, ag, sink2, ret, x, mod3, w_out[0], final_norm_w[None, :])
```

```python
import numpy as np
import jax
import jax.numpy as jnp
from jax import lax
from jax.experimental import pallas as pl
from jax.experimental.pallas import tpu as pltpu

F32 = jnp.float32
BF16 = jnp.bfloat16

D_MODEL = 1024
GRID_W = 64
RET_HEADS = 4
RET_D = 128
RET_W = RET_HEADS * RET_D
ATT_HEADS = 8
ATT_KV = 2
ATT_DH = 64
ATT_W = ATT_HEADS * ATT_DH
ATT_BLOCK = 128
WINDOW = 128
ROPE_BASE = 10000.0
EPS = 1e-6
NEG = -1e30
LOG2E = 1.4426950408889634
IN_COLS = 4 * RET_W + 2 * ATT_W + 2 * ATT_KV * ATT_DH
OFF_RQ, OFF_RK, OFF_RV, OFF_RG = 0, RET_W, 2 * RET_W, 3 * RET_W
OFF_AQ = 4 * RET_W
OFF_AK = OFF_AQ + ATT_W
OFF_AV = OFF_AK + ATT_KV * ATT_DH
OFF_AG = OFF_AV + ATT_KV * ATT_DH

LANES = 128
SUBLANES = 8
VMEM_LIMIT = 56 * 1024 * 1024

MOD_COLS = 1536
PROJ_ROWS = 1024
PROJ_SUB = 256
RET_CHUNK = 256
RET_PAIR = 2
ATT_ROWS = 1024
ATT_AHEAD = 2


def _silu(t):
    return t / (1.0 + jnp.exp(-t))


def _log_sigmoid(t):
    return -(jnp.maximum(-t, 0.0) + jnp.log1p(jnp.exp(-jnp.abs(t))))


def _mod_kernel(c_ref, cx_ref, w_ref, b_ref, o_ref):
    n = o_ref.shape[0]
    cc = jnp.concatenate([c_ref[...], jnp.broadcast_to(cx_ref[...], (n - c_ref.shape[0], D_MODEL))], axis=0)
    s, w = _silu(cc), w_ref[...]
    s_hi, w_hi = s.astype(BF16), w.astype(BF16)
    s_lo = (s - s_hi.astype(F32)).astype(BF16)
    w_lo = (w - w_hi.astype(F32)).astype(BF16)
    a = jnp.dot(jnp.concatenate([s_hi, s_lo], axis=0), w_hi, preferred_element_type=F32)
    o_ref[:, 0, :] = a[0:n] + a[n:2 * n] + jnp.dot(s_hi, w_lo, preferred_element_type=F32) + b_ref[...]


def _modulation(c, c_ctx, w_ada, b_ada):
    B, n = c.shape[0], w_ada.shape[1]
    bn = MOD_COLS
    return pl.pallas_call(
        _mod_kernel,
        out_shape=jax.ShapeDtypeStruct((2 * B, 1, n), F32),
        grid=(n // bn,),
        in_specs=[pl.BlockSpec(c.shape, lambda i: (0, 0)),
                  pl.BlockSpec((1, D_MODEL), lambda i: (0, 0)),
                  pl.BlockSpec((D_MODEL, bn), lambda i: (0, i)),
                  pl.BlockSpec((1, bn), lambda i: (0, i))],
        out_specs=pl.BlockSpec((2 * B, 1, bn), lambda i: (0, 0, i)),
        compiler_params=pltpu.CompilerParams(dimension_semantics=("arbitrary",)),
        name="mod",
    )(c, c_ctx[None, :], w_ada, b_ada)


def _norm_mod(x, m_ref):
    shift = m_ref[0, :, 0:D_MODEL]
    scale = m_ref[0, :, D_MODEL:2 * D_MODEL]
    h = x * lax.rsqrt(jnp.mean(x * x, axis=-1, keepdims=True) + EPS)
    return h * (1.0 + scale) + shift


def _rope_ret(t, cos, sin):
    return t * cos + pltpu.roll(t, 64, 1) * sin


def _rope_att(t, cos, sin, low_half):
    partner = jnp.where(low_half, pltpu.roll(t, 96, 1), pltpu.roll(t, 32, 1))
    return t * cos + partner * sin


def _store_kv_groups(k_ref, v_ref, rows, k, v):
    low = lax.broadcasted_iota(jnp.int32, k.shape, 1) < ATT_DH
    k_ref[0, 0, rows, :] = jnp.where(low, k, 0.0).astype(BF16)
    k_ref[0, 1, rows, :] = jnp.where(low, 0.0, k).astype(BF16)
    v_ref[0, 0, rows, :] = jnp.where(low, v, 1.0).astype(BF16)
    v_ref[0, 1, rows, :] = jnp.where(low, 1.0, v).astype(BF16)


def _pair_heads(nat):
    low = lax.broadcasted_iota(jnp.int32, nat[0].shape, 1) < ATT_DH
    out = []
    for j in range(len(nat)):
        a, b = nat[j // 2], nat[2 + j // 2]
        if j % 2 == 0:
            out.append(jnp.where(low, a, pltpu.roll(b, ATT_DH, 1)))
        else:
            out.append(jnp.where(low, pltpu.roll(a, ATT_DH, 1), b))
    return out


def _context_block(x_ref, m_ref, w_ref, dl_ref, sf_ref, sb_ref, ck_ref, cv_ref):
    lc = x_ref.shape[1]
    hb = _norm_mod(x_ref[0], m_ref).astype(BF16)
    yk = jnp.dot(hb, w_ref[:, OFF_RK:OFF_RV], preferred_element_type=F32)
    yv = jnp.dot(hb, w_ref[:, OFF_RV:OFF_RG], preferred_element_type=F32)
    ya = jnp.dot(hb, w_ref[:, OFF_AK:OFF_AG], preferred_element_type=F32)
    _store_kv_groups(ck_ref, cv_ref, slice(0, lc), ya[:, 0:LANES], ya[:, LANES:2 * LANES])
    lg = _log_sigmoid(dl_ref[...])
    pos = lax.broadcasted_iota(jnp.int32, (lc, 1), 0).astype(F32)
    dn = (((0,), (0,)), ((), ()))
    for h in range(RET_HEADS):
        k = yk[:, h * RET_D:(h + 1) * RET_D] * (RET_D ** -0.5)
        v = yv[:, h * RET_D:(h + 1) * RET_D].astype(BF16)
        wf = jnp.exp(lg[0, h, 0:1, :] * (lc - 1.0 - pos))
        wb = jnp.exp(lg[1, h, 0:1, :] * pos)
        sf_ref[0, h] = lax.dot_general(v, (k * wf).astype(BF16), dn, preferred_element_type=F32)
        sb_ref[0, h] = lax.dot_general(v, (k * wb).astype(BF16), dn, preferred_element_type=F32)


def _proj_kernel(x_ref, m_ref, w_ref, cr_ref, sr_ref, ca_ref, sa_ref, cx_ref, mc_ref, dl_ref,
                 r_ref, rvt_ref, aq_ref, ak_ref, av_ref, ag_ref, sf_ref, sb_ref, ck_ref, cv_ref,
                 w_scr):

    @pl.when((pl.program_id(0) == 0) & (pl.program_id(1) == 0))
    def _():
        for r in range(0, D_MODEL, LANES):
            rr = slice(r, r + LANES)
            for c0, c1 in ((OFF_RQ, OFF_RK), (OFF_RK, OFF_RV), (OFF_RV, OFF_RG), (OFF_RG, OFF_AQ),
                           (OFF_AK, OFF_AG)):
                w_scr[rr, c0:c1] = w_ref[rr, c0:c1].astype(BF16)
            for off in (OFF_AQ, OFF_AG):
                nat = [w_ref[rr, off + j * LANES:off + (j + 1) * LANES] for j in range(ATT_W // LANES)]
                for j, blk in enumerate(_pair_heads(nat)):
                    w_scr[rr, off + j * LANES:off + (j + 1) * LANES] = blk.astype(BF16)

    @pl.when(pl.program_id(1) == 0)
    def _():
        _context_block(cx_ref, mc_ref, w_scr, dl_ref, sf_ref, sb_ref, ck_ref, cv_ref)

    for t in range(x_ref.shape[1] // PROJ_SUB):
        rows = slice(t * PROJ_SUB, (t + 1) * PROJ_SUB)
        hb = _norm_mod(x_ref[0, rows, :], m_ref).astype(BF16)
        cr, sr, ca, sa = cr_ref[rows, :], sr_ref[rows, :], ca_ref[rows, :], sa_ref[rows, :]
        low_half = (lax.broadcasted_iota(jnp.int32, ca.shape, 1) % ATT_DH) < (ATT_DH // 2)

        def mm(c0, c1):
            return jnp.dot(hb, w_scr[:, c0:c1], preferred_element_type=F32)

        y = mm(OFF_RQ, OFF_RK)
        for h in range(RET_HEADS):
            sl = slice(h * LANES, (h + 1) * LANES)
            r_ref[0, rows, OFF_RQ + h * LANES:OFF_RQ + (h + 1) * LANES] = _rope_ret(y[:, sl], cr, sr).astype(BF16)
        y = mm(OFF_RK, OFF_RV) * (RET_D ** -0.5)
        for h in range(RET_HEADS):
            sl = slice(h * LANES, (h + 1) * LANES)
            r_ref[0, rows, OFF_RK + h * LANES:OFF_RK + (h + 1) * LANES] = _rope_ret(y[:, sl], cr, sr).astype(BF16)
        y = mm(OFF_RV, OFF_RG)
        for h in range(RET_HEADS):
            sl = slice(h * LANES, (h + 1) * LANES)
            rvt_ref[0, sl, rows] = y[:, sl].T.astype(BF16)
        r_ref[0, rows, 2 * RET_W:3 * RET_W] = _silu(mm(OFF_RG, OFF_AQ)).astype(BF16)
        y = mm(OFF_AQ, OFF_AK)
        for j in range(ATT_W // LANES):
            sl = slice(j * LANES, (j + 1) * LANES)
            aq_ref[0, rows, sl] = (_rope_att(y[:, sl], ca, sa, low_half) * (ATT_DH ** -0.5 * LOG2E)).astype(BF16)
        y = mm(OFF_AK, OFF_AG)
        _store_kv_groups(ak_ref, av_ref, rows, _rope_att(y[:, 0:LANES], ca, sa, low_half), y[:, LANES:2 * LANES])
        ag_ref[0, rows, :] = _silu(mm(OFF_AG, IN_COLS)).astype(BF16)


def _project(x, ctx, mod3, w_in_l, tabs, dl):
    B, L, _ = x.shape
    lc = ctx.shape[1]
    tm = PROJ_ROWS
    ctx_row = mod3.shape[0] // 2
    st_shape = jax.ShapeDtypeStruct((B, RET_HEADS, RET_D, RET_D), F32)
    st_spec = pl.BlockSpec((1, RET_HEADS, RET_D, RET_D), lambda b, i: (b, 0, 0, 0))
    ckv_shape = jax.ShapeDtypeStruct((B, ATT_KV, lc, LANES), BF16)
    ckv_spec = pl.BlockSpec((1, ATT_KV, lc, LANES), lambda b, i: (b, 0, 0, 0))
    tab_spec = pl.BlockSpec((tm, LANES), lambda b, i: (i, 0))
    row = lambda n: pl.BlockSpec((1, tm, n), lambda b, i: (b, i, 0))
    kv_spec = pl.BlockSpec((1, ATT_KV, tm, LANES), lambda b, i: (b, 0, i, 0))
    kv_shape = jax.ShapeDtypeStruct((B, ATT_KV, L, LANES), BF16)
    return pl.pallas_call(
        _proj_kernel,
        out_shape=(jax.ShapeDtypeStruct((B, L, 3 * RET_W), BF16),
                   jax.ShapeDtypeStruct((B, RET_W, L), BF16),
                   jax.ShapeDtypeStruct((B, L, ATT_W), BF16),
                   kv_shape, kv_shape,
                   jax.ShapeDtypeStruct((B, L, ATT_W), BF16),
                   st_shape, st_shape, ckv_shape, ckv_shape),
        grid=(B, L // tm),
        in_specs=[row(D_MODEL),
                  pl.BlockSpec((1, 1, 3 * D_MODEL), lambda b, i: (b, 0, 0)),
                  pl.BlockSpec((D_MODEL, IN_COLS), lambda b, i: (0, 0), pipeline_mode=pl.Buffered(1)),
                  tab_spec, tab_spec, tab_spec, tab_spec,
                  pl.BlockSpec((1, lc, D_MODEL), lambda b, i: (b, 0, 0)),
                  pl.BlockSpec((1, 1, 3 * D_MODEL), lambda b, i: (ctx_row, 0, 0)),
                  pl.BlockSpec(dl.shape, lambda b, i: (0, 0, 0, 0))],
        out_specs=(row(3 * RET_W), pl.BlockSpec((1, RET_W, tm), lambda b, i: (b, 0, i)),
                   row(ATT_W), kv_spec, kv_spec, row(ATT_W),
                   st_spec, st_spec, ckv_spec, ckv_spec),
        scratch_shapes=[pltpu.VMEM((D_MODEL, IN_COLS), BF16)],
        compiler_params=pltpu.CompilerParams(
            dimension_semantics=("arbitrary", "arbitrary"), vmem_limit_bytes=VMEM_LIMIT),
        name="proj",
    )(x, mod3, w_in_l, *tabs, ctx, mod3, dl)


def _ret_kernel(q_ref, k_ref, vt_ref, g_ref, sf_ref, sb_ref, dl_ref, gn_ref, o_ref,
                uf_scr, rb_scr, kdec_scr, qdec_scr, cdec_scr, decay_scr):
    L = q_ref.shape[1]
    C = RET_CHUNK
    nch = L // C
    heads = range(RET_PAIR)
    lanes = lambda h: slice(h * RET_D, (h + 1) * RET_D)

    @pl.when(pl.program_id(1) == 0)
    def _():
        lg = _log_sigmoid(dl_ref[...])
        pos = lax.broadcasted_iota(jnp.int32, (C, 1), 0).astype(F32)
        posl = lax.broadcasted_iota(jnp.int32, (SUBLANES, C), 1).astype(F32)
        diff = (lax.broadcasted_iota(jnp.int32, (C, C), 1)
                - lax.broadcasted_iota(jnp.int32, (C, C), 0)).astype(F32)
        for h in heads:
            lgf, lgb = lg[0, h, 0:1, :], lg[1, h, 0:1, :]
            lgf1, lgb1 = lgf[:, 0:1], lgb[:, 0:1]
            kdec_scr[h, 0] = jnp.exp(lgf * (C - 1.0 - pos))
            kdec_scr[h, 1] = jnp.exp(lgb * pos)
            qdec_scr[h, 0] = jnp.exp(lgf1 * (posl + 1.0))
            qdec_scr[h, 1] = jnp.exp(lgb1 * (C - posl))
            cdec_scr[h, 0] = jnp.broadcast_to(jnp.exp(lgf * C), (SUBLANES, LANES))
            cdec_scr[h, 1] = jnp.broadcast_to(jnp.exp(lgb * C), (SUBLANES, LANES))
            decay_scr[h] = jnp.where(diff >= 0.0, jnp.exp(lgf1 * jnp.maximum(diff, 0.0)),
                                     jnp.exp(lgb1 * jnp.maximum(-diff, 0.0)))

    nt = (((1,), (1,)), ((), ()))
    chunk = lambda n: slice(n * C, (n + 1) * C)

    rb = [sb_ref[0, h] for h in heads]
    for n in reversed(range(nch)):
        for h in heads:
            kc = k_ref[0, chunk(n), lanes(h)].astype(F32)
            kd = jnp.concatenate([(kc * kdec_scr[h, 0]).astype(BF16), (kc * kdec_scr[h, 1]).astype(BF16)],
                                 axis=1)
            u = jnp.dot(vt_ref[0, lanes(h), chunk(n)], kd, preferred_element_type=F32)
            uf_scr[h, n] = u[:, 0:RET_D]
            rb_scr[h, n] = rb[h].astype(BF16)
            rb[h] = rb[h] * cdec_scr[h, 1, 0:1, :] + u[:, RET_D:2 * RET_D]

    def kq(h, n):
        return lax.dot_general(k_ref[0, chunk(n), lanes(h)], q_ref[0, chunk(n), lanes(h)], nt,
                               preferred_element_type=F32)

    rf = [sf_ref[0, h] for h in heads]
    s_next = [kq(h, 0) for h in heads]
    for n in range(nch):
        for h in heads:
            s = s_next[h]
            q = q_ref[0, chunk(n), lanes(h)]
            states = jnp.concatenate([rf[h].astype(BF16), rb_scr[h, n]], axis=0)
            cross = lax.dot_general(states, q, nt, preferred_element_type=F32)
            if n + 1 < nch:
                s_next[h] = kq(h, n + 1)
                rf[h] = rf[h] * cdec_scr[h, 0, 0:1, :] + uf_scr[h, n]
            inner = jnp.dot(vt_ref[0, lanes(h), chunk(n)], (s * decay_scr[h]).astype(BF16),
                            preferred_element_type=F32)
            y = (inner + cross[0:RET_D] * qdec_scr[h, 0, 0:1, :]
                 + cross[RET_D:2 * RET_D] * qdec_scr[h, 1, 0:1, :])
            mu = jnp.mean(y, axis=0, keepdims=True)
            yc = y - mu
            var = jnp.mean(yc * yc, axis=0, keepdims=True)
            yn = (yc * lax.rsqrt(var + EPS)).T * gn_ref[h]
            o_ref[0, chunk(n), lanes(h)] = (yn * g_ref[0, chunk(n), lanes(h)].astype(F32)).astype(BF16)


def _retention(r, rvt, sf, sb, dl, gn):
    B, L, _ = r.shape
    P, C = RET_PAIR, RET_CHUNK
    col = lambda off: pl.BlockSpec((1, L, P * RET_D), lambda hp, b: (b, 0, off + hp))
    st_spec = pl.BlockSpec((1, P, RET_D, RET_D), lambda hp, b: (b, hp, 0, 0))
    npair = RET_HEADS // P
    return pl.pallas_call(
        _ret_kernel,
        out_shape=jax.ShapeDtypeStruct((B, L, RET_W), BF16),
        grid=(npair, B),
        in_specs=[col(0), col(npair),
                  pl.BlockSpec((1, P * RET_D, L), lambda hp, b: (b, hp, 0)),
                  col(2 * npair),
                  st_spec, st_spec,
                  pl.BlockSpec((2, P, SUBLANES, LANES), lambda hp, b: (0, hp, 0, 0)),
                  pl.BlockSpec((P, 1, RET_D), lambda hp, b: (hp, 0, 0))],
        out_specs=pl.BlockSpec((1, L, P * RET_D), lambda hp, b: (b, 0, hp)),
        scratch_shapes=[pltpu.VMEM((P, L // C, RET_D, RET_D), F32),
                        pltpu.VMEM((P, L // C, RET_D, RET_D), BF16),
                        pltpu.VMEM((P, 2, C, LANES), F32),
                        pltpu.VMEM((P, 2, SUBLANES, C), F32),
                        pltpu.VMEM((P, 2, SUBLANES, LANES), F32),
                        pltpu.VMEM((P, C, C), F32)],
        compiler_params=pltpu.CompilerParams(
            dimension_semantics=("arbitrary", "arbitrary"), vmem_limit_bytes=VMEM_LIMIT),
        name="ret",
    )(r, r, rvt, r, sf, sb, dl, gn)


def _att_kernel(q_ref, k_ref, v_ref, ck_ref, cv_ref, g_ref, sink_ref, r_ref, x_ref, m_ref, w_ref, fn_ref,
                o_ref, w_scr):
    @pl.when((pl.program_id(0) == 0) & (pl.program_id(1) == 0))
    def _():
        w_scr[0:RET_W, :] = w_ref[0:RET_W, :].astype(BF16)
        for j in range(ATT_HEADS // ATT_KV):
            for half in range(ATT_KV):
                dst = RET_W + j * LANES + half * ATT_DH
                src = RET_W + (j + half * (ATT_HEADS // ATT_KV)) * ATT_DH
                w_scr[dst:dst + ATT_DH, :] = w_ref[src:src + ATT_DH, :].astype(BF16)

    L = k_ref.shape[2]
    nb_total = L // ATT_BLOCK
    nb_step = q_ref.shape[1] // ATT_BLOCK
    nhb = ATT_W // LANES
    i = pl.program_id(1)
    r_i = lax.broadcasted_iota(jnp.int32, (ATT_BLOCK, ATT_BLOCK), 0)
    q_i = lax.broadcasted_iota(jnp.int32, (ATT_BLOCK, ATT_BLOCK), 1)
    band_prev = jnp.where(q_i <= r_i, 0.0, NEG).astype(F32)
    band_next = jnp.where(r_i <= q_i, 0.0, NEG).astype(F32)
    nt = (((1,), (1,)), ((), ()))
    tn = (((0,), (0,)), ((), ()))

    def key_rows(jb):
        n = i * nb_step + jb
        blk = lambda t: pl.ds(pl.multiple_of(t * ATT_BLOCK, ATT_BLOCK), ATT_BLOCK)
        return n, (blk(jnp.maximum(n - 1, 0)), blk(n), blk(jnp.minimum(n + 1, nb_total - 1)))

    def scores(jb, kv):
        n, (p_rows, o_rows, n_rows) = key_rows(jb)
        q = q_ref[0, jb * ATT_BLOCK:(jb + 1) * ATT_BLOCK, :]
        qall = jnp.concatenate([q[:, j * LANES:(j + 1) * LANES] for j in range(nhb)], axis=0)
        kall = jnp.concatenate([ck_ref[0, kv], k_ref[0, kv, p_rows, :], k_ref[0, kv, o_rows, :],
                                k_ref[0, kv, n_rows, :]], axis=0)
        return lax.dot_general(kall, qall, nt, preferred_element_type=F32)

    def attend(jb, kv, s):
        n, (p_rows, o_rows, n_rows) = key_rows(jb)
        lc = ck_ref.shape[2]
        bias_prev = jnp.concatenate([band_prev + jnp.where(n == 0, NEG, 0.0)] * nhb, axis=1)
        bias_next = jnp.concatenate([band_next + jnp.where(n == nb_total - 1, NEG, 0.0)] * nhb, axis=1)
        parts = [s[0:lc], s[lc:lc + ATT_BLOCK] + bias_prev, s[lc + ATT_BLOCK:lc + 2 * ATT_BLOCK],
                 s[lc + 2 * ATT_BLOCK:] + bias_next]
        sink = sink_ref[kv:kv + 1, :] * LOG2E
        m = sink
        for t in parts:
            m = jnp.maximum(m, jnp.max(t, axis=0, keepdims=True))
        p = jnp.concatenate([jnp.exp2(t - m).astype(BF16) for t in parts], axis=0)
        vaug = jnp.concatenate([cv_ref[0, kv], v_ref[0, kv, p_rows, :], v_ref[0, kv, o_rows, :],
                                v_ref[0, kv, n_rows, :]], axis=0)
        o = lax.dot_general(vaug, p, tn, preferred_element_type=F32)
        if kv == 0:
            val, ones = slice(0, ATT_DH), slice(ATT_DH, ATT_DH + 1)
        else:
            val, ones = slice(ATT_DH, 2 * ATT_DH), slice(0, 1)
        den = o[ones] + jnp.exp2(sink - m)
        return o[val] * (1.0 / den)

    def gated_block(jb, halves):
        rows = slice(jb * ATT_BLOCK, (jb + 1) * ATT_BLOCK)
        comb = jnp.concatenate(halves, axis=0)
        return jnp.concatenate(
            [(comb[:, j * LANES:(j + 1) * LANES].T
              * g_ref[0, rows, j * LANES:(j + 1) * LANES].astype(F32)).astype(BF16) for j in range(nhb)], axis=1)

    def finish(jb, att_blk):
        rows = slice(jb * ATT_BLOCK, (jb + 1) * ATT_BLOCK)
        mix_in = jnp.concatenate([r_ref[0, rows, :], att_blk], axis=1)
        mixed = jnp.dot(mix_in, w_scr[...], preferred_element_type=F32)
        xn = x_ref[0, rows, :] + m_ref[0, :, 2 * D_MODEL:3 * D_MODEL] * mixed
        o_ref[0, rows, :] = xn * lax.rsqrt(jnp.mean(xn * xn, axis=-1, keepdims=True) + EPS) * fn_ref[...]

    chains = [(jb, kv) for jb in range(nb_step) for kv in range(ATT_KV)]
    ahead = [scores(*chains[t]) for t in range(ATT_AHEAD)]
    halves, pending = [], []
    for c, (jb, kv) in enumerate(chains):
        s = ahead.pop(0)
        if c + ATT_AHEAD < len(chains):
            ahead.append(scores(*chains[c + ATT_AHEAD]))
        halves.append(attend(jb, kv, s))
        if pending:
            finish(*pending.pop())
        if kv == ATT_KV - 1:
            pending.append((jb, gated_block(jb, halves)))
            halves = []
    finish(*pending.pop())


def _attention_output(aq, ak, av, ck, cv, ag, sink2, ret, x, mod3, w_out_l, fnw):
    B, L, _ = aq.shape
    lc = ck.shape[2]
    tq = ATT_ROWS
    row = lambda n: pl.BlockSpec((1, tq, n), lambda b, i: (b, i, 0))
    full = lambda n: pl.BlockSpec((1, ATT_KV, n, LANES), lambda b, i: (b, 0, 0, 0))
    return pl.pallas_call(
        _att_kernel,
        out_shape=jax.ShapeDtypeStruct((B, L, D_MODEL), F32),
        grid=(B, L // tq),
        in_specs=[row(ATT_W), full(L), full(L), full(lc), full(lc), row(ATT_W),
                  pl.BlockSpec(sink2.shape, lambda b, i: (0, 0)),
                  row(RET_W), row(D_MODEL),
                  pl.BlockSpec((1, 1, 3 * D_MODEL), lambda b, i: (b, 0, 0)),
                  pl.BlockSpec((RET_W + ATT_W, D_MODEL), lambda b, i: (0, 0), pipeline_mode=pl.Buffered(1)),
                  pl.BlockSpec((1, D_MODEL), lambda b, i: (0, 0))],
        out_specs=row(D_MODEL),
        scratch_shapes=[pltpu.VMEM((RET_W + ATT_W, D_MODEL), BF16)],
        compiler_params=pltpu.CompilerParams(
            dimension_semantics=("arbitrary", "arbitrary"), vmem_limit_bytes=VMEM_LIMIT),
        name="att",
    )(aq, ak, av, ck, cv, ag, sink2, ret, x, mod3, w_out_l, fnw)


def _rope_tables(L):
    pos = np.arange(L)
    rows, cols = (pos // GRID_W).astype(np.float64), (pos % GRID_W).astype(np.float64)

    def tables(dh):
        nf = dh // 4
        inv = ROPE_BASE ** (-np.arange(nf, dtype=np.float64) / nf)
        ang = np.concatenate([rows[:, None] * inv, cols[:, None] * inv], axis=-1)
        cos, sin = np.cos(ang), np.sin(ang)
        reps = LANES // dh
        return (jnp.asarray(np.tile(np.concatenate([cos, cos], axis=-1), (1, reps)), F32),
                jnp.asarray(np.tile(np.concatenate([-sin, sin], axis=-1), (1, reps)), F32))

    cr, sr = tables(RET_D)
    ca, sa = tables(ATT_DH)
    return cr, sr, ca, sa


def kernel(x, c, ctx, c_ctx, w_ada, b_ada, w_in, ret_decay_logit, ret_gn_w, att_sink, w_out, final_norm_w):
    B, L, _ = x.shape
    assert w_ada.shape[0] == 1, "single-layer trunk"

    mod3 = _modulation(c, c_ctx, w_ada[0], b_ada[0][None, :])

    dl = jnp.broadcast_to(ret_decay_logit[0][:, :, None, None], (2, RET_HEADS, SUBLANES, LANES))

    r, rvt, aq, ak, av, ag, sf, sb, ck, cv = _project(x, ctx, mod3, w_in[0], _rope_tables(L), dl)
    ret = _retention(r, rvt, sf, sb, dl, ret_gn_w[0].reshape(RET_HEADS, 1, RET_D))

    sink2 = jnp.repeat(att_sink[0].reshape(ATT_KV, ATT_HEADS // ATT_KV), LANES, axis=1)
    return _attention_output(aq, ak, av, ck, cv, ag, sink2, ret, x, mod3, w_out[0], final_norm_w[None, :])
```

```python
import numpy as np
import jax
import jax.numpy as jnp
from jax import lax
from jax.experimental import pallas as pl
from jax.experimental.pallas import tpu as pltpu

F32 = jnp.float32
BF16 = jnp.bfloat16

D_MODEL = 1024
GRID_W = 64
RET_HEADS = 4
RET_D = 128
RET_W = RET_HEADS * RET_D
ATT_HEADS = 8
ATT_KV = 2
ATT_DH = 64
ATT_W = ATT_HEADS * ATT_DH
ATT_BLOCK = 128
WINDOW = 128
ROPE_BASE = 10000.0
EPS = 1e-6
NEG = -1e30
LOG2E = 1.4426950408889634
IN_COLS = 4 * RET_W + 2 * ATT_W + 2 * ATT_KV * ATT_DH
OFF_RQ, OFF_RK, OFF_RV, OFF_RG = 0, RET_W, 2 * RET_W, 3 * RET_W
OFF_AQ = 4 * RET_W
OFF_AK = OFF_AQ + ATT_W
OFF_AV = OFF_AK + ATT_KV * ATT_DH
OFF_AG = OFF_AV + ATT_KV * ATT_DH

LANES = 128
SUBLANES = 8
VMEM_LIMIT = 56 * 1024 * 1024

MOD_COLS = 768
PROJ_ROWS = 1024
PROJ_SUB = 256
RET_CHUNK = 256
RET_PAIR = 2
ATT_ROWS = 1024
ATT_AHEAD = 2


def _silu(t):
    return t / (1.0 + jnp.exp(-t))


def _log_sigmoid(t):
    return -(jnp.maximum(-t, 0.0) + jnp.log1p(jnp.exp(-jnp.abs(t))))


def _mod_kernel(c_ref, cx_ref, w_ref, b_ref, o_ref):
    n = o_ref.shape[0]
    cc = jnp.concatenate([c_ref[...], jnp.broadcast_to(cx_ref[...], (n - c_ref.shape[0], D_MODEL))], axis=0)
    s, w = _silu(cc), w_ref[...]
    s_hi, w_hi = s.astype(BF16), w.astype(BF16)
    s_lo = (s - s_hi.astype(F32)).astype(BF16)
    w_lo = (w - w_hi.astype(F32)).astype(BF16)
    a = jnp.dot(jnp.concatenate([s_hi, s_lo], axis=0), w_hi, preferred_element_type=F32)
    o_ref[:, 0, :] = a[0:n] + a[n:2 * n] + jnp.dot(s_hi, w_lo, preferred_element_type=F32) + b_ref[...]


def _modulation(c, c_ctx, w_ada, b_ada):
    B, n = c.shape[0], w_ada.shape[1]
    bn = MOD_COLS
    return pl.pallas_call(
        _mod_kernel,
        out_shape=jax.ShapeDtypeStruct((2 * B, 1, n), F32),
        grid=(n // bn,),
        in_specs=[pl.BlockSpec(c.shape, lambda i: (0, 0)),
                  pl.BlockSpec((1, D_MODEL), lambda i: (0, 0)),
                  pl.BlockSpec((D_MODEL, bn), lambda i: (0, i)),
                  pl.BlockSpec((1, bn), lambda i: (0, i))],
        out_specs=pl.BlockSpec((2 * B, 1, bn), lambda i: (0, 0, i)),
        compiler_params=pltpu.CompilerParams(dimension_semantics=("arbitrary",)),
        name="mod",
    )(c, c_ctx[None, :], w_ada, b_ada)


def _norm_mod(x, m_ref):
    shift = m_ref[0, :, 0:D_MODEL]
    scale = m_ref[0, :, D_MODEL:2 * D_MODEL]
    h = x * lax.rsqrt(jnp.mean(x * x, axis=-1, keepdims=True) + EPS)
    return h * (1.0 + scale) + shift


def _rope_ret(t, cos, sin):
    return t * cos + pltpu.roll(t, 64, 1) * sin


def _rope_att(t, cos, sin, low_half):
    partner = jnp.where(low_half, pltpu.roll(t, 96, 1), pltpu.roll(t, 32, 1))
    return t * cos + partner * sin


def _store_kv_groups(k_ref, v_ref, rows, k, v):
    low = lax.broadcasted_iota(jnp.int32, k.shape, 1) < ATT_DH
    k_ref[0, 0, rows, :] = jnp.where(low, k, 0.0).astype(BF16)
    k_ref[0, 1, rows, :] = jnp.where(low, 0.0, k).astype(BF16)
    v_ref[0, 0, rows, :] = jnp.where(low, v, 1.0).astype(BF16)
    v_ref[0, 1, rows, :] = jnp.where(low, 1.0, v).astype(BF16)


def _pair_heads(nat):
    low = lax.broadcasted_iota(jnp.int32, nat[0].shape, 1) < ATT_DH
    out = []
    for j in range(len(nat)):
        a, b = nat[j // 2], nat[2 + j // 2]
        if j % 2 == 0:
            out.append(jnp.where(low, a, pltpu.roll(b, ATT_DH, 1)))
        else:
            out.append(jnp.where(low, pltpu.roll(a, ATT_DH, 1), b))
    return out


def _context_block(x_ref, m_ref, w_ref, dl_ref, sf_ref, sb_ref, ck_ref, cv_ref):
    lc = x_ref.shape[1]
    hb = _norm_mod(x_ref[0], m_ref).astype(BF16)
    yk = jnp.dot(hb, w_ref[:, OFF_RK:OFF_RV], preferred_element_type=F32)
    yv = jnp.dot(hb, w_ref[:, OFF_RV:OFF_RG], preferred_element_type=F32)
    ya = jnp.dot(hb, w_ref[:, OFF_AK:OFF_AG], preferred_element_type=F32)
    _store_kv_groups(ck_ref, cv_ref, slice(0, lc), ya[:, 0:LANES], ya[:, LANES:2 * LANES])
    lg = _log_sigmoid(dl_ref[...])
    pos = lax.broadcasted_iota(jnp.int32, (lc, 1), 0).astype(F32)
    dn = (((0,), (0,)), ((), ()))
    for h in range(RET_HEADS):
        k = yk[:, h * RET_D:(h + 1) * RET_D] * (RET_D ** -0.5)
        v = yv[:, h * RET_D:(h + 1) * RET_D].astype(BF16)
        wf = jnp.exp(lg[0, h, 0:1, :] * (lc - 1.0 - pos))
        wb = jnp.exp(lg[1, h, 0:1, :] * pos)
        sf_ref[0, h] = lax.dot_general(v, (k * wf).astype(BF16), dn, preferred_element_type=F32)
        sb_ref[0, h] = lax.dot_general(v, (k * wb).astype(BF16), dn, preferred_element_type=F32)


def _proj_kernel(x_ref, m_ref, w_ref, cr_ref, sr_ref, ca_ref, sa_ref, cx_ref, mc_ref, dl_ref,
                 r_ref, rvt_ref, aq_ref, ak_ref, av_ref, ag_ref, sf_ref, sb_ref, ck_ref, cv_ref,
                 w_scr):

    @pl.when((pl.program_id(0) == 0) & (pl.program_id(1) == 0))
    def _():
        for r in range(0, D_MODEL, LANES):
            rr = slice(r, r + LANES)
            for c0, c1 in ((OFF_RQ, OFF_RK), (OFF_RK, OFF_RV), (OFF_RV, OFF_RG), (OFF_RG, OFF_AQ),
                           (OFF_AK, OFF_AG)):
                w_scr[rr, c0:c1] = w_ref[rr, c0:c1].astype(BF16)
            for off in (OFF_AQ, OFF_AG):
                nat = [w_ref[rr, off + j * LANES:off + (j + 1) * LANES] for j in range(ATT_W // LANES)]
                for j, blk in enumerate(_pair_heads(nat)):
                    w_scr[rr, off + j * LANES:off + (j + 1) * LANES] = blk.astype(BF16)

    @pl.when(pl.program_id(1) == 0)
    def _():
        _context_block(cx_ref, mc_ref, w_scr, dl_ref, sf_ref, sb_ref, ck_ref, cv_ref)

    for t in range(x_ref.shape[1] // PROJ_SUB):
        rows = slice(t * PROJ_SUB, (t + 1) * PROJ_SUB)
        hb = _norm_mod(x_ref[0, rows, :], m_ref).astype(BF16)
        cr, sr, ca, sa = cr_ref[rows, :], sr_ref[rows, :], ca_ref[rows, :], sa_ref[rows, :]
        low_half = (lax.broadcasted_iota(jnp.int32, ca.shape, 1) % ATT_DH) < (ATT_DH // 2)

        def mm(c0, c1):
            return jnp.dot(hb, w_scr[:, c0:c1], preferred_element_type=F32)

        y = mm(OFF_RQ, OFF_RK)
        for h in range(RET_HEADS):
            sl = slice(h * LANES, (h + 1) * LANES)
            r_ref[0, rows, OFF_RQ + h * LANES:OFF_RQ + (h + 1) * LANES] = _rope_ret(y[:, sl], cr, sr).astype(BF16)
        y = mm(OFF_RK, OFF_RV) * (RET_D ** -0.5)
        for h in range(RET_HEADS):
            sl = slice(h * LANES, (h + 1) * LANES)
            r_ref[0, rows, OFF_RK + h * LANES:OFF_RK + (h + 1) * LANES] = _rope_ret(y[:, sl], cr, sr).astype(BF16)
        y = mm(OFF_RV, OFF_RG)
        for h in range(RET_HEADS):
            sl = slice(h * LANES, (h + 1) * LANES)
            rvt_ref[0, sl, rows] = y[:, sl].T.astype(BF16)
        r_ref[0, rows, 2 * RET_W:3 * RET_W] = _silu(mm(OFF_RG, OFF_AQ)).astype(BF16)
        y = mm(OFF_AQ, OFF_AK)
        for j in range(ATT_W // LANES):
            sl = slice(j * LANES, (j + 1) * LANES)
            aq_ref[0, rows, sl] = (_rope_att(y[:, sl], ca, sa, low_half) * (ATT_DH ** -0.5 * LOG2E)).astype(BF16)
        y = mm(OFF_AK, OFF_AG)
        _store_kv_groups(ak_ref, av_ref, rows, _rope_att(y[:, 0:LANES], ca, sa, low_half), y[:, LANES:2 * LANES])
        ag_ref[0, rows, :] = _silu(mm(OFF_AG, IN_COLS)).astype(BF16)


def _project(x, ctx, mod3, w_in_l, tabs, dl):
    B, L, _ = x.shape
    lc = ctx.shape[1]
    tm = PROJ_ROWS
    ctx_row = mod3.shape[0] // 2
    st_shape = jax.ShapeDtypeStruct((B, RET_HEADS, RET_D, RET_D), F32)
    st_spec = pl.BlockSpec((1, RET_HEADS, RET_D, RET_D), lambda b, i: (b, 0, 0, 0))
    ckv_shape = jax.ShapeDtypeStruct((B, ATT_KV, lc, LANES), BF16)
    ckv_spec = pl.BlockSpec((1, ATT_KV, lc, LANES), lambda b, i: (b, 0, 0, 0))
    tab_spec = pl.BlockSpec((tm, LANES), lambda b, i: (i, 0))
    row = lambda n: pl.BlockSpec((1, tm, n), lambda b, i: (b, i, 0))
    kv_spec = pl.BlockSpec((1, ATT_KV, tm, LANES), lambda b, i: (b, 0, i, 0))
    kv_shape = jax.ShapeDtypeStruct((B, ATT_KV, L, LANES), BF16)
    return pl.pallas_call(
        _proj_kernel,
        out_shape=(jax.ShapeDtypeStruct((B, L, 3 * RET_W), BF16),
                   jax.ShapeDtypeStruct((B, RET_W, L), BF16),
                   jax.ShapeDtypeStruct((B, L, ATT_W), BF16),
                   kv_shape, kv_shape,
                   jax.ShapeDtypeStruct((B, L, ATT_W), BF16),
                   st_shape, st_shape, ckv_shape, ckv_shape),
        grid=(B, L // tm),
        in_specs=[row(D_MODEL),
                  pl.BlockSpec((1, 1, 3 * D_MODEL), lambda b, i: (b, 0, 0)),
                  pl.BlockSpec((D_MODEL, IN_COLS), lambda b, i: (0, 0), pipeline_mode=pl.Buffered(1)),
                  tab_spec, tab_spec, tab_spec, tab_spec,
                  pl.BlockSpec((1, lc, D_MODEL), lambda b, i: (b, 0, 0)),
                  pl.BlockSpec((1, 1, 3 * D_MODEL), lambda b, i: (ctx_row, 0, 0)),
                  pl.BlockSpec(dl.shape, lambda b, i: (0, 0, 0, 0))],
        out_specs=(row(3 * RET_W), pl.BlockSpec((1, RET_W, tm), lambda b, i: (b, 0, i)),
                   row(ATT_W), kv_spec, kv_spec, row(ATT_W),
                   st_spec, st_spec, ckv_spec, ckv_spec),
        scratch_shapes=[pltpu.VMEM((D_MODEL, IN_COLS), BF16)],
        compiler_params=pltpu.CompilerParams(
            dimension_semantics=("arbitrary", "arbitrary"), vmem_limit_bytes=VMEM_LIMIT),
        name="proj",
    )(x, mod3, w_in_l, *tabs, ctx, mod3, dl)


def _ret_kernel(q_ref, k_ref, vt_ref, g_ref, sf_ref, sb_ref, dl_ref, gn_ref, o_ref,
                uf_scr, rb_scr, kdec_scr, qdec_scr, cdec_scr, decay_scr):
    L = q_ref.shape[1]
    C = RET_CHUNK
    nch = L // C
    heads = range(RET_PAIR)
    lanes = lambda h: slice(h * RET_D, (h + 1) * RET_D)

    @pl.when(pl.program_id(1) == 0)
    def _():
        lg = _log_sigmoid(dl_ref[...])
        pos = lax.broadcasted_iota(jnp.int32, (C, 1), 0).astype(F32)
        posl = lax.broadcasted_iota(jnp.int32, (SUBLANES, C), 1).astype(F32)
        diff = (lax.broadcasted_iota(jnp.int32, (C, C), 1)
                - lax.broadcasted_iota(jnp.int32, (C, C), 0)).astype(F32)
        for h in heads:
            lgf, lgb = lg[0, h, 0:1, :], lg[1, h, 0:1, :]
            lgf1, lgb1 = lgf[:, 0:1], lgb[:, 0:1]
            kdec_scr[h, 0] = jnp.exp(lgf * (C - 1.0 - pos))
            kdec_scr[h, 1] = jnp.exp(lgb * pos)
            qdec_scr[h, 0] = jnp.exp(lgf1 * (posl + 1.0))
            qdec_scr[h, 1] = jnp.exp(lgb1 * (C - posl))
            cdec_scr[h, 0] = jnp.broadcast_to(jnp.exp(lgf * C), (SUBLANES, LANES))
            cdec_scr[h, 1] = jnp.broadcast_to(jnp.exp(lgb * C), (SUBLANES, LANES))
            decay_scr[h] = jnp.where(diff >= 0.0, jnp.exp(lgf1 * jnp.maximum(diff, 0.0)),
                                     jnp.exp(lgb1 * jnp.maximum(-diff, 0.0)))

    nt = (((1,), (1,)), ((), ()))
    chunk = lambda n: slice(n * C, (n + 1) * C)

    rb = [sb_ref[0, h] for h in heads]
    for n in reversed(range(nch)):
        for h in heads:
            kc = k_ref[0, chunk(n), lanes(h)].astype(F32)
            kd = jnp.concatenate([(kc * kdec_scr[h, 0]).astype(BF16), (kc * kdec_scr[h, 1]).astype(BF16)],
                                 axis=1)
            u = jnp.dot(vt_ref[0, lanes(h), chunk(n)], kd, preferred_element_type=F32)
            uf_scr[h, n] = u[:, 0:RET_D]
            rb_scr[h, n] = rb[h].astype(BF16)
            rb[h] = rb[h] * cdec_scr[h, 1, 0:1, :] + u[:, RET_D:2 * RET_D]

    def kq(h, n):
        return lax.dot_general(k_ref[0, chunk(n), lanes(h)], q_ref[0, chunk(n), lanes(h)], nt,
                               preferred_element_type=F32)

    rf = [sf_ref[0, h] for h in heads]
    s_next = [kq(h, 0) for h in heads]
    for n in range(nch):
        for h in heads:
            s = s_next[h]
            q = q_ref[0, chunk(n), lanes(h)]
            states = jnp.concatenate([rf[h].astype(BF16), rb_scr[h, n]], axis=0)
            cross = lax.dot_general(states, q, nt, preferred_element_type=F32)
            if n + 1 < nch:
                s_next[h] = kq(h, n + 1)
                rf[h] = rf[h] * cdec_scr[h, 0, 0:1, :] + uf_scr[h, n]
            inner = jnp.dot(vt_ref[0, lanes(h), chunk(n)], (s * decay_scr[h]).astype(BF16),
                            preferred_element_type=F32)
            y = (inner + cross[0:RET_D] * qdec_scr[h, 0, 0:1, :]
                 + cross[RET_D:2 * RET_D] * qdec_scr[h, 1, 0:1, :])
            mu = jnp.mean(y, axis=0, keepdims=True)
            yc = y - mu
            var = jnp.mean(yc * yc, axis=0, keepdims=True)
            yn = (yc * lax.rsqrt(var + EPS)).T * gn_ref[h]
            o_ref[0, chunk(n), lanes(h)] = (yn * g_ref[0, chunk(n), lanes(h)].astype(F32)).astype(BF16)


def _retention(r, rvt, sf, sb, dl, gn):
    B, L, _ = r.shape
    P, C = RET_PAIR, RET_CHUNK
    col = lambda off: pl.BlockSpec((1, L, P * RET_D), lambda hp, b: (b, 0, off + hp))
    st_spec = pl.BlockSpec((1, P, RET_D, RET_D), lambda hp, b: (b, hp, 0, 0))
    npair = RET_HEADS // P
    return pl.pallas_call(
        _ret_kernel,
        out_shape=jax.ShapeDtypeStruct((B, L, RET_W), BF16),
        grid=(npair, B),
        in_specs=[col(0), col(npair),
                  pl.BlockSpec((1, P * RET_D, L), lambda hp, b: (b, hp, 0)),
                  col(2 * npair),
                  st_spec, st_spec,
                  pl.BlockSpec((2, P, SUBLANES, LANES), lambda hp, b: (0, hp, 0, 0)),
                  pl.BlockSpec((P, 1, RET_D), lambda hp, b: (hp, 0, 0))],
        out_specs=pl.BlockSpec((1, L, P * RET_D), lambda hp, b: (b, 0, hp)),
        scratch_shapes=[pltpu.VMEM((P, L // C, RET_D, RET_D), F32),
                        pltpu.VMEM((P, L // C, RET_D, RET_D), BF16),
                        pltpu.VMEM((P, 2, C, LANES), F32),
                        pltpu.VMEM((P, 2, SUBLANES, C), F32),
                        pltpu.VMEM((P, 2, SUBLANES, LANES), F32),
                        pltpu.VMEM((P, C, C), F32)],
        compiler_params=pltpu.CompilerParams(
            dimension_semantics=("arbitrary", "arbitrary"), vmem_limit_bytes=VMEM_LIMIT),
        name="ret",
    )(r, r, rvt, r, sf, sb, dl, gn)


def _att_kernel(q_ref, k_ref, v_ref, ck_ref, cv_ref, g_ref, sink_ref, r_ref, x_ref, m_ref, w_ref, fn_ref,
                o_ref, w_scr):
    @pl.when((pl.program_id(0) == 0) & (pl.program_id(1) == 0))
    def _():
        w_scr[0:RET_W, :] = w_ref[0:RET_W, :].astype(BF16)
        for j in range(ATT_HEADS // ATT_KV):
            for half in range(ATT_KV):
                dst = RET_W + j * LANES + half * ATT_DH
                src = RET_W + (j + half * (ATT_HEADS // ATT_KV)) * ATT_DH
                w_scr[dst:dst + ATT_DH, :] = w_ref[src:src + ATT_DH, :].astype(BF16)

    L = k_ref.shape[2]
    nb_total = L // ATT_BLOCK
    nb_step = q_ref.shape[1] // ATT_BLOCK
    nhb = ATT_W // LANES
    i = pl.program_id(1)
    r_i = lax.broadcasted_iota(jnp.int32, (ATT_BLOCK, ATT_BLOCK), 0)
    q_i = lax.broadcasted_iota(jnp.int32, (ATT_BLOCK, ATT_BLOCK), 1)
    band_prev = jnp.where(q_i <= r_i, 0.0, NEG).astype(F32)
    band_next = jnp.where(r_i <= q_i, 0.0, NEG).astype(F32)
    nt = (((1,), (1,)), ((), ()))
    tn = (((0,), (0,)), ((), ()))

    def key_rows(jb):
        n = i * nb_step + jb
        blk = lambda t: pl.ds(pl.multiple_of(t * ATT_BLOCK, ATT_BLOCK), ATT_BLOCK)
        return n, (blk(jnp.maximum(n - 1, 0)), blk(n), blk(jnp.minimum(n + 1, nb_total - 1)))

    def scores(jb, kv):
        n, (p_rows, o_rows, n_rows) = key_rows(jb)
        q = q_ref[0, jb * ATT_BLOCK:(jb + 1) * ATT_BLOCK, :]
        qall = jnp.concatenate([q[:, j * LANES:(j + 1) * LANES] for j in range(nhb)], axis=0)
        kall = jnp.concatenate([ck_ref[0, kv], k_ref[0, kv, p_rows, :], k_ref[0, kv, o_rows, :],
                                k_ref[0, kv, n_rows, :]], axis=0)
        return lax.dot_general(kall, qall, nt, preferred_element_type=F32)

    def attend(jb, kv, s):
        n, (p_rows, o_rows, n_rows) = key_rows(jb)
        lc = ck_ref.shape[2]
        bias_prev = jnp.concatenate([band_prev + jnp.where(n == 0, NEG, 0.0)] * nhb, axis=1)
        bias_next = jnp.concatenate([band_next + jnp.where(n == nb_total - 1, NEG, 0.0)] * nhb, axis=1)
        parts = [s[0:lc], s[lc:lc + ATT_BLOCK] + bias_prev, s[lc + ATT_BLOCK:lc + 2 * ATT_BLOCK],
                 s[lc + 2 * ATT_BLOCK:] + bias_next]
        sink = sink_ref[kv:kv + 1, :] * LOG2E
        m = sink
        for t in parts:
            m = jnp.maximum(m, jnp.max(t, axis=0, keepdims=True))
        p = jnp.concatenate([jnp.exp2(t - m).astype(BF16) for t in parts], axis=0)
        vaug = jnp.concatenate([cv_ref[0, kv], v_ref[0, kv, p_rows, :], v_ref[0, kv, o_rows, :],
                                v_ref[0, kv, n_rows, :]], axis=0)
        o = lax.dot_general(vaug, p, tn, preferred_element_type=F32)
        if kv == 0:
            val, ones = slice(0, ATT_DH), slice(ATT_DH, ATT_DH + 1)
        else:
            val, ones = slice(ATT_DH, 2 * ATT_DH), slice(0, 1)
        den = o[ones] + jnp.exp2(sink - m)
        return o[val] * (1.0 / den)

    def gated_block(jb, halves):
        rows = slice(jb * ATT_BLOCK, (jb + 1) * ATT_BLOCK)
        comb = jnp.concatenate(halves, axis=0)
        return jnp.concatenate(
            [(comb[:, j * LANES:(j + 1) * LANES].T
              * g_ref[0, rows, j * LANES:(j + 1) * LANES].astype(F32)).astype(BF16) for j in range(nhb)], axis=1)

    def finish(jb, att_blk):
        rows = slice(jb * ATT_BLOCK, (jb + 1) * ATT_BLOCK)
        mix_in = jnp.concatenate([r_ref[0, rows, :], att_blk], axis=1)
        mixed = jnp.dot(mix_in, w_scr[...], preferred_element_type=F32)
        xn = x_ref[0, rows, :] + m_ref[0, :, 2 * D_MODEL:3 * D_MODEL] * mixed
        o_ref[0, rows, :] = xn * lax.rsqrt(jnp.mean(xn * xn, axis=-1, keepdims=True) + EPS) * fn_ref[...]

    chains = [(jb, kv) for jb in range(nb_step) for kv in range(ATT_KV)]
    ahead = [scores(*chains[t]) for t in range(ATT_AHEAD)]
    halves, pending = [], []
    for c, (jb, kv) in enumerate(chains):
        s = ahead.pop(0)
        if c + ATT_AHEAD < len(chains):
            ahead.append(scores(*chains[c + ATT_AHEAD]))
        halves.append(attend(jb, kv, s))
        if pending:
            finish(*pending.pop())
        if kv == ATT_KV - 1:
            pending.append((jb, gated_block(jb, halves)))
            halves = []
    finish(*pending.pop())


def _attention_output(aq, ak, av, ck, cv, ag, sink2, ret, x, mod3, w_out_l, fnw):
    B, L, _ = aq.shape
    lc = ck.shape[2]
    tq = ATT_ROWS
    row = lambda n: pl.BlockSpec((1, tq, n), lambda b, i: (b, i, 0))
    full = lambda n: pl.BlockSpec((1, ATT_KV, n, LANES), lambda b, i: (b, 0, 0, 0))
    return pl.pallas_call(
        _att_kernel,
        out_shape=jax.ShapeDtypeStruct((B, L, D_MODEL), F32),
        grid=(B, L // tq),
        in_specs=[row(ATT_W), full(L), full(L), full(lc), full(lc), row(ATT_W),
                  pl.BlockSpec(sink2.shape, lambda b, i: (0, 0)),
                  row(RET_W), row(D_MODEL),
                  pl.BlockSpec((1, 1, 3 * D_MODEL), lambda b, i: (b, 0, 0)),
                  pl.BlockSpec((RET_W + ATT_W, D_MODEL), lambda b, i: (0, 0), pipeline_mode=pl.Buffered(1)),
                  pl.BlockSpec((1, D_MODEL), lambda b, i: (0, 0))],
        out_specs=row(D_MODEL),
        scratch_shapes=[pltpu.VMEM((RET_W + ATT_W, D_MODEL), BF16)],
        compiler_params=pltpu.CompilerParams(
            dimension_semantics=("arbitrary", "arbitrary"), vmem_limit_bytes=VMEM_LIMIT),
        name="att",
    )(aq, ak, av, ck, cv, ag, sink2, ret, x, mod3, w_out_l, fnw)


def _rope_tables(L):
    pos = np.arange(L)
    rows, cols = (pos // GRID_W).astype(np.float64), (pos % GRID_W).astype(np.float64)

    def tables(dh):
        nf = dh // 4
        inv = ROPE_BASE ** (-np.arange(nf, dtype=np.float64) / nf)
        ang = np.concatenate([rows[:, None] * inv, cols[:, None] * inv], axis=-1)
        cos, sin = np.cos(ang), np.sin(ang)
        reps = LANES // dh
        return (jnp.asarray(np.tile(np.concatenate([cos, cos], axis=-1), (1, reps)), F32),
                jnp.asarray(np.tile(np.concatenate([-sin, sin], axis=-1), (1, reps)), F32))

    cr, sr = tables(RET_D)
    ca, sa = tables(ATT_DH)
    return cr, sr, ca, sa


def kernel(x, c, ctx, c_ctx, w_ada, b_ada, w_in, ret_decay_logit, ret_gn_w, att_sink, w_out, final_norm_w):
    B, L, _ = x.shape
    assert w_ada.shape[0] == 1, "single-layer trunk"

    mod3 = _modulation(c, c_ctx, w_ada[0], b_ada[0][None, :])

    dl = jnp.broadcast_to(ret_decay_logit[0][:, :, None, None], (2, RET_HEADS, SUBLANES, LANES))

    r, rvt, aq, ak, av, ag, sf, sb, ck, cv = _project(x, ctx, mod3, w_in[0], _rope_tables(L), dl)
    ret = _retention(r, rvt, sf, sb, dl, ret_gn_w[0].reshape(RET_HEADS, 1, RET_D))

    sink2 = jnp.repeat(att_sink[0].reshape(ATT_KV, ATT_HEADS // ATT_KV), LANES, axis=1)
    return _attention_output(aq, ak, av, ck, cv, ag, sink2, ret, x, mod3, w_out[0], final_norm_w[None, :])
```

```python
import numpy as np
import jax
import jax.numpy as jnp
from jax import lax
from jax.experimental import pallas as pl
from jax.experimental.pallas import tpu as pltpu

F32 = jnp.float32
BF16 = jnp.bfloat16

D_MODEL = 1024
GRID_W = 64
RET_HEADS = 4
RET_D = 128
RET_W = RET_HEADS * RET_D
ATT_HEADS = 8
ATT_KV = 2
ATT_DH = 64
ATT_W = ATT_HEADS * ATT_DH
ATT_BLOCK = 128
WINDOW = 128
assert WINDOW == ATT_BLOCK
ROPE_BASE = 10000.0
EPS = 1e-6
NEG = -1e30
LOG2E = 1.4426950408889634
IN_COLS = 4 * RET_W + 2 * ATT_W + 2 * ATT_KV * ATT_DH
OFF_RQ, OFF_RK, OFF_RV, OFF_RG = 0, RET_W, 2 * RET_W, 3 * RET_W
OFF_AQ = 4 * RET_W
OFF_AK = OFF_AQ + ATT_W
OFF_AV = OFF_AK + ATT_KV * ATT_DH
OFF_AG = OFF_AV + ATT_KV * ATT_DH

LANES = 128
SUBLANES = 8
VMEM_LIMIT = 56 * 1024 * 1024

MOD_COLS = 768
PROJ_ROWS = 1024
PROJ_SUB = 256
RET_CHUNK = 256
RET_PAIR = 2
ATT_ROWS = 1024
ATT_AHEAD = 2


def _silu(t):
    return t / (1.0 + jnp.exp(-t))


def _log_sigmoid(t):
    return -(jnp.maximum(-t, 0.0) + jnp.log1p(jnp.exp(-jnp.abs(t))))


def _mod_kernel(c_ref, cx_ref, w_ref, b_ref, o_ref):
    n = o_ref.shape[0]
    cc = jnp.concatenate([c_ref[...], jnp.broadcast_to(cx_ref[...], (n - c_ref.shape[0], D_MODEL))], axis=0)
    s, w = _silu(cc), w_ref[...]
    s_hi, w_hi = s.astype(BF16), w.astype(BF16)
    s_lo = (s - s_hi.astype(F32)).astype(BF16)
    w_lo = (w - w_hi.astype(F32)).astype(BF16)
    a = jnp.dot(jnp.concatenate([s_hi, s_lo], axis=0), w_hi, preferred_element_type=F32)
    o_ref[:, 0, :] = a[0:n] + a[n:2 * n] + jnp.dot(s_hi, w_lo, preferred_element_type=F32) + b_ref[...]


def _modulation(c, c_ctx, w_ada, b_ada):
    B, n = c.shape[0], w_ada.shape[1]
    bn = MOD_COLS
    return pl.pallas_call(
        _mod_kernel,
        out_shape=jax.ShapeDtypeStruct((2 * B, 1, n), F32),
        grid=(n // bn,),
        in_specs=[pl.BlockSpec(c.shape, lambda i: (0, 0)),
                  pl.BlockSpec((1, D_MODEL), lambda i: (0, 0)),
                  pl.BlockSpec((D_MODEL, bn), lambda i: (0, i)),
                  pl.BlockSpec((1, bn), lambda i: (0, i))],
        out_specs=pl.BlockSpec((2 * B, 1, bn), lambda i: (0, 0, i)),
        compiler_params=pltpu.CompilerParams(dimension_semantics=("arbitrary",)),
        name="mod",
    )(c, c_ctx[None, :], w_ada, b_ada)


def _norm_mod(x, m_ref):
    shift = m_ref[0, :, 0:D_MODEL]
    scale = m_ref[0, :, D_MODEL:2 * D_MODEL]
    h = x * lax.rsqrt(jnp.mean(x * x, axis=-1, keepdims=True) + EPS)
    return h * (1.0 + scale) + shift


def _rope_ret(t, cos, sin):
    return t * cos + pltpu.roll(t, RET_D // 2, 1) * sin


def _rope_att(t, cos, sin, low_half):
    half = ATT_DH // 2
    partner = jnp.where(low_half, pltpu.roll(t, LANES - half, 1), pltpu.roll(t, half, 1))
    return t * cos + partner * sin


def _store_kv_groups(k_ref, v_ref, rows, k, v):
    low = lax.broadcasted_iota(jnp.int32, k.shape, 1) < ATT_DH
    k_ref[0, 0, rows, :] = jnp.where(low, k, 0.0).astype(BF16)
    k_ref[0, 1, rows, :] = jnp.where(low, 0.0, k).astype(BF16)
    v_ref[0, 0, rows, :] = jnp.where(low, v, 1.0).astype(BF16)
    v_ref[0, 1, rows, :] = jnp.where(low, 1.0, v).astype(BF16)


def _pair_heads(nat):
    low = lax.broadcasted_iota(jnp.int32, nat[0].shape, 1) < ATT_DH
    out = []
    for j in range(len(nat)):
        a, b = nat[j // 2], nat[2 + j // 2]
        if j % 2 == 0:
            out.append(jnp.where(low, a, pltpu.roll(b, ATT_DH, 1)))
        else:
            out.append(jnp.where(low, pltpu.roll(a, ATT_DH, 1), b))
    return out


def _context_block(x_ref, m_ref, w_ref, dl_ref, sf_ref, sb_ref, ck_ref, cv_ref):
    lc = x_ref.shape[1]
    hb = _norm_mod(x_ref[0], m_ref).astype(BF16)
    yk = jnp.dot(hb, w_ref[:, OFF_RK:OFF_RV], preferred_element_type=F32)
    yv = jnp.dot(hb, w_ref[:, OFF_RV:OFF_RG], preferred_element_type=F32)
    ya = jnp.dot(hb, w_ref[:, OFF_AK:OFF_AG], preferred_element_type=F32)
    _store_kv_groups(ck_ref, cv_ref, slice(0, lc), ya[:, 0:LANES], ya[:, LANES:2 * LANES])
    lg = _log_sigmoid(dl_ref[...])
    pos = lax.broadcasted_iota(jnp.int32, (lc, 1), 0).astype(F32)
    dn = (((0,), (0,)), ((), ()))
    for h in range(RET_HEADS):
        k = yk[:, h * RET_D:(h + 1) * RET_D] * (RET_D ** -0.5)
        v = yv[:, h * RET_D:(h + 1) * RET_D].astype(BF16)
        wf = jnp.exp(lg[0, h, 0:1, :] * (lc - 1.0 - pos))
        wb = jnp.exp(lg[1, h, 0:1, :] * pos)
        sf_ref[0, h] = lax.dot_general(v, (k * wf).astype(BF16), dn, preferred_element_type=F32)
        sb_ref[0, h] = lax.dot_general(v, (k * wb).astype(BF16), dn, preferred_element_type=F32)


def _proj_kernel(x_ref, m_ref, w_ref, cr_ref, sr_ref, ca_ref, sa_ref, cx_ref, mc_ref, dl_ref,
                 r_ref, rvt_ref, aq_ref, ak_ref, av_ref, ag_ref, sf_ref, sb_ref, ck_ref, cv_ref,
                 w_scr):

    @pl.when((pl.program_id(0) == 0) & (pl.program_id(1) == 0))
    def _():
        for r in range(0, D_MODEL, LANES):
            rr = slice(r, r + LANES)
            for c0, c1 in ((OFF_RQ, OFF_RK), (OFF_RK, OFF_RV), (OFF_RV, OFF_RG), (OFF_RG, OFF_AQ),
                           (OFF_AK, OFF_AG)):
                w_scr[rr, c0:c1] = w_ref[rr, c0:c1].astype(BF16)
            for off in (OFF_AQ, OFF_AG):
                nat = [w_ref[rr, off + j * LANES:off + (j + 1) * LANES] for j in range(ATT_W // LANES)]
                for j, blk in enumerate(_pair_heads(nat)):
                    w_scr[rr, off + j * LANES:off + (j + 1) * LANES] = blk.astype(BF16)

    @pl.when(pl.program_id(1) == 0)
    def _():
        _context_block(cx_ref, mc_ref, w_scr, dl_ref, sf_ref, sb_ref, ck_ref, cv_ref)

    for t in range(x_ref.shape[1] // PROJ_SUB):
        rows = slice(t * PROJ_SUB, (t + 1) * PROJ_SUB)
        hb = _norm_mod(x_ref[0, rows, :], m_ref).astype(BF16)
        cr, sr, ca, sa = cr_ref[rows, :], sr_ref[rows, :], ca_ref[rows, :], sa_ref[rows, :]
        low_half = (lax.broadcasted_iota(jnp.int32, ca.shape, 1) % ATT_DH) < (ATT_DH // 2)

        def mm(c0, c1):
            return jnp.dot(hb, w_scr[:, c0:c1], preferred_element_type=F32)

        y = mm(OFF_RQ, OFF_RK)
        for h in range(RET_HEADS):
            sl = slice(h * LANES, (h + 1) * LANES)
            r_ref[0, rows, OFF_RQ + h * LANES:OFF_RQ + (h + 1) * LANES] = _rope_ret(y[:, sl], cr, sr).astype(BF16)
        y = mm(OFF_RK, OFF_RV) * (RET_D ** -0.5)
        for h in range(RET_HEADS):
            sl = slice(h * LANES, (h + 1) * LANES)
            r_ref[0, rows, OFF_RK + h * LANES:OFF_RK + (h + 1) * LANES] = _rope_ret(y[:, sl], cr, sr).astype(BF16)
        y = mm(OFF_RV, OFF_RG)
        for h in range(RET_HEADS):
            sl = slice(h * LANES, (h + 1) * LANES)
            rvt_ref[0, sl, rows] = y[:, sl].T.astype(BF16)
        r_ref[0, rows, 2 * RET_W:3 * RET_W] = _silu(mm(OFF_RG, OFF_AQ)).astype(BF16)
        y = mm(OFF_AQ, OFF_AK)
        for j in range(ATT_W // LANES):
            sl = slice(j * LANES, (j + 1) * LANES)
            aq_ref[0, rows, sl] = (_rope_att(y[:, sl], ca, sa, low_half) * (ATT_DH ** -0.5 * LOG2E)).astype(BF16)
        y = mm(OFF_AK, OFF_AG)
        _store_kv_groups(ak_ref, av_ref, rows, _rope_att(y[:, 0:LANES], ca, sa, low_half), y[:, LANES:2 * LANES])
        ag_ref[0, rows, :] = _silu(mm(OFF_AG, IN_COLS)).astype(BF16)


def _project(x, ctx, mod3, w_in_l, tabs, dl):
    B, L, _ = x.shape
    lc = ctx.shape[1]
    tm = PROJ_ROWS
    ctx_row = mod3.shape[0] // 2
    st_shape = jax.ShapeDtypeStruct((B, RET_HEADS, RET_D, RET_D), F32)
    st_spec = pl.BlockSpec((1, RET_HEADS, RET_D, RET_D), lambda b, i: (b, 0, 0, 0))
    ckv_shape = jax.ShapeDtypeStruct((B, ATT_KV, lc, LANES), BF16)
    ckv_spec = pl.BlockSpec((1, ATT_KV, lc, LANES), lambda b, i: (b, 0, 0, 0))
    tab_spec = pl.BlockSpec((tm, LANES), lambda b, i: (i, 0))
    row = lambda n: pl.BlockSpec((1, tm, n), lambda b, i: (b, i, 0))
    kv_spec = pl.BlockSpec((1, ATT_KV, tm, LANES), lambda b, i: (b, 0, i, 0))
    kv_shape = jax.ShapeDtypeStruct((B, ATT_KV, L, LANES), BF16)
    return pl.pallas_call(
        _proj_kernel,
        out_shape=(jax.ShapeDtypeStruct((B, L, 3 * RET_W), BF16),
                   jax.ShapeDtypeStruct((B, RET_W, L), BF16),
                   jax.ShapeDtypeStruct((B, L, ATT_W), BF16),
                   kv_shape, kv_shape,
                   jax.ShapeDtypeStruct((B, L, ATT_W), BF16),
                   st_shape, st_shape, ckv_shape, ckv_shape),
        grid=(B, L // tm),
        in_specs=[row(D_MODEL),
                  pl.BlockSpec((1, 1, 3 * D_MODEL), lambda b, i: (b, 0, 0)),
                  pl.BlockSpec((D_MODEL, IN_COLS), lambda b, i: (0, 0), pipeline_mode=pl.Buffered(1)),
                  tab_spec, tab_spec, tab_spec, tab_spec,
                  pl.BlockSpec((1, lc, D_MODEL), lambda b, i: (b, 0, 0)),
                  pl.BlockSpec((1, 1, 3 * D_MODEL), lambda b, i: (ctx_row, 0, 0)),
                  pl.BlockSpec(dl.shape, lambda b, i: (0, 0, 0, 0))],
        out_specs=(row(3 * RET_W), pl.BlockSpec((1, RET_W, tm), lambda b, i: (b, 0, i)),
                   row(ATT_W), kv_spec, kv_spec, row(ATT_W),
                   st_spec, st_spec, ckv_spec, ckv_spec),
        scratch_shapes=[pltpu.VMEM((D_MODEL, IN_COLS), BF16)],
        compiler_params=pltpu.CompilerParams(
            dimension_semantics=("arbitrary", "arbitrary"), vmem_limit_bytes=VMEM_LIMIT),
        name="proj",
    )(x, mod3, w_in_l, *tabs, ctx, mod3, dl)


def _ret_kernel(q_ref, k_ref, vt_ref, g_ref, sf_ref, sb_ref, dl_ref, gn_ref, o_ref,
                uf_scr, rb_scr, kdec_scr, qdec_scr, cdec_scr, decay_scr):
    L = q_ref.shape[1]
    C = RET_CHUNK
    nch = L // C
    heads = range(RET_PAIR)
    lanes = lambda h: slice(h * RET_D, (h + 1) * RET_D)

    @pl.when(pl.program_id(1) == 0)
    def _():
        lg = _log_sigmoid(dl_ref[...])
        pos = lax.broadcasted_iota(jnp.int32, (C, 1), 0).astype(F32)
        posl = lax.broadcasted_iota(jnp.int32, (SUBLANES, C), 1).astype(F32)
        diff = (lax.broadcasted_iota(jnp.int32, (C, C), 1)
                - lax.broadcasted_iota(jnp.int32, (C, C), 0)).astype(F32)
        for h in heads:
            lgf, lgb = lg[0, h, 0:1, :], lg[1, h, 0:1, :]
            lgf1, lgb1 = lgf[:, 0:1], lgb[:, 0:1]
            kdec_scr[h, 0] = jnp.exp(lgf * (C - 1.0 - pos))
            kdec_scr[h, 1] = jnp.exp(lgb * pos)
            qdec_scr[h, 0] = jnp.exp(lgf1 * (posl + 1.0))
            qdec_scr[h, 1] = jnp.exp(lgb1 * (C - posl))
            cdec_scr[h, 0] = jnp.broadcast_to(jnp.exp(lgf * C), (SUBLANES, LANES))
            cdec_scr[h, 1] = jnp.broadcast_to(jnp.exp(lgb * C), (SUBLANES, LANES))
            decay_scr[h] = jnp.where(diff >= 0.0, jnp.exp(lgf1 * jnp.maximum(diff, 0.0)),
                                     jnp.exp(lgb1 * jnp.maximum(-diff, 0.0)))

    nt = (((1,), (1,)), ((), ()))
    chunk = lambda n: slice(n * C, (n + 1) * C)

    rb = [sb_ref[0, h] for h in heads]
    for n in reversed(range(nch)):
        for h in heads:
            kc = k_ref[0, chunk(n), lanes(h)].astype(F32)
            kd = jnp.concatenate([(kc * kdec_scr[h, 0]).astype(BF16), (kc * kdec_scr[h, 1]).astype(BF16)],
                                 axis=1)
            u = jnp.dot(vt_ref[0, lanes(h), chunk(n)], kd, preferred_element_type=F32)
            uf_scr[h, n] = u[:, 0:RET_D]
            rb_scr[h, n] = rb[h].astype(BF16)
            rb[h] = rb[h] * cdec_scr[h, 1, 0:1, :] + u[:, RET_D:2 * RET_D]

    def kq(h, n):
        return lax.dot_general(k_ref[0, chunk(n), lanes(h)], q_ref[0, chunk(n), lanes(h)], nt,
                               preferred_element_type=F32)

    rf = [sf_ref[0, h] for h in heads]
    s_next = [kq(h, 0) for h in heads]
    for n in range(nch):
        for h in heads:
            s = s_next[h]
            q = q_ref[0, chunk(n), lanes(h)]
            states = jnp.concatenate([rf[h].astype(BF16), rb_scr[h, n]], axis=0)
            cross = lax.dot_general(states, q, nt, preferred_element_type=F32)
            if n + 1 < nch:
                s_next[h] = kq(h, n + 1)
                rf[h] = rf[h] * cdec_scr[h, 0, 0:1, :] + uf_scr[h, n]
            inner = jnp.dot(vt_ref[0, lanes(h), chunk(n)], (s * decay_scr[h]).astype(BF16),
                            preferred_element_type=F32)
            y = (inner + cross[0:RET_D] * qdec_scr[h, 0, 0:1, :]
                 + cross[RET_D:2 * RET_D] * qdec_scr[h, 1, 0:1, :])
            mu = jnp.mean(y, axis=0, keepdims=True)
            yc = y - mu
            var = jnp.mean(yc * yc, axis=0, keepdims=True)
            yn = (yc * lax.rsqrt(var + EPS)).T * gn_ref[h]
            o_ref[0, chunk(n), lanes(h)] = (yn * g_ref[0, chunk(n), lanes(h)].astype(F32)).astype(BF16)


def _retention(r, rvt, sf, sb, dl, gn):
    B, L, _ = r.shape
    P, C = RET_PAIR, RET_CHUNK
    col = lambda off: pl.BlockSpec((1, L, P * RET_D), lambda hp, b: (b, 0, off + hp))
    st_spec = pl.BlockSpec((1, P, RET_D, RET_D), lambda hp, b: (b, hp, 0, 0))
    npair = RET_HEADS // P
    return pl.pallas_call(
        _ret_kernel,
        out_shape=jax.ShapeDtypeStruct((B, L, RET_W), BF16),
        grid=(npair, B),
        in_specs=[col(0), col(npair),
                  pl.BlockSpec((1, P * RET_D, L), lambda hp, b: (b, hp, 0)),
                  col(2 * npair),
                  st_spec, st_spec,
                  pl.BlockSpec((2, P, SUBLANES, LANES), lambda hp, b: (0, hp, 0, 0)),
                  pl.BlockSpec((P, 1, RET_D), lambda hp, b: (hp, 0, 0))],
        out_specs=pl.BlockSpec((1, L, P * RET_D), lambda hp, b: (b, 0, hp)),
        scratch_shapes=[pltpu.VMEM((P, L // C, RET_D, RET_D), F32),
                        pltpu.VMEM((P, L // C, RET_D, RET_D), BF16),
                        pltpu.VMEM((P, 2, C, LANES), F32),
                        pltpu.VMEM((P, 2, SUBLANES, C), F32),
                        pltpu.VMEM((P, 2, SUBLANES, LANES), F32),
                        pltpu.VMEM((P, C, C), F32)],
        compiler_params=pltpu.CompilerParams(
            dimension_semantics=("arbitrary", "arbitrary"), vmem_limit_bytes=VMEM_LIMIT),
        name="ret",
    )(r, r, rvt, r, sf, sb, dl, gn)


def _att_kernel(q_ref, k_ref, v_ref, ck_ref, cv_ref, g_ref, sink_ref, r_ref, x_ref, m_ref, w_ref, fn_ref,
                o_ref, w_scr):
    @pl.when((pl.program_id(0) == 0) & (pl.program_id(1) == 0))
    def _():
        w_scr[0:RET_W, :] = w_ref[0:RET_W, :].astype(BF16)
        for j in range(ATT_HEADS // ATT_KV):
            for half in range(ATT_KV):
                dst = RET_W + j * LANES + half * ATT_DH
                src = RET_W + (j + half * (ATT_HEADS // ATT_KV)) * ATT_DH
                w_scr[dst:dst + ATT_DH, :] = w_ref[src:src + ATT_DH, :].astype(BF16)

    L = k_ref.shape[2]
    nb_total = L // ATT_BLOCK
    nb_step = q_ref.shape[1] // ATT_BLOCK
    nhb = ATT_W // LANES
    i = pl.program_id(1)
    r_i = lax.broadcasted_iota(jnp.int32, (ATT_BLOCK, ATT_BLOCK), 0)
    q_i = lax.broadcasted_iota(jnp.int32, (ATT_BLOCK, ATT_BLOCK), 1)
    band_prev = jnp.where(q_i <= r_i, 0.0, NEG).astype(F32)
    band_next = jnp.where(r_i <= q_i, 0.0, NEG).astype(F32)
    nt = (((1,), (1,)), ((), ()))
    tn = (((0,), (0,)), ((), ()))

    def key_rows(jb):
        n = i * nb_step + jb
        blk = lambda t: pl.ds(pl.multiple_of(t * ATT_BLOCK, ATT_BLOCK), ATT_BLOCK)
        return n, (blk(jnp.maximum(n - 1, 0)), blk(n), blk(jnp.minimum(n + 1, nb_total - 1)))

    def scores(jb, kv):
        n, (p_rows, o_rows, n_rows) = key_rows(jb)
        q = q_ref[0, jb * ATT_BLOCK:(jb + 1) * ATT_BLOCK, :]
        qall = jnp.concatenate([q[:, j * LANES:(j + 1) * LANES] for j in range(nhb)], axis=0)
        kall = jnp.concatenate([ck_ref[0, kv], k_ref[0, kv, p_rows, :], k_ref[0, kv, o_rows, :],
                                k_ref[0, kv, n_rows, :]], axis=0)
        return lax.dot_general(kall, qall, nt, preferred_element_type=F32)

    def attend(jb, kv, s):
        n, (p_rows, o_rows, n_rows) = key_rows(jb)
        lc = ck_ref.shape[2]
        bias_prev = jnp.concatenate([band_prev + jnp.where(n == 0, NEG, 0.0)] * nhb, axis=1)
        bias_next = jnp.concatenate([band_next + jnp.where(n == nb_total - 1, NEG, 0.0)] * nhb, axis=1)
        parts = [s[0:lc], s[lc:lc + ATT_BLOCK] + bias_prev, s[lc + ATT_BLOCK:lc + 2 * ATT_BLOCK],
                 s[lc + 2 * ATT_BLOCK:] + bias_next]
        sink = sink_ref[kv:kv + 1, :] * LOG2E
        m = sink
        for t in parts:
            m = jnp.maximum(m, jnp.max(t, axis=0, keepdims=True))
        p = jnp.concatenate([jnp.exp2(t - m).astype(BF16) for t in parts], axis=0)
        vaug = jnp.concatenate([cv_ref[0, kv], v_ref[0, kv, p_rows, :], v_ref[0, kv, o_rows, :],
                                v_ref[0, kv, n_rows, :]], axis=0)
        o = lax.dot_general(vaug, p, tn, preferred_element_type=F32)
        if kv == 0:
            val, ones = slice(0, ATT_DH), slice(ATT_DH, ATT_DH + 1)
        else:
            val, ones = slice(ATT_DH, 2 * ATT_DH), slice(0, 1)
        den = o[ones] + jnp.exp2(sink - m)
        return o[val] * (1.0 / den)

    def gated_block(jb, halves):
        rows = slice(jb * ATT_BLOCK, (jb + 1) * ATT_BLOCK)
        comb = jnp.concatenate(halves, axis=0)
        return jnp.concatenate(
            [(comb[:, j * LANES:(j + 1) * LANES].T
              * g_ref[0, rows, j * LANES:(j + 1) * LANES].astype(F32)).astype(BF16) for j in range(nhb)], axis=1)

    def finish(jb, att_blk):
        rows = slice(jb * ATT_BLOCK, (jb + 1) * ATT_BLOCK)
        mix_in = jnp.concatenate([r_ref[0, rows, :], att_blk], axis=1)
        mixed = jnp.dot(mix_in, w_scr[...], preferred_element_type=F32)
        xn = x_ref[0, rows, :] + m_ref[0, :, 2 * D_MODEL:3 * D_MODEL] * mixed
        o_ref[0, rows, :] = xn * lax.rsqrt(jnp.mean(xn * xn, axis=-1, keepdims=True) + EPS) * fn_ref[...]

    chains = [(jb, kv) for jb in range(nb_step) for kv in range(ATT_KV)]
    ahead = [scores(*chains[t]) for t in range(ATT_AHEAD)]
    halves, pending = [], []
    for c, (jb, kv) in enumerate(chains):
        s = ahead.pop(0)
        if c + ATT_AHEAD < len(chains):
            ahead.append(scores(*chains[c + ATT_AHEAD]))
        halves.append(attend(jb, kv, s))
        if pending:
            finish(*pending.pop())
        if kv == ATT_KV - 1:
            pending.append((jb, gated_block(jb, halves)))
            halves = []
    finish(*pending.pop())


def _attention_output(aq, ak, av, ck, cv, ag, sink2, ret, x, mod3, w_out_l, fnw):
    B, L, _ = aq.shape
    lc = ck.shape[2]
    tq = ATT_ROWS
    row = lambda n: pl.BlockSpec((1, tq, n), lambda b, i: (b, i, 0))
    full = lambda n: pl.BlockSpec((1, ATT_KV, n, LANES), lambda b, i: (b, 0, 0, 0))
    return pl.pallas_call(
        _att_kernel,
        out_shape=jax.ShapeDtypeStruct((B, L, D_MODEL), F32),
        grid=(B, L // tq),
        in_specs=[row(ATT_W), full(L), full(L), full(lc), full(lc), row(ATT_W),
                  pl.BlockSpec(sink2.shape, lambda b, i: (0, 0)),
                  row(RET_W), row(D_MODEL),
                  pl.BlockSpec((1, 1, 3 * D_MODEL), lambda b, i: (b, 0, 0)),
                  pl.BlockSpec((RET_W + ATT_W, D_MODEL), lambda b, i: (0, 0), pipeline_mode=pl.Buffered(1)),
                  pl.BlockSpec((1, D_MODEL), lambda b, i: (0, 0))],
        out_specs=row(D_MODEL),
        scratch_shapes=[pltpu.VMEM((RET_W + ATT_W, D_MODEL), BF16)],
        compiler_params=pltpu.CompilerParams(
            dimension_semantics=("arbitrary", "arbitrary"), vmem_limit_bytes=VMEM_LIMIT),
        name="att",
    )(aq, ak, av, ck, cv, ag, sink2, ret, x, mod3, w_out_l, fnw)


def _rope_tables(L):
    pos = np.arange(L)
    rows, cols = (pos // GRID_W).astype(np.float64), (pos % GRID_W).astype(np.float64)

    def tables(dh):
        nf = dh // 4
        inv = ROPE_BASE ** (-np.arange(nf, dtype=np.float64) / nf)
        ang = np.concatenate([rows[:, None] * inv, cols[:, None] * inv], axis=-1)
        cos, sin = np.cos(ang), np.sin(ang)
        reps = LANES // dh
        return (jnp.asarray(np.tile(np.concatenate([cos, cos], axis=-1), (1, reps)), F32),
                jnp.asarray(np.tile(np.concatenate([-sin, sin], axis=-1), (1, reps)), F32))

    cr, sr = tables(RET_D)
    ca, sa = tables(ATT_DH)
    return cr, sr, ca, sa


def kernel(x, c, ctx, c_ctx, w_ada, b_ada, w_in, ret_decay_logit, ret_gn_w, att_sink, w_out, final_norm_w):
    B, L, _ = x.shape
    assert w_ada.shape[0] == 1, "single-layer trunk"

    mod3 = _modulation(c, c_ctx, w_ada[0], b_ada[0][None, :])

    dl = jnp.broadcast_to(ret_decay_logit[0][:, :, None, None], (2, RET_HEADS, SUBLANES, LANES))

    r, rvt, aq, ak, av, ag, sf, sb, ck, cv = _project(x, ctx, mod3, w_in[0], _rope_tables(L), dl)
    ret = _retention(r, rvt, sf, sb, dl, ret_gn_w[0].reshape(RET_HEADS, 1, RET_D))

    sink2 = jnp.repeat(att_sink[0].reshape(ATT_KV, ATT_HEADS // ATT_KV), LANES, axis=1)
    return _attention_output(aq, ak, av, ck, cv, ag, sink2, ret, x, mod3, w_out[0], final_norm_w[None, :])
```

```python
import numpy as np
import jax
import jax.numpy as jnp
from jax import lax
from jax.experimental import pallas as pl
from jax.experimental.pallas import tpu as pltpu

F32 = jnp.float32
BF16 = jnp.bfloat16

D_MODEL = 1024
GRID_W = 64
RET_HEADS = 4
RET_D = 128
RET_W = RET_HEADS * RET_D
ATT_HEADS = 8
ATT_KV = 2
ATT_DH = 64
ATT_W = ATT_HEADS * ATT_DH
ATT_BLOCK = 128
WINDOW = 128
assert WINDOW == ATT_BLOCK
ROPE_BASE = 10000.0
EPS = 1e-6
NEG = -1e30
LOG2E = 1.4426950408889634
IN_COLS = 4 * RET_W + 2 * ATT_W + 2 * ATT_KV * ATT_DH
OFF_RQ, OFF_RK, OFF_RV, OFF_RG = 0, RET_W, 2 * RET_W, 3 * RET_W
OFF_AQ = 4 * RET_W
OFF_AK = OFF_AQ + ATT_W
OFF_AV = OFF_AK + ATT_KV * ATT_DH
OFF_AG = OFF_AV + ATT_KV * ATT_DH

LANES = 128
SUBLANES = 8
VMEM_LIMIT = 60 * 1024 * 1024

MOD_COLS = 768
PROJ_ROWS = 1024
PROJ_SUB = 256
RET_CHUNK = 256
RET_PAIR = 2
ATT_ROWS = 1024
ATT_AHEAD = 2


def _silu(t):
    return t / (1.0 + jnp.exp(-t))


def _log_sigmoid(t):
    return -(jnp.maximum(-t, 0.0) + jnp.log1p(jnp.exp(-jnp.abs(t))))


def _mod_kernel(c_ref, cx_ref, w_ref, b_ref, o_ref):
    n = o_ref.shape[0]
    cc = jnp.concatenate([c_ref[...], jnp.broadcast_to(cx_ref[...], (n - c_ref.shape[0], D_MODEL))], axis=0)
    s, w = _silu(cc), w_ref[...]
    s_hi, w_hi = s.astype(BF16), w.astype(BF16)
    s_lo = (s - s_hi.astype(F32)).astype(BF16)
    w_lo = (w - w_hi.astype(F32)).astype(BF16)
    a = jnp.dot(jnp.concatenate([s_hi, s_lo], axis=0), w_hi, preferred_element_type=F32)
    o_ref[:, 0, :] = a[0:n] + a[n:2 * n] + jnp.dot(s_hi, w_lo, preferred_element_type=F32) + b_ref[...]


def _modulation(c, c_ctx, w_ada, b_ada):
    B, n = c.shape[0], w_ada.shape[1]
    bn = MOD_COLS
    return pl.pallas_call(
        _mod_kernel,
        out_shape=jax.ShapeDtypeStruct((2 * B, 1, n), F32),
        grid=(n // bn,),
        in_specs=[pl.BlockSpec(c.shape, lambda i: (0, 0)),
                  pl.BlockSpec((1, D_MODEL), lambda i: (0, 0)),
                  pl.BlockSpec((D_MODEL, bn), lambda i: (0, i)),
                  pl.BlockSpec((1, bn), lambda i: (0, i))],
        out_specs=pl.BlockSpec((2 * B, 1, bn), lambda i: (0, 0, i)),
        compiler_params=pltpu.CompilerParams(dimension_semantics=("arbitrary",)),
        name="mod",
    )(c, c_ctx[None, :], w_ada, b_ada)


def _norm_mod(x, m_ref):
    shift = m_ref[0, :, 0:D_MODEL]
    scale = m_ref[0, :, D_MODEL:2 * D_MODEL]
    h = x * lax.rsqrt(jnp.mean(x * x, axis=-1, keepdims=True) + EPS)
    return h * (1.0 + scale) + shift


def _rope_ret(t, cos, sin):
    return t * cos + pltpu.roll(t, RET_D // 2, 1) * sin


def _rope_att(t, cos, sin, low_half):
    half = ATT_DH // 2
    partner = jnp.where(low_half, pltpu.roll(t, LANES - half, 1), pltpu.roll(t, half, 1))
    return t * cos + partner * sin


def _store_kv_groups(k_ref, v_ref, rows, k, v):
    low = lax.broadcasted_iota(jnp.int32, k.shape, 1) < ATT_DH
    k_ref[0, 0, rows, :] = jnp.where(low, k, 0.0).astype(BF16)
    k_ref[0, 1, rows, :] = jnp.where(low, 0.0, k).astype(BF16)
    v_ref[0, 0, rows, :] = jnp.where(low, v, 1.0).astype(BF16)
    v_ref[0, 1, rows, :] = jnp.where(low, 1.0, v).astype(BF16)


def _pair_heads(nat):
    low = lax.broadcasted_iota(jnp.int32, nat[0].shape, 1) < ATT_DH
    out = []
    for j in range(len(nat)):
        a, b = nat[j // 2], nat[2 + j // 2]
        if j % 2 == 0:
            out.append(jnp.where(low, a, pltpu.roll(b, ATT_DH, 1)))
        else:
            out.append(jnp.where(low, pltpu.roll(a, ATT_DH, 1), b))
    return out


def _context_block(x_ref, m_ref, w_ref, dl_ref, sf_ref, sb_ref, ck_ref, cv_ref):
    lc = x_ref.shape[1]
    hb = _norm_mod(x_ref[0], m_ref).astype(BF16)
    yk = jnp.dot(hb, w_ref[:, OFF_RK:OFF_RV], preferred_element_type=F32)
    yv = jnp.dot(hb, w_ref[:, OFF_RV:OFF_RG], preferred_element_type=F32)
    ya = jnp.dot(hb, w_ref[:, OFF_AK:OFF_AG], preferred_element_type=F32)
    _store_kv_groups(ck_ref, cv_ref, slice(0, lc), ya[:, 0:LANES], ya[:, LANES:2 * LANES])
    lg = _log_sigmoid(dl_ref[...])
    pos = lax.broadcasted_iota(jnp.int32, (lc, 1), 0).astype(F32)
    dn = (((0,), (0,)), ((), ()))
    for h in range(RET_HEADS):
        k = yk[:, h * RET_D:(h + 1) * RET_D] * (RET_D ** -0.5)
        v = yv[:, h * RET_D:(h + 1) * RET_D].astype(BF16)
        wf = jnp.exp(lg[0, h, 0:1, :] * (lc - 1.0 - pos))
        wb = jnp.exp(lg[1, h, 0:1, :] * pos)
        sf_ref[0, h] = lax.dot_general(v, (k * wf).astype(BF16), dn, preferred_element_type=F32)
        sb_ref[0, h] = lax.dot_general(v, (k * wb).astype(BF16), dn, preferred_element_type=F32)


def _proj_kernel(x_ref, m_ref, w_ref, cr_ref, sr_ref, ca_ref, sa_ref, cx_ref, mc_ref, dl_ref,
                 r_ref, rvt_ref, aq_ref, ak_ref, av_ref, ag_ref, sf_ref, sb_ref, ck_ref, cv_ref,
                 w_scr):

    @pl.when((pl.program_id(0) == 0) & (pl.program_id(1) == 0))
    def _():
        for r in range(0, D_MODEL, LANES):
            rr = slice(r, r + LANES)
            for c0, c1 in ((OFF_RQ, OFF_RK), (OFF_RK, OFF_RV), (OFF_RV, OFF_RG), (OFF_RG, OFF_AQ),
                           (OFF_AK, OFF_AG)):
                w_scr[rr, c0:c1] = w_ref[rr, c0:c1].astype(BF16)
            for off in (OFF_AQ, OFF_AG):
                nat = [w_ref[rr, off + j * LANES:off + (j + 1) * LANES] for j in range(ATT_W // LANES)]
                for j, blk in enumerate(_pair_heads(nat)):
                    w_scr[rr, off + j * LANES:off + (j + 1) * LANES] = blk.astype(BF16)

    @pl.when(pl.program_id(1) == 0)
    def _():
        _context_block(cx_ref, mc_ref, w_scr, dl_ref, sf_ref, sb_ref, ck_ref, cv_ref)

    for t in range(x_ref.shape[1] // PROJ_SUB):
        rows = slice(t * PROJ_SUB, (t + 1) * PROJ_SUB)
        hb = _norm_mod(x_ref[0, rows, :], m_ref).astype(BF16)
        pos = pl.ds(pl.multiple_of(pl.program_id(1) * x_ref.shape[1] + t * PROJ_SUB, PROJ_SUB), PROJ_SUB)
        cr, sr, ca, sa = cr_ref[pos, :], sr_ref[pos, :], ca_ref[pos, :], sa_ref[pos, :]
        low_half = (lax.broadcasted_iota(jnp.int32, ca.shape, 1) % ATT_DH) < (ATT_DH // 2)

        def mm(c0, c1):
            return jnp.dot(hb, w_scr[:, c0:c1], preferred_element_type=F32)

        y = mm(OFF_RQ, OFF_RK)
        for h in range(RET_HEADS):
            sl = slice(h * LANES, (h + 1) * LANES)
            r_ref[0, rows, OFF_RQ + h * LANES:OFF_RQ + (h + 1) * LANES] = _rope_ret(y[:, sl], cr, sr).astype(BF16)
        y = mm(OFF_RK, OFF_RV) * (RET_D ** -0.5)
        for h in range(RET_HEADS):
            sl = slice(h * LANES, (h + 1) * LANES)
            r_ref[0, rows, OFF_RK + h * LANES:OFF_RK + (h + 1) * LANES] = _rope_ret(y[:, sl], cr, sr).astype(BF16)
        y = mm(OFF_RV, OFF_RG)
        for h in range(RET_HEADS):
            sl = slice(h * LANES, (h + 1) * LANES)
            rvt_ref[0, sl, rows] = y[:, sl].T.astype(BF16)
        r_ref[0, rows, 2 * RET_W:3 * RET_W] = _silu(mm(OFF_RG, OFF_AQ)).astype(BF16)
        y = mm(OFF_AQ, OFF_AK)
        for j in range(ATT_W // LANES):
            sl = slice(j * LANES, (j + 1) * LANES)
            aq_ref[0, rows, sl] = (_rope_att(y[:, sl], ca, sa, low_half) * (ATT_DH ** -0.5 * LOG2E)).astype(BF16)
        y = mm(OFF_AK, OFF_AG)
        _store_kv_groups(ak_ref, av_ref, rows, _rope_att(y[:, 0:LANES], ca, sa, low_half), y[:, LANES:2 * LANES])
        ag_ref[0, rows, :] = _silu(mm(OFF_AG, IN_COLS)).astype(BF16)


def _project(x, ctx, mod3, w_in_l, tabs, dl):
    B, L, _ = x.shape
    lc = ctx.shape[1]
    tm = PROJ_ROWS
    ctx_row = mod3.shape[0] // 2
    st_shape = jax.ShapeDtypeStruct((B, RET_HEADS, RET_D, RET_D), F32)
    st_spec = pl.BlockSpec((1, RET_HEADS, RET_D, RET_D), lambda b, i: (b, 0, 0, 0))
    ckv_shape = jax.ShapeDtypeStruct((B, ATT_KV, lc, LANES), BF16)
    ckv_spec = pl.BlockSpec((1, ATT_KV, lc, LANES), lambda b, i: (b, 0, 0, 0))
    tab_spec = pl.BlockSpec((L, LANES), lambda b, i: (0, 0), pipeline_mode=pl.Buffered(1))
    row = lambda n: pl.BlockSpec((1, tm, n), lambda b, i: (b, i, 0))
    kv_spec = pl.BlockSpec((1, ATT_KV, tm, LANES), lambda b, i: (b, 0, i, 0))
    kv_shape = jax.ShapeDtypeStruct((B, ATT_KV, L, LANES), BF16)
    return pl.pallas_call(
        _proj_kernel,
        out_shape=(jax.ShapeDtypeStruct((B, L, 3 * RET_W), BF16),
                   jax.ShapeDtypeStruct((B, RET_W, L), BF16),
                   jax.ShapeDtypeStruct((B, L, ATT_W), BF16),
                   kv_shape, kv_shape,
                   jax.ShapeDtypeStruct((B, L, ATT_W), BF16),
                   st_shape, st_shape, ckv_shape, ckv_shape),
        grid=(B, L // tm),
        in_specs=[row(D_MODEL),
                  pl.BlockSpec((1, 1, 3 * D_MODEL), lambda b, i: (b, 0, 0)),
                  pl.BlockSpec((D_MODEL, IN_COLS), lambda b, i: (0, 0), pipeline_mode=pl.Buffered(1)),
                  tab_spec, tab_spec, tab_spec, tab_spec,
                  pl.BlockSpec((1, lc, D_MODEL), lambda b, i: (b, 0, 0)),
                  pl.BlockSpec((1, 1, 3 * D_MODEL), lambda b, i: (ctx_row, 0, 0)),
                  pl.BlockSpec(dl.shape, lambda b, i: (0, 0, 0, 0))],
        out_specs=(row(3 * RET_W), pl.BlockSpec((1, RET_W, tm), lambda b, i: (b, 0, i)),
                   row(ATT_W), kv_spec, kv_spec, row(ATT_W),
                   st_spec, st_spec, ckv_spec, ckv_spec),
        scratch_shapes=[pltpu.VMEM((D_MODEL, IN_COLS), BF16)],
        compiler_params=pltpu.CompilerParams(
            dimension_semantics=("arbitrary", "arbitrary"), vmem_limit_bytes=VMEM_LIMIT),
        name="proj",
    )(x, mod3, w_in_l, *tabs, ctx, mod3, dl)


def _ret_kernel(q_ref, k_ref, vt_ref, g_ref, sf_ref, sb_ref, dl_ref, gn_ref, o_ref,
                uf_scr, rb_scr, kdec_scr, qdec_scr, cdec_scr, decay_scr):
    L = q_ref.shape[1]
    C = RET_CHUNK
    nch = L // C
    heads = range(RET_PAIR)
    lanes = lambda h: slice(h * RET_D, (h + 1) * RET_D)

    @pl.when(pl.program_id(1) == 0)
    def _():
        lg = _log_sigmoid(dl_ref[...])
        pos = lax.broadcasted_iota(jnp.int32, (C, 1), 0).astype(F32)
        posl = lax.broadcasted_iota(jnp.int32, (SUBLANES, C), 1).astype(F32)
        diff = (lax.broadcasted_iota(jnp.int32, (C, C), 1)
                - lax.broadcasted_iota(jnp.int32, (C, C), 0)).astype(F32)
        for h in heads:
            lgf, lgb = lg[0, h, 0:1, :], lg[1, h, 0:1, :]
            lgf1, lgb1 = lgf[:, 0:1], lgb[:, 0:1]
            kdec_scr[h, 0] = jnp.exp(lgf * (C - 1.0 - pos))
            kdec_scr[h, 1] = jnp.exp(lgb * pos)
            qdec_scr[h, 0] = jnp.exp(lgf1 * (posl + 1.0))
            qdec_scr[h, 1] = jnp.exp(lgb1 * (C - posl))
            cdec_scr[h, 0] = jnp.broadcast_to(jnp.exp(lgf * C), (SUBLANES, LANES))
            cdec_scr[h, 1] = jnp.broadcast_to(jnp.exp(lgb * C), (SUBLANES, LANES))
            decay_scr[h] = jnp.where(diff >= 0.0, jnp.exp(lgf1 * jnp.maximum(diff, 0.0)),
                                     jnp.exp(lgb1 * jnp.maximum(-diff, 0.0)))

    nt = (((1,), (1,)), ((), ()))
    chunk = lambda n: slice(n * C, (n + 1) * C)

    rb = [sb_ref[0, h] for h in heads]
    for n in reversed(range(nch)):
        for h in heads:
            kc = k_ref[0, chunk(n), lanes(h)].astype(F32)
            kd = jnp.concatenate([(kc * kdec_scr[h, 0]).astype(BF16), (kc * kdec_scr[h, 1]).astype(BF16)],
                                 axis=1)
            u = jnp.dot(vt_ref[0, lanes(h), chunk(n)], kd, preferred_element_type=F32)
            uf_scr[h, n] = u[:, 0:RET_D]
            rb_scr[h, n] = rb[h].astype(BF16)
            rb[h] = rb[h] * cdec_scr[h, 1, 0:1, :] + u[:, RET_D:2 * RET_D]

    def kq(h, n):
        return lax.dot_general(k_ref[0, chunk(n), lanes(h)], q_ref[0, chunk(n), lanes(h)], nt,
                               preferred_element_type=F32)

    rf = [sf_ref[0, h] for h in heads]
    s_next = [kq(h, 0) for h in heads]
    for n in range(nch):
        for h in heads:
            s = s_next[h]
            q = q_ref[0, chunk(n), lanes(h)]
            states = jnp.concatenate([rf[h].astype(BF16), rb_scr[h, n]], axis=0)
            cross = lax.dot_general(states, q, nt, preferred_element_type=F32)
            if n + 1 < nch:
                s_next[h] = kq(h, n + 1)
                rf[h] = rf[h] * cdec_scr[h, 0, 0:1, :] + uf_scr[h, n]
            inner = jnp.dot(vt_ref[0, lanes(h), chunk(n)], (s * decay_scr[h]).astype(BF16),
                            preferred_element_type=F32)
            y = (inner + cross[0:RET_D] * qdec_scr[h, 0, 0:1, :]
                 + cross[RET_D:2 * RET_D] * qdec_scr[h, 1, 0:1, :])
            mu = jnp.mean(y, axis=0, keepdims=True)
            yc = y - mu
            var = jnp.mean(yc * yc, axis=0, keepdims=True)
            yn = (yc * lax.rsqrt(var + EPS)).T * gn_ref[h]
            o_ref[0, chunk(n), lanes(h)] = (yn * g_ref[0, chunk(n), lanes(h)].astype(F32)).astype(BF16)


def _retention(r, rvt, sf, sb, dl, gn):
    B, L, _ = r.shape
    P, C = RET_PAIR, RET_CHUNK
    col = lambda off: pl.BlockSpec((1, L, P * RET_D), lambda hp, b: (b, 0, off + hp))
    st_spec = pl.BlockSpec((1, P, RET_D, RET_D), lambda hp, b: (b, hp, 0, 0))
    npair = RET_HEADS // P
    return pl.pallas_call(
        _ret_kernel,
        out_shape=jax.ShapeDtypeStruct((B, L, RET_W), BF16),
        grid=(npair, B),
        in_specs=[col(0), col(npair),
                  pl.BlockSpec((1, P * RET_D, L), lambda hp, b: (b, hp, 0)),
                  col(2 * npair),
                  st_spec, st_spec,
                  pl.BlockSpec((2, P, SUBLANES, LANES), lambda hp, b: (0, hp, 0, 0)),
                  pl.BlockSpec((P, 1, RET_D), lambda hp, b: (hp, 0, 0))],
        out_specs=pl.BlockSpec((1, L, P * RET_D), lambda hp, b: (b, 0, hp)),
        scratch_shapes=[pltpu.VMEM((P, L // C, RET_D, RET_D), F32),
                        pltpu.VMEM((P, L // C, RET_D, RET_D), BF16),
                        pltpu.VMEM((P, 2, C, LANES), F32),
                        pltpu.VMEM((P, 2, SUBLANES, C), F32),
                        pltpu.VMEM((P, 2, SUBLANES, LANES), F32),
                        pltpu.VMEM((P, C, C), F32)],
        compiler_params=pltpu.CompilerParams(
            dimension_semantics=("arbitrary", "arbitrary"), vmem_limit_bytes=VMEM_LIMIT),
        name="ret",
    )(r, r, rvt, r, sf, sb, dl, gn)


def _att_kernel(q_ref, k_ref, v_ref, ck_ref, cv_ref, g_ref, sink_ref, r_ref, x_ref, m_ref, w_ref, fn_ref,
                o_ref, w_scr):
    @pl.when((pl.program_id(0) == 0) & (pl.program_id(1) == 0))
    def _():
        w_scr[0:RET_W, :] = w_ref[0:RET_W, :].astype(BF16)
        for j in range(ATT_HEADS // ATT_KV):
            for half in range(ATT_KV):
                dst = RET_W + j * LANES + half * ATT_DH
                src = RET_W + (j + half * (ATT_HEADS // ATT_KV)) * ATT_DH
                w_scr[dst:dst + ATT_DH, :] = w_ref[src:src + ATT_DH, :].astype(BF16)

    L = k_ref.shape[2]
    nb_total = L // ATT_BLOCK
    nb_step = q_ref.shape[1] // ATT_BLOCK
    nhb = ATT_W // LANES
    i = pl.program_id(1)
    r_i = lax.broadcasted_iota(jnp.int32, (ATT_BLOCK, ATT_BLOCK), 0)
    q_i = lax.broadcasted_iota(jnp.int32, (ATT_BLOCK, ATT_BLOCK), 1)
    band_prev = jnp.where(q_i <= r_i, 0.0, NEG).astype(F32)
    band_next = jnp.where(r_i <= q_i, 0.0, NEG).astype(F32)
    nt = (((1,), (1,)), ((), ()))
    tn = (((0,), (0,)), ((), ()))

    def key_rows(jb):
        n = i * nb_step + jb
        blk = lambda t: pl.ds(pl.multiple_of(t * ATT_BLOCK, ATT_BLOCK), ATT_BLOCK)
        return n, (blk(jnp.maximum(n - 1, 0)), blk(n), blk(jnp.minimum(n + 1, nb_total - 1)))

    def scores(jb, kv):
        n, (p_rows, o_rows, n_rows) = key_rows(jb)
        q = q_ref[0, jb * ATT_BLOCK:(jb + 1) * ATT_BLOCK, :]
        qall = jnp.concatenate([q[:, j * LANES:(j + 1) * LANES] for j in range(nhb)], axis=0)
        kall = jnp.concatenate([ck_ref[0, kv], k_ref[0, kv, p_rows, :], k_ref[0, kv, o_rows, :],
                                k_ref[0, kv, n_rows, :]], axis=0)
        return lax.dot_general(kall, qall, nt, preferred_element_type=F32)

    def attend(jb, kv, s):
        n, (p_rows, o_rows, n_rows) = key_rows(jb)
        lc = ck_ref.shape[2]
        bias_prev = jnp.concatenate([band_prev + jnp.where(n == 0, NEG, 0.0)] * nhb, axis=1)
        bias_next = jnp.concatenate([band_next + jnp.where(n == nb_total - 1, NEG, 0.0)] * nhb, axis=1)
        parts = [s[0:lc], s[lc:lc + ATT_BLOCK] + bias_prev, s[lc + ATT_BLOCK:lc + 2 * ATT_BLOCK],
                 s[lc + 2 * ATT_BLOCK:] + bias_next]
        sink = sink_ref[kv:kv + 1, :] * LOG2E
        m = sink
        for t in parts:
            m = jnp.maximum(m, jnp.max(t, axis=0, keepdims=True))
        p = jnp.concatenate([jnp.exp2(t - m).astype(BF16) for t in parts], axis=0)
        vaug = jnp.concatenate([cv_ref[0, kv], v_ref[0, kv, p_rows, :], v_ref[0, kv, o_rows, :],
                                v_ref[0, kv, n_rows, :]], axis=0)
        o = lax.dot_general(vaug, p, tn, preferred_element_type=F32)
        if kv == 0:
            val, ones = slice(0, ATT_DH), slice(ATT_DH, ATT_DH + 1)
        else:
            val, ones = slice(ATT_DH, 2 * ATT_DH), slice(0, 1)
        den = o[ones] + jnp.exp2(sink - m)
        return o[val] * (1.0 / den)

    def gated_block(jb, halves):
        rows = slice(jb * ATT_BLOCK, (jb + 1) * ATT_BLOCK)
        comb = jnp.concatenate(halves, axis=0)
        return jnp.concatenate(
            [(comb[:, j * LANES:(j + 1) * LANES].T
              * g_ref[0, rows, j * LANES:(j + 1) * LANES].astype(F32)).astype(BF16) for j in range(nhb)], axis=1)

    def finish(jb, att_blk):
        rows = slice(jb * ATT_BLOCK, (jb + 1) * ATT_BLOCK)
        mix_in = jnp.concatenate([r_ref[0, rows, :], att_blk], axis=1)
        mixed = jnp.dot(mix_in, w_scr[...], preferred_element_type=F32)
        xn = x_ref[0, rows, :] + m_ref[0, :, 2 * D_MODEL:3 * D_MODEL] * mixed
        o_ref[0, rows, :] = xn * lax.rsqrt(jnp.mean(xn * xn, axis=-1, keepdims=True) + EPS) * fn_ref[...]

    chains = [(jb, kv) for jb in range(nb_step) for kv in range(ATT_KV)]
    ahead = [scores(*chains[t]) for t in range(ATT_AHEAD)]
    halves, pending = [], []
    for c, (jb, kv) in enumerate(chains):
        s = ahead.pop(0)
        if c + ATT_AHEAD < len(chains):
            ahead.append(scores(*chains[c + ATT_AHEAD]))
        halves.append(attend(jb, kv, s))
        if pending:
            finish(*pending.pop())
        if kv == ATT_KV - 1:
            pending.append((jb, gated_block(jb, halves)))
            halves = []
    finish(*pending.pop())


def _attention_output(aq, ak, av, ck, cv, ag, sink2, ret, x, mod3, w_out_l, fnw):
    B, L, _ = aq.shape
    lc = ck.shape[2]
    tq = ATT_ROWS
    row = lambda n: pl.BlockSpec((1, tq, n), lambda b, i: (b, i, 0))
    full = lambda n: pl.BlockSpec((1, ATT_KV, n, LANES), lambda b, i: (b, 0, 0, 0))
    return pl.pallas_call(
        _att_kernel,
        out_shape=jax.ShapeDtypeStruct((B, L, D_MODEL), F32),
        grid=(B, L // tq),
        in_specs=[row(ATT_W), full(L), full(L), full(lc), full(lc), row(ATT_W),
                  pl.BlockSpec(sink2.shape, lambda b, i: (0, 0)),
                  row(RET_W), row(D_MODEL),
                  pl.BlockSpec((1, 1, 3 * D_MODEL), lambda b, i: (b, 0, 0)),
                  pl.BlockSpec((RET_W + ATT_W, D_MODEL), lambda b, i: (0, 0), pipeline_mode=pl.Buffered(1)),
                  pl.BlockSpec((1, D_MODEL), lambda b, i: (0, 0))],
        out_specs=row(D_MODEL),
        scratch_shapes=[pltpu.VMEM((RET_W + ATT_W, D_MODEL), BF16)],
        compiler_params=pltpu.CompilerParams(
            dimension_semantics=("arbitrary", "arbitrary"), vmem_limit_bytes=VMEM_LIMIT),
        name="att",
    )(aq, ak, av, ck, cv, ag, sink2, ret, x, mod3, w_out_l, fnw)


def _rope_tables(L):
    pos = np.arange(L)
    rows, cols = (pos // GRID_W).astype(np.float64), (pos % GRID_W).astype(np.float64)

    def tables(dh):
        nf = dh // 4
        inv = ROPE_BASE ** (-np.arange(nf, dtype=np.float64) / nf)
        ang = np.concatenate([rows[:, None] * inv, cols[:, None] * inv], axis=-1)
        cos, sin = np.cos(ang), np.sin(ang)
        reps = LANES // dh
        return (jnp.asarray(np.tile(np.concatenate([cos, cos], axis=-1), (1, reps)), F32),
                jnp.asarray(np.tile(np.concatenate([-sin, sin], axis=-1), (1, reps)), F32))

    cr, sr = tables(RET_D)
    ca, sa = tables(ATT_DH)
    return cr, sr, ca, sa


def kernel(x, c, ctx, c_ctx, w_ada, b_ada, w_in, ret_decay_logit, ret_gn_w, att_sink, w_out, final_norm_w):
    B, L, _ = x.shape
    assert w_ada.shape[0] == 1, "single-layer trunk"

    mod3 = _modulation(c, c_ctx, w_ada[0], b_ada[0][None, :])

    dl = jnp.broadcast_to(ret_decay_logit[0][:, :, None, None], (2, RET_HEADS, SUBLANES, LANES))

    r, rvt, aq, ak, av, ag, sf, sb, ck, cv = _project(x, ctx, mod3, w_in[0], _rope_tables(L), dl)
    ret = _retention(r, rvt, sf, sb, dl, ret_gn_w[0].reshape(RET_HEADS, 1, RET_D))

    sink2 = jnp.repeat(att_sink[0].reshape(ATT_KV, ATT_HEADS // ATT_KV), LANES, axis=1)
    return _attention_output(aq, ak, av, ck, cv, ag, sink2, ret, x, mod3, w_out[0], final_norm_w[None, :])
```

```python
import numpy as np
import jax
import jax.numpy as jnp
from jax import lax
from jax.experimental import pallas as pl
from jax.experimental.pallas import tpu as pltpu

F32 = jnp.float32
BF16 = jnp.bfloat16

D_MODEL = 1024
GRID_W = 64
RET_HEADS = 4
RET_D = 128
RET_W = RET_HEADS * RET_D
ATT_HEADS = 8
ATT_KV = 2
ATT_DH = 64
ATT_W = ATT_HEADS * ATT_DH
ATT_BLOCK = 128
WINDOW = 128
assert WINDOW == ATT_BLOCK
ROPE_BASE = 10000.0
EPS = 1e-6
NEG = -1e30
LOG2E = 1.4426950408889634
IN_COLS = 4 * RET_W + 2 * ATT_W + 2 * ATT_KV * ATT_DH
OFF_RQ, OFF_RK, OFF_RV, OFF_RG = 0, RET_W, 2 * RET_W, 3 * RET_W
OFF_AQ = 4 * RET_W
OFF_AK = OFF_AQ + ATT_W
OFF_AV = OFF_AK + ATT_KV * ATT_DH
OFF_AG = OFF_AV + ATT_KV * ATT_DH

LANES = 128
SUBLANES = 8
VMEM_LIMIT = 56 * 1024 * 1024

MOD_COLS = 768
PROJ_ROWS = 1024
PROJ_SUB = 256
RET_CHUNK = 256
RET_PAIR = 2
ATT_ROWS = 1024
ATT_AHEAD = 2


def _silu(t):
    return t / (1.0 + jnp.exp(-t))


def _log_sigmoid(t):
    return -(jnp.maximum(-t, 0.0) + jnp.log1p(jnp.exp(-jnp.abs(t))))


def _mod_kernel(c_ref, cx_ref, w_ref, b_ref, o_ref):
    n = o_ref.shape[0]
    cc = jnp.concatenate([c_ref[...], jnp.broadcast_to(cx_ref[...], (n - c_ref.shape[0], D_MODEL))], axis=0)
    s, w = _silu(cc), w_ref[...]
    s_hi, w_hi = s.astype(BF16), w.astype(BF16)
    s_lo = (s - s_hi.astype(F32)).astype(BF16)
    w_lo = (w - w_hi.astype(F32)).astype(BF16)
    a = jnp.dot(jnp.concatenate([s_hi, s_lo], axis=0), w_hi, preferred_element_type=F32)
    o_ref[:, 0, :] = a[0:n] + a[n:2 * n] + jnp.dot(s_hi, w_lo, preferred_element_type=F32) + b_ref[...]


def _modulation(c, c_ctx, w_ada, b_ada):
    B, n = c.shape[0], w_ada.shape[1]
    bn = MOD_COLS
    return pl.pallas_call(
        _mod_kernel,
        out_shape=jax.ShapeDtypeStruct((2 * B, 1, n), F32),
        grid=(n // bn,),
        in_specs=[pl.BlockSpec(c.shape, lambda i: (0, 0)),
                  pl.BlockSpec((1, D_MODEL), lambda i: (0, 0)),
                  pl.BlockSpec((D_MODEL, bn), lambda i: (0, i)),
                  pl.BlockSpec((1, bn), lambda i: (0, i))],
        out_specs=pl.BlockSpec((2 * B, 1, bn), lambda i: (0, 0, i)),
        compiler_params=pltpu.CompilerParams(dimension_semantics=("arbitrary",)),
        name="mod",
    )(c, c_ctx[None, :], w_ada, b_ada)


def _norm_mod(x, m_ref):
    shift = m_ref[0, :, 0:D_MODEL]
    scale = m_ref[0, :, D_MODEL:2 * D_MODEL]
    h = x * lax.rsqrt(jnp.mean(x * x, axis=-1, keepdims=True) + EPS)
    return h * (1.0 + scale) + shift


def _rope_ret(t, cos, sin):
    return t * cos + pltpu.roll(t, RET_D // 2, 1) * sin


def _rope_att(t, cos, sin, low_half):
    half = ATT_DH // 2
    partner = jnp.where(low_half, pltpu.roll(t, LANES - half, 1), pltpu.roll(t, half, 1))
    return t * cos + partner * sin


def _store_kv_groups(k_ref, v_ref, rows, k, v):
    low = lax.broadcasted_iota(jnp.int32, k.shape, 1) < ATT_DH
    k_ref[0, 0, rows, :] = jnp.where(low, k, 0.0).astype(BF16)
    k_ref[0, 1, rows, :] = jnp.where(low, 0.0, k).astype(BF16)
    v_ref[0, 0, rows, :] = jnp.where(low, v, 1.0).astype(BF16)
    v_ref[0, 1, rows, :] = jnp.where(low, 1.0, v).astype(BF16)


def _pair_heads(nat):
    low = lax.broadcasted_iota(jnp.int32, nat[0].shape, 1) < ATT_DH
    out = []
    for j in range(len(nat)):
        a, b = nat[j // 2], nat[2 + j // 2]
        if j % 2 == 0:
            out.append(jnp.where(low, a, pltpu.roll(b, ATT_DH, 1)))
        else:
            out.append(jnp.where(low, pltpu.roll(a, ATT_DH, 1), b))
    return out


def _context_block(x_ref, m_ref, w_ref, dl_ref, sf_ref, sb_ref, ck_ref, cv_ref):
    lc = x_ref.shape[1]
    hb = _norm_mod(x_ref[0], m_ref).astype(BF16)
    yk = jnp.dot(hb, w_ref[:, OFF_RK:OFF_RV], preferred_element_type=F32)
    yv = jnp.dot(hb, w_ref[:, OFF_RV:OFF_RG], preferred_element_type=F32)
    ya = jnp.dot(hb, w_ref[:, OFF_AK:OFF_AG], preferred_element_type=F32)
    _store_kv_groups(ck_ref, cv_ref, slice(0, lc), ya[:, 0:LANES], ya[:, LANES:2 * LANES])
    lg = _log_sigmoid(dl_ref[...])
    pos = lax.broadcasted_iota(jnp.int32, (lc, 1), 0).astype(F32)
    dn = (((0,), (0,)), ((), ()))
    for h in range(RET_HEADS):
        k = yk[:, h * RET_D:(h + 1) * RET_D] * (RET_D ** -0.5)
        v = yv[:, h * RET_D:(h + 1) * RET_D].astype(BF16)
        wf = jnp.exp(lg[0, h, 0:1, :] * (lc - 1.0 - pos))
        wb = jnp.exp(lg[1, h, 0:1, :] * pos)
        sf_ref[0, h] = lax.dot_general(v, (k * wf).astype(BF16), dn, preferred_element_type=F32)
        sb_ref[0, h] = lax.dot_general(v, (k * wb).astype(BF16), dn, preferred_element_type=F32)


def _proj_kernel(x_ref, m_ref, w_ref, cr_ref, sr_ref, ca_ref, sa_ref, cx_ref, mc_ref, dl_ref,
                 r_ref, rvt_ref, aq_ref, ak_ref, av_ref, ag_ref, sf_ref, sb_ref, ck_ref, cv_ref,
                 w_scr):

    @pl.when((pl.program_id(0) == 0) & (pl.program_id(1) == 0))
    def _():
        for r in range(0, D_MODEL, LANES):
            rr = slice(r, r + LANES)
            for c0, c1 in ((OFF_RQ, OFF_RK), (OFF_RK, OFF_RV), (OFF_RV, OFF_RG), (OFF_RG, OFF_AQ),
                           (OFF_AK, OFF_AG)):
                w_scr[rr, c0:c1] = w_ref[rr, c0:c1].astype(BF16)
            for off in (OFF_AQ, OFF_AG):
                nat = [w_ref[rr, off + j * LANES:off + (j + 1) * LANES] for j in range(ATT_W // LANES)]
                for j, blk in enumerate(_pair_heads(nat)):
                    w_scr[rr, off + j * LANES:off + (j + 1) * LANES] = blk.astype(BF16)

    @pl.when(pl.program_id(1) == 0)
    def _():
        _context_block(cx_ref, mc_ref, w_scr, dl_ref, sf_ref, sb_ref, ck_ref, cv_ref)

    for t in range(x_ref.shape[1] // PROJ_SUB):
        rows = slice(t * PROJ_SUB, (t + 1) * PROJ_SUB)
        hb = _norm_mod(x_ref[0, rows, :], m_ref).astype(BF16)
        cr, sr, ca, sa = cr_ref[rows, :], sr_ref[rows, :], ca_ref[rows, :], sa_ref[rows, :]
        low_half = (lax.broadcasted_iota(jnp.int32, ca.shape, 1) % ATT_DH) < (ATT_DH // 2)

        def mm(c0, c1):
            return jnp.dot(hb, w_scr[:, c0:c1], preferred_element_type=F32)

        y = mm(OFF_RQ, OFF_RK)
        for h in range(RET_HEADS):
            sl = slice(h * LANES, (h + 1) * LANES)
            r_ref[0, rows, OFF_RQ + h * LANES:OFF_RQ + (h + 1) * LANES] = _rope_ret(y[:, sl], cr, sr).astype(BF16)
        y = mm(OFF_RK, OFF_RV) * (RET_D ** -0.5)
        for h in range(RET_HEADS):
            sl = slice(h * LANES, (h + 1) * LANES)
            r_ref[0, rows, OFF_RK + h * LANES:OFF_RK + (h + 1) * LANES] = _rope_ret(y[:, sl], cr, sr).astype(BF16)
        y = mm(OFF_RV, OFF_RG)
        for h in range(RET_HEADS):
            sl = slice(h * LANES, (h + 1) * LANES)
            rvt_ref[0, sl, rows] = y[:, sl].T.astype(BF16)
        r_ref[0, rows, 2 * RET_W:3 * RET_W] = _silu(mm(OFF_RG, OFF_AQ)).astype(BF16)
        y = mm(OFF_AQ, OFF_AK)
        for j in range(ATT_W // LANES):
            sl = slice(j * LANES, (j + 1) * LANES)
            aq_ref[0, rows, sl] = (_rope_att(y[:, sl], ca, sa, low_half) * (ATT_DH ** -0.5 * LOG2E)).astype(BF16)
        ag_ref[0, rows, :] = _silu(mm(OFF_AG, IN_COLS)).astype(BF16)
        y = mm(OFF_AK, OFF_AG)
        _store_kv_groups(ak_ref, av_ref, rows, _rope_att(y[:, 0:LANES], ca, sa, low_half), y[:, LANES:2 * LANES])


def _project(x, ctx, mod3, w_in_l, tabs, dl):
    B, L, _ = x.shape
    lc = ctx.shape[1]
    tm = PROJ_ROWS
    ctx_row = mod3.shape[0] // 2
    st_shape = jax.ShapeDtypeStruct((B, RET_HEADS, RET_D, RET_D), F32)
    st_spec = pl.BlockSpec((1, RET_HEADS, RET_D, RET_D), lambda b, i: (b, 0, 0, 0))
    ckv_shape = jax.ShapeDtypeStruct((B, ATT_KV, lc, LANES), BF16)
    ckv_spec = pl.BlockSpec((1, ATT_KV, lc, LANES), lambda b, i: (b, 0, 0, 0))
    tab_spec = pl.BlockSpec((tm, LANES), lambda b, i: (i, 0))
    row = lambda n: pl.BlockSpec((1, tm, n), lambda b, i: (b, i, 0))
    kv_spec = pl.BlockSpec((1, ATT_KV, tm, LANES), lambda b, i: (b, 0, i, 0))
    kv_shape = jax.ShapeDtypeStruct((B, ATT_KV, L, LANES), BF16)
    return pl.pallas_call(
        _proj_kernel,
        out_shape=(jax.ShapeDtypeStruct((B, L, 3 * RET_W), BF16),
                   jax.ShapeDtypeStruct((B, RET_W, L), BF16),
                   jax.ShapeDtypeStruct((B, L, ATT_W), BF16),
                   kv_shape, kv_shape,
                   jax.ShapeDtypeStruct((B, L, ATT_W), BF16),
                   st_shape, st_shape, ckv_shape, ckv_shape),
        grid=(B, L // tm),
        in_specs=[row(D_MODEL),
                  pl.BlockSpec((1, 1, 3 * D_MODEL), lambda b, i: (b, 0, 0)),
                  pl.BlockSpec((D_MODEL, IN_COLS), lambda b, i: (0, 0), pipeline_mode=pl.Buffered(1)),
                  tab_spec, tab_spec, tab_spec, tab_spec,
                  pl.BlockSpec((1, lc, D_MODEL), lambda b, i: (b, 0, 0)),
                  pl.BlockSpec((1, 1, 3 * D_MODEL), lambda b, i: (ctx_row, 0, 0)),
                  pl.BlockSpec(dl.shape, lambda b, i: (0, 0, 0, 0))],
        out_specs=(row(3 * RET_W), pl.BlockSpec((1, RET_W, tm), lambda b, i: (b, 0, i)),
                   row(ATT_W), kv_spec, kv_spec, row(ATT_W),
                   st_spec, st_spec, ckv_spec, ckv_spec),
        scratch_shapes=[pltpu.VMEM((D_MODEL, IN_COLS), BF16)],
        compiler_params=pltpu.CompilerParams(
            dimension_semantics=("arbitrary", "arbitrary"), vmem_limit_bytes=VMEM_LIMIT),
        name="proj",
    )(x, mod3, w_in_l, *tabs, ctx, mod3, dl)


def _ret_kernel(q_ref, k_ref, vt_ref, g_ref, sf_ref, sb_ref, dl_ref, gn_ref, o_ref,
                uf_scr, rb_scr, kdec_scr, qdec_scr, cdec_scr, decay_scr):
    L = q_ref.shape[1]
    C = RET_CHUNK
    nch = L // C
    heads = range(RET_PAIR)
    lanes = lambda h: slice(h * RET_D, (h + 1) * RET_D)

    @pl.when(pl.program_id(1) == 0)
    def _():
        lg = _log_sigmoid(dl_ref[...])
        pos = lax.broadcasted_iota(jnp.int32, (C, 1), 0).astype(F32)
        posl = lax.broadcasted_iota(jnp.int32, (SUBLANES, C), 1).astype(F32)
        diff = (lax.broadcasted_iota(jnp.int32, (C, C), 1)
                - lax.broadcasted_iota(jnp.int32, (C, C), 0)).astype(F32)
        for h in heads:
            lgf, lgb = lg[0, h, 0:1, :], lg[1, h, 0:1, :]
            lgf1, lgb1 = lgf[:, 0:1], lgb[:, 0:1]
            kdec_scr[h, 0] = jnp.exp(lgf * (C - 1.0 - pos))
            kdec_scr[h, 1] = jnp.exp(lgb * pos)
            qdec_scr[h, 0] = jnp.exp(lgf1 * (posl + 1.0))
            qdec_scr[h, 1] = jnp.exp(lgb1 * (C - posl))
            cdec_scr[h, 0] = jnp.broadcast_to(jnp.exp(lgf * C), (SUBLANES, LANES))
            cdec_scr[h, 1] = jnp.broadcast_to(jnp.exp(lgb * C), (SUBLANES, LANES))
            decay_scr[h] = jnp.where(diff >= 0.0, jnp.exp(lgf1 * jnp.maximum(diff, 0.0)),
                                     jnp.exp(lgb1 * jnp.maximum(-diff, 0.0)))

    nt = (((1,), (1,)), ((), ()))
    chunk = lambda n: slice(n * C, (n + 1) * C)

    rb = [sb_ref[0, h] for h in heads]
    for n in reversed(range(nch)):
        for h in heads:
            kc = k_ref[0, chunk(n), lanes(h)].astype(F32)
            kd = jnp.concatenate([(kc * kdec_scr[h, 0]).astype(BF16), (kc * kdec_scr[h, 1]).astype(BF16)],
                                 axis=1)
            u = jnp.dot(vt_ref[0, lanes(h), chunk(n)], kd, preferred_element_type=F32)
            uf_scr[h, n] = u[:, 0:RET_D]
            rb_scr[h, n] = rb[h].astype(BF16)
            rb[h] = rb[h] * cdec_scr[h, 1, 0:1, :] + u[:, RET_D:2 * RET_D]

    def kq(h, n):
        return lax.dot_general(k_ref[0, chunk(n), lanes(h)], q_ref[0, chunk(n), lanes(h)], nt,
                               preferred_element_type=F32)

    rf = [sf_ref[0, h] for h in heads]
    s_next = [kq(h, 0) for h in heads]
    for n in range(nch):
        for h in heads:
            s = s_next[h]
            q = q_ref[0, chunk(n), lanes(h)]
            states = jnp.concatenate([rf[h].astype(BF16), rb_scr[h, n]], axis=0)
            cross = lax.dot_general(states, q, nt, preferred_element_type=F32)
            if n + 1 < nch:
                s_next[h] = kq(h, n + 1)
                rf[h] = rf[h] * cdec_scr[h, 0, 0:1, :] + uf_scr[h, n]
            inner = jnp.dot(vt_ref[0, lanes(h), chunk(n)], (s * decay_scr[h]).astype(BF16),
                            preferred_element_type=F32)
            y = (inner + cross[0:RET_D] * qdec_scr[h, 0, 0:1, :]
                 + cross[RET_D:2 * RET_D] * qdec_scr[h, 1, 0:1, :])
            mu = jnp.mean(y, axis=0, keepdims=True)
            yc = y - mu
            var = jnp.mean(yc * yc, axis=0, keepdims=True)
            yn = (yc * lax.rsqrt(var + EPS)).T * gn_ref[h]
            o_ref[0, chunk(n), lanes(h)] = (yn * g_ref[0, chunk(n), lanes(h)].astype(F32)).astype(BF16)


def _retention(r, rvt, sf, sb, dl, gn):
    B, L, _ = r.shape
    P, C = RET_PAIR, RET_CHUNK
    col = lambda off: pl.BlockSpec((1, L, P * RET_D), lambda hp, b: (b, 0, off + hp))
    st_spec = pl.BlockSpec((1, P, RET_D, RET_D), lambda hp, b: (b, hp, 0, 0))
    npair = RET_HEADS // P
    return pl.pallas_call(
        _ret_kernel,
        out_shape=jax.ShapeDtypeStruct((B, L, RET_W), BF16),
        grid=(npair, B),
        in_specs=[col(0), col(npair),
                  pl.BlockSpec((1, P * RET_D, L), lambda hp, b: (b, hp, 0)),
                  col(2 * npair),
                  st_spec, st_spec,
                  pl.BlockSpec((2, P, SUBLANES, LANES), lambda hp, b: (0, hp, 0, 0)),
                  pl.BlockSpec((P, 1, RET_D), lambda hp, b: (hp, 0, 0))],
        out_specs=pl.BlockSpec((1, L, P * RET_D), lambda hp, b: (b, 0, hp)),
        scratch_shapes=[pltpu.VMEM((P, L // C, RET_D, RET_D), F32),
                        pltpu.VMEM((P, L // C, RET_D, RET_D), BF16),
                        pltpu.VMEM((P, 2, C, LANES), F32),
                        pltpu.VMEM((P, 2, SUBLANES, C), F32),
                        pltpu.VMEM((P, 2, SUBLANES, LANES), F32),
                        pltpu.VMEM((P, C, C), F32)],
        compiler_params=pltpu.CompilerParams(
            dimension_semantics=("arbitrary", "arbitrary"), vmem_limit_bytes=VMEM_LIMIT),
        name="ret",
    )(r, r, rvt, r, sf, sb, dl, gn)


def _att_kernel(q_ref, k_ref, v_ref, ck_ref, cv_ref, g_ref, sink_ref, r_ref, x_ref, m_ref, w_ref, fn_ref,
                o_ref, w_scr):
    @pl.when((pl.program_id(0) == 0) & (pl.program_id(1) == 0))
    def _():
        w_scr[0:RET_W, :] = w_ref[0:RET_W, :].astype(BF16)
        for j in range(ATT_HEADS // ATT_KV):
            for half in range(ATT_KV):
                dst = RET_W + j * LANES + half * ATT_DH
                src = RET_W + (j + half * (ATT_HEADS // ATT_KV)) * ATT_DH
                w_scr[dst:dst + ATT_DH, :] = w_ref[src:src + ATT_DH, :].astype(BF16)

    L = k_ref.shape[2]
    nb_total = L // ATT_BLOCK
    nb_step = q_ref.shape[1] // ATT_BLOCK
    nhb = ATT_W // LANES
    i = pl.program_id(1)
    r_i = lax.broadcasted_iota(jnp.int32, (ATT_BLOCK, ATT_BLOCK), 0)
    q_i = lax.broadcasted_iota(jnp.int32, (ATT_BLOCK, ATT_BLOCK), 1)
    band_prev = jnp.where(q_i <= r_i, 0.0, NEG).astype(F32)
    band_next = jnp.where(r_i <= q_i, 0.0, NEG).astype(F32)
    nt = (((1,), (1,)), ((), ()))
    tn = (((0,), (0,)), ((), ()))

    def key_rows(jb):
        n = i * nb_step + jb
        blk = lambda t: pl.ds(pl.multiple_of(t * ATT_BLOCK, ATT_BLOCK), ATT_BLOCK)
        return n, (blk(jnp.maximum(n - 1, 0)), blk(n), blk(jnp.minimum(n + 1, nb_total - 1)))

    def scores(jb, kv):
        n, (p_rows, o_rows, n_rows) = key_rows(jb)
        q = q_ref[0, jb * ATT_BLOCK:(jb + 1) * ATT_BLOCK, :]
        qall = jnp.concatenate([q[:, j * LANES:(j + 1) * LANES] for j in range(nhb)], axis=0)
        kall = jnp.concatenate([ck_ref[0, kv], k_ref[0, kv, p_rows, :], k_ref[0, kv, o_rows, :],
                                k_ref[0, kv, n_rows, :]], axis=0)
        return lax.dot_general(kall, qall, nt, preferred_element_type=F32)

    def attend(jb, kv, s):
        n, (p_rows, o_rows, n_rows) = key_rows(jb)
        lc = ck_ref.shape[2]
        bias_prev = jnp.concatenate([band_prev + jnp.where(n == 0, NEG, 0.0)] * nhb, axis=1)
        bias_next = jnp.concatenate([band_next + jnp.where(n == nb_total - 1, NEG, 0.0)] * nhb, axis=1)
        parts = [s[0:lc], s[lc:lc + ATT_BLOCK] + bias_prev, s[lc + ATT_BLOCK:lc + 2 * ATT_BLOCK],
                 s[lc + 2 * ATT_BLOCK:] + bias_next]
        sink = sink_ref[kv:kv + 1, :] * LOG2E
        m = sink
        for t in parts:
            m = jnp.maximum(m, jnp.max(t, axis=0, keepdims=True))
        p = jnp.concatenate([jnp.exp2(t - m).astype(BF16) for t in parts], axis=0)
        vaug = jnp.concatenate([cv_ref[0, kv], v_ref[0, kv, p_rows, :], v_ref[0, kv, o_rows, :],
                                v_ref[0, kv, n_rows, :]], axis=0)
        o = lax.dot_general(vaug, p, tn, preferred_element_type=F32)
        if kv == 0:
            val, ones = slice(0, ATT_DH), slice(ATT_DH, ATT_DH + 1)
        else:
            val, ones = slice(ATT_DH, 2 * ATT_DH), slice(0, 1)
        den = o[ones] + jnp.exp2(sink - m)
        return o[val] * (1.0 / den)

    def gated_block(jb, halves):
        rows = slice(jb * ATT_BLOCK, (jb + 1) * ATT_BLOCK)
        comb = jnp.concatenate(halves, axis=0)
        return jnp.concatenate(
            [(comb[:, j * LANES:(j + 1) * LANES].T
              * g_ref[0, rows, j * LANES:(j + 1) * LANES].astype(F32)).astype(BF16) for j in range(nhb)], axis=1)

    def finish(jb, att_blk):
        rows = slice(jb * ATT_BLOCK, (jb + 1) * ATT_BLOCK)
        mix_in = jnp.concatenate([r_ref[0, rows, :], att_blk], axis=1)
        mixed = jnp.dot(mix_in, w_scr[...], preferred_element_type=F32)
        xn = x_ref[0, rows, :] + m_ref[0, :, 2 * D_MODEL:3 * D_MODEL] * mixed
        o_ref[0, rows, :] = xn * lax.rsqrt(jnp.mean(xn * xn, axis=-1, keepdims=True) + EPS) * fn_ref[...]

    chains = [(jb, kv) for jb in range(nb_step) for kv in range(ATT_KV)]
    ahead = [scores(*chains[t]) for t in range(ATT_AHEAD)]
    halves, pending = [], []
    for c, (jb, kv) in enumerate(chains):
        s = ahead.pop(0)
        if c + ATT_AHEAD < len(chains):
            ahead.append(scores(*chains[c + ATT_AHEAD]))
        halves.append(attend(jb, kv, s))
        if pending:
            finish(*pending.pop())
        if kv == ATT_KV - 1:
            pending.append((jb, gated_block(jb, halves)))
            halves = []
    finish(*pending.pop())


def _attention_output(aq, ak, av, ck, cv, ag, sink2, ret, x, mod3, w_out_l, fnw):
    B, L, _ = aq.shape
    lc = ck.shape[2]
    tq = ATT_ROWS
    row = lambda n: pl.BlockSpec((1, tq, n), lambda b, i: (b, i, 0))
    full = lambda n: pl.BlockSpec((1, ATT_KV, n, LANES), lambda b, i: (b, 0, 0, 0))
    return pl.pallas_call(
        _att_kernel,
        out_shape=jax.ShapeDtypeStruct((B, L, D_MODEL), F32),
        grid=(B, L // tq),
        in_specs=[row(ATT_W), full(L), full(L), full(lc), full(lc), row(ATT_W),
                  pl.BlockSpec(sink2.shape, lambda b, i: (0, 0)),
                  row(RET_W), row(D_MODEL),
                  pl.BlockSpec((1, 1, 3 * D_MODEL), lambda b, i: (b, 0, 0)),
                  pl.BlockSpec((RET_W + ATT_W, D_MODEL), lambda b, i: (0, 0), pipeline_mode=pl.Buffered(1)),
                  pl.BlockSpec((1, D_MODEL), lambda b, i: (0, 0))],
        out_specs=row(D_MODEL),
        scratch_shapes=[pltpu.VMEM((RET_W + ATT_W, D_MODEL), BF16)],
        compiler_params=pltpu.CompilerParams(
            dimension_semantics=("arbitrary", "arbitrary"), vmem_limit_bytes=VMEM_LIMIT),
        name="att",
    )(aq, ak, av, ck, cv, ag, sink2, ret, x, mod3, w_out_l, fnw)


def _rope_tables(L):
    pos = np.arange(L)
    rows, cols = (pos // GRID_W).astype(np.float64), (pos % GRID_W).astype(np.float64)

    def tables(dh):
        nf = dh // 4
        inv = ROPE_BASE ** (-np.arange(nf, dtype=np.float64) / nf)
        ang = np.concatenate([rows[:, None] * inv, cols[:, None] * inv], axis=-1)
        cos, sin = np.cos(ang), np.sin(ang)
        reps = LANES // dh
        return (jnp.asarray(np.tile(np.concatenate([cos, cos], axis=-1), (1, reps)), F32),
                jnp.asarray(np.tile(np.concatenate([-sin, sin], axis=-1), (1, reps)), F32))

    cr, sr = tables(RET_D)
    ca, sa = tables(ATT_DH)
    return cr, sr, ca, sa


def kernel(x, c, ctx, c_ctx, w_ada, b_ada, w_in, ret_decay_logit, ret_gn_w, att_sink, w_out, final_norm_w):
    B, L, _ = x.shape
    assert w_ada.shape[0] == 1, "single-layer trunk"

    mod3 = _modulation(c, c_ctx, w_ada[0], b_ada[0][None, :])

    dl = jnp.broadcast_to(ret_decay_logit[0][:, :, None, None], (2, RET_HEADS, SUBLANES, LANES))

    r, rvt, aq, ak, av, ag, sf, sb, ck, cv = _project(x, ctx, mod3, w_in[0], _rope_tables(L), dl)
    ret = _retention(r, rvt, sf, sb, dl, ret_gn_w[0].reshape(RET_HEADS, 1, RET_D))

    sink2 = jnp.repeat(att_sink[0].reshape(ATT_KV, ATT_HEADS // ATT_KV), LANES, axis=1)
    return _attention_output(aq, ak, av, ck, cv, ag, sink2, ret, x, mod3, w_out[0], final_norm_w[None, :])
```
